```python
import math
import jax, jax.numpy as jnp
from jax import lax
import numpy as np

D_MODEL = 1024
BATCH = 8
SEQ = 2048
DEPTH = 1

GRID_W = 64
NA_HEADS = 8
NA_HEAD_DIM = 64
NA_WIN_ROWS = 8
NA_WIN_COLS = 16
DIFF_HEADS = 4
DIFF_QK_DIM = 64
DIFF_V_DIM = 2 * DIFF_QK_DIM
Q_BLOCK = 128
MEM_TOKENS = 256
MEM_HEADS = 4
MEM_HEAD_DIM = 128
D_FF = 2816
N_BRANCH = 3
NORM_EPS = 1e-6

NA_WIDTH = NA_HEADS * NA_HEAD_DIM
DIFF_QK_WIDTH = DIFF_HEADS * 2 * DIFF_QK_DIM
DIFF_V_WIDTH = DIFF_HEADS * DIFF_V_DIM
MEM_WIDTH = MEM_HEADS * MEM_HEAD_DIM
IN_WIDTH = 3 * NA_WIDTH + 2 * DIFF_QK_WIDTH + DIFF_V_WIDTH + MEM_WIDTH

kernel_name = "hybrid_gated_na_diffattn_memory_macaron"


def rmsnorm(x, g):
    xf = x.astype(jnp.float32)
    y = xf * lax.rsqrt(jnp.mean(xf * xf, axis=-1, keepdims=True) + NORM_EPS)
    return (y * g.astype(jnp.float32)).astype(x.dtype)


def swiglu(x, w_gate, w_up, w_down):
    return (jax.nn.silu(x @ w_gate) * (x @ w_up)) @ w_down


def neighbourhood_attention(q, k, v, rpb):
    b, s, h, d = q.shape
    rows = s // GRID_W
    wr = min(NA_WIN_ROWS, rows)
    wc = min(NA_WIN_COLS, GRID_W)
    to_grid = lambda t: t.reshape(b, rows, GRID_W, h, d).transpose(0, 3, 1, 2, 4)
    qg, kg, vg = to_grid(q), to_grid(k), to_grid(v)
    cols = jnp.arange(GRID_W)
    col_start = jnp.clip(cols - wc // 2, 0, GRID_W - wc)
    col_idx = col_start[:, None] + jnp.arange(wc)[None, :]
    dc = col_idx - cols[:, None] + (NA_WIN_COLS - 1)
    row_ids = jnp.arange(rows)
    row_start = jnp.clip(row_ids - wr // 2, 0, rows - wr)
    scale = d ** -0.5

    def one_row(args):
        q_row, r, rs = args
        k_band = lax.dynamic_slice_in_dim(kg, rs, wr, axis=2)
        v_band = lax.dynamic_slice_in_dim(vg, rs, wr, axis=2)
        k_win = k_band[:, :, :, col_idx, :]
        v_win = v_band[:, :, :, col_idx, :]
        logits = jnp.einsum('bhqd,bhrqcd->bhqrc', q_row, k_win).astype(jnp.float32) * scale
        dr = rs + jnp.arange(wr) - r + (NA_WIN_ROWS - 1)
        bias = rpb[:, dr[None, :, None], dc[:, None, :]]
        logits = logits + bias.astype(jnp.float32)[None]
        p = jax.nn.softmax(logits.reshape(b, h, GRID_W, wr * wc), axis=-1)
        p = p.reshape(b, h, GRID_W, wr, wc).astype(v.dtype)
        return jnp.einsum('bhqrc,bhrqcd->bhqd', p, v_win)

    out = lax.map(one_row, (qg.transpose(2, 0, 1, 3, 4), row_ids, row_start))
    return out.transpose(1, 0, 3, 2, 4).reshape(b, s, h * d)


def differential_attention(q1, q2, k1, k2, v, lam, slopes):
    b, h, s, dk = q1.shape
    dv = v.shape[-1]
    nb = s // Q_BLOCK
    scale = dk ** -0.5
    kpos = jnp.arange(s).astype(jnp.float32)

    def one_block(args):
        i, q1b, q2b = args
        qpos = (i * Q_BLOCK + jnp.arange(Q_BLOCK)).astype(jnp.float32)
        alibi = -slopes[:, None, None] * jnp.abs(qpos[:, None] - kpos[None, :])[None]
        p1 = jax.nn.softmax(jnp.einsum('bhqd,bhkd->bhqk', q1b, k1).astype(jnp.float32) * scale + alibi, axis=-1)
        p2 = jax.nn.softmax(jnp.einsum('bhqd,bhkd->bhqk', q2b, k2).astype(jnp.float32) * scale + alibi, axis=-1)
        w = (p1 - lam * p2).astype(v.dtype)
        return jnp.einsum('bhqk,bhkd->bhqd', w, v)

    blk = lambda t: t.reshape(b, h, nb, Q_BLOCK, dk).transpose(2, 0, 1, 3, 4)
    out = lax.map(one_block, (jnp.arange(nb), blk(q1), blk(q2)))
    return out.transpose(1, 2, 0, 3, 4).reshape(b, h, s, dv)


def memory_attention(q, k, v):
    b, s, h, d = q.shape
    logits = jnp.einsum('bshd,bmhd->bhsm', q, k).astype(jnp.float32) * (d ** -0.5)
    p = jax.nn.softmax(logits, axis=-1).astype(v.dtype)
    return jnp.einsum('bhsm,bmhd->bshd', p, v).reshape(b, s, h * d)


def setup_inputs(seed: int = 0) -> dict:
    key = jax.random.key(seed)
    ks = jax.random.split(key, 32)
    f32 = jnp.float32
    nrm = lambda k, shape, scale: jax.random.normal(k, shape, f32) * scale
    gain = lambda k, shape: 1.0 + 0.05 * jax.random.normal(k, shape, f32)
    L, D = DEPTH, D_MODEL
    return {
        "x": jax.random.normal(ks[0], (BATCH, SEQ, D), f32),
        "mem": jax.random.normal(ks[1], (BATCH, MEM_TOKENS, D), f32),
        "ffn1_norm": gain(ks[2], (L, D)),
        "ffn1_w_gate": nrm(ks[3], (L, D, D_FF), D ** -0.5),
        "ffn1_w_up": nrm(ks[4], (L, D, D_FF), D ** -0.5),
        "ffn1_w_down": nrm(ks[5], (L, D_FF, D), D_FF ** -0.5),
        "mix_norm": gain(ks[6], (L, D)),
        "w_in": nrm(ks[7], (L, D, IN_WIDTH), D ** -0.5),
        "na_rpb": nrm(ks[8], (L, NA_HEADS, 2 * NA_WIN_ROWS - 1, 2 * NA_WIN_COLS - 1), 0.1),
        "diff_lambda_q1": nrm(ks[9], (L, DIFF_QK_DIM), 0.1),
        "diff_lambda_k1": nrm(ks[10], (L, DIFF_QK_DIM), 0.1),
        "diff_lambda_q2": nrm(ks[11], (L, DIFF_QK_DIM), 0.1),
        "diff_lambda_k2": nrm(ks[12], (L, DIFF_QK_DIM), 0.1),
        "diff_subln": gain(ks[13], (L, DIFF_V_DIM)),
        "mem_norm": gain(ks[14], (L, D)),
        "w_mem_kv": nrm(ks[15], (L, D, 2 * MEM_WIDTH), D ** -0.5),
        "w_gate": nrm(ks[16], (L, D, N_BRANCH * D), D ** -0.5),
        "b_gate": nrm(ks[17], (L, N_BRANCH * D), 0.01),
        "w_br_na": nrm(ks[18], (L, NA_WIDTH, D), NA_WIDTH ** -0.5),
        "w_br_diff": nrm(ks[19], (L, DIFF_V_WIDTH, D), DIFF_V_WIDTH ** -0.5),
        "w_br_mem": nrm(ks[20], (L, MEM_WIDTH, D), MEM_WIDTH ** -0.5),
        "w_out": nrm(ks[21], (L, D, D), D ** -0.5),
        "ffn2_norm": gain(ks[22], (L, D)),
        "ffn2_w_gate": nrm(ks[23], (L, D, D_FF), D ** -0.5),
        "ffn2_w_up": nrm(ks[24], (L, D, D_FF), D ** -0.5),
        "ffn2_w_down": nrm(ks[25], (L, D_FF, D), D_FF ** -0.5),
        "final_norm": gain(ks[26], (D,)),
    }


def reference(x, mem, ffn1_norm, ffn1_w_gate, ffn1_w_up, ffn1_w_down, mix_norm, w_in, na_rpb,
              diff_lambda_q1, diff_lambda_k1, diff_lambda_q2, diff_lambda_k2, diff_subln,
              mem_norm, w_mem_kv, w_gate, b_gate, w_br_na, w_br_diff, w_br_mem, w_out,
              ffn2_norm, ffn2_w_gate, ffn2_w_up, ffn2_w_down, final_norm):
    b, s, d_model = x.shape
    m = mem.shape[1]
    slopes = jnp.asarray([2.0 ** (-8.0 * (i + 1) / DIFF_HEADS) for i in range(DIFF_HEADS)], jnp.float32)
    o_nq = 0
    o_nk = o_nq + NA_WIDTH
    o_nv = o_nk + NA_WIDTH
    o_dq = o_nv + NA_WIDTH
    o_dk = o_dq + DIFF_QK_WIDTH
    o_dv = o_dk + DIFF_QK_WIDTH
    o_mq = o_dv + DIFF_V_WIDTH

    for l in range(DEPTH):
        x = x + 0.5 * swiglu(rmsnorm(x, ffn1_norm[l]), ffn1_w_gate[l], ffn1_w_up[l], ffn1_w_down[l])

        h = rmsnorm(x, mix_norm[l])
        proj = h @ w_in[l]

        na_shape = (b, s, NA_HEADS, NA_HEAD_DIM)
        na_q = proj[..., o_nq:o_nk].reshape(na_shape)
        na_k = proj[..., o_nk:o_nv].reshape(na_shape)
        na_v = proj[..., o_nv:o_dq].reshape(na_shape)
        y_na = neighbourhood_attention(na_q, na_k, na_v, na_rpb[l]) @ w_br_na[l]

        dq = proj[..., o_dq:o_dk].reshape(b, s, DIFF_HEADS, 2, DIFF_QK_DIM).transpose(3, 0, 2, 1, 4)
        dk = proj[..., o_dk:o_dv].reshape(b, s, DIFF_HEADS, 2, DIFF_QK_DIM).transpose(3, 0, 2, 1, 4)
        dv = proj[..., o_dv:o_mq].reshape(b, s, DIFF_HEADS, DIFF_V_DIM).transpose(0, 2, 1, 3)
        lam_init = 0.8 - 0.6 * math.exp(-0.3 * l)
        lam = (jnp.exp(jnp.sum(diff_lambda_q1[l].astype(jnp.float32) * diff_lambda_k1[l].astype(jnp.float32)))
               - jnp.exp(jnp.sum(diff_lambda_q2[l].astype(jnp.float32) * diff_lambda_k2[l].astype(jnp.float32)))
               + lam_init)
        o_diff = differential_attention(dq[0], dq[1], dk[0], dk[1], dv, lam, slopes)
        o_diff = rmsnorm(o_diff, diff_subln[l]) * (1.0 - lam_init)
        o_diff = o_diff.transpose(0, 2, 1, 3).reshape(b, s, DIFF_V_WIDTH)
        y_diff = o_diff @ w_br_diff[l]

        mq = proj[..., o_mq:].reshape(b, s, MEM_HEADS, MEM_HEAD_DIM)
        mkv = rmsnorm(mem, mem_norm[l]) @ w_mem_kv[l]
        mk = mkv[..., :MEM_WIDTH].reshape(b, m, MEM_HEADS, MEM_HEAD_DIM)
        mv = mkv[..., MEM_WIDTH:].reshape(b, m, MEM_HEADS, MEM_HEAD_DIM)
        y_mem = memory_attention(mq, mk, mv) @ w_br_mem[l]

        g = jax.nn.sigmoid((h @ w_gate[l] + b_gate[l]).astype(jnp.float32)).astype(h.dtype)
        g = g.reshape(b, s, N_BRANCH, d_model)
        merged = g[:, :, 0] * y_na + g[:, :, 1] * y_diff + g[:, :, 2] * y_mem
        x = x + merged @ w_out[l]

        x = x + 0.5 * swiglu(rmsnorm(x, ffn2_norm[l]), ffn2_w_gate[l], ffn2_w_up[l], ffn2_w_down[l])

    return rmsnorm(x, final_norm)
```

```python
import functools
import math

import jax
import jax.numpy as jnp
from jax import lax
from jax.experimental import pallas as pl
from jax.experimental.pallas import tpu as pltpu

F32 = jnp.float32
BF16 = jnp.bfloat16

D_MODEL = 1024
GRID_W = 64
NA_HEADS = 8
NA_HEAD_DIM = 64
NA_WIN_ROWS = 8
NA_WIN_COLS = 16
DIFF_HEADS = 4
DIFF_QK_DIM = 64
DIFF_V_DIM = 128
MEM_HEADS = 4
MEM_HEAD_DIM = 128
D_FF = 2816
NORM_EPS = 1e-6
NA_WIDTH = NA_HEADS * NA_HEAD_DIM
DIFF_QK_WIDTH = DIFF_HEADS * 2 * DIFF_QK_DIM
DIFF_V_WIDTH = DIFF_HEADS * DIFF_V_DIM
MEM_WIDTH = MEM_HEADS * MEM_HEAD_DIM
IN_WIDTH = 3 * NA_WIDTH + 2 * DIFF_QK_WIDTH + DIFF_V_WIDTH + MEM_WIDTH
O_NQ = 0
O_NK = O_NQ + NA_WIDTH
O_NV = O_NK + NA_WIDTH
O_DQ = O_NV + NA_WIDTH
O_DK = O_DQ + DIFF_QK_WIDTH
O_DV = O_DK + DIFF_QK_WIDTH
O_MQ = O_DV + DIFF_V_WIDTH

LANES = 128
V7X_VMEM_LIMIT_BYTES = 56 * 1024 * 1024

FFN_TM = 512
FFN_TF = 256
PROJ_TM = 512
PROJ_TN = 512
DIFF_TQ = 256
MERGE_TM = 256
NEG_BIG = -1e30


def _rmsnorm_f32(x, g):
    ms = jnp.mean(x * x, axis=-1, keepdims=True)
    return (x * lax.rsqrt(ms + NORM_EPS)) * g


def _softmax_rows(logits):
    m = jnp.max(logits, axis=-1, keepdims=True)
    e = jnp.exp(logits - m)
    s = jnp.sum(e, axis=-1, keepdims=True)
    return e * (1.0 / s)


def _dot_nt(a, b):
    return lax.dot_general(a, b, (((1,), (1,)), ((), ())), preferred_element_type=F32)


def _dot(a, b):
    return jnp.dot(a, b, preferred_element_type=F32)


def _resident(shape):
    nd = len(shape)
    return pl.BlockSpec(shape, lambda *_: (0,) * nd, pipeline_mode=pl.Buffered(1))


def _ffn_body(x_ref, g_ref, wg_ref, wu_ref, wd_ref, fg_ref, o_ref, *, final_norm):
    x = x_ref[...]
    h = _rmsnorm_f32(x, g_ref[...]).astype(BF16)
    acc = jnp.zeros(x.shape, F32)
    for c in range(D_FF // FFN_TF):
        sl = slice(c * FFN_TF, (c + 1) * FFN_TF)
        gate = _dot(h, wg_ref[:, sl])
        up = _dot(h, wu_ref[:, sl])
        act = (gate * jax.nn.sigmoid(gate) * up).astype(BF16)
        acc = acc + _dot(act, wd_ref[sl, :])
    y = x + 0.5 * acc
    if final_norm:
        y = _rmsnorm_f32(y, fg_ref[...])
    o_ref[...] = y


def _ffn(x, norm_g, wg, wu, wd, final_g, *, final_norm):
    n = x.shape[0]
    row = lambda i: (i, 0)
    return pl.pallas_call(
        functools.partial(_ffn_body, final_norm=final_norm),
        grid=(n // FFN_TM,),
        in_specs=[
            pl.BlockSpec((FFN_TM, D_MODEL), row),
            _resident((1, D_MODEL)),
            _resident((D_MODEL, D_FF)),
            _resident((D_MODEL, D_FF)),
            _resident((D_FF, D_MODEL)),
            _resident((1, D_MODEL)),
        ],
        out_specs=pl.BlockSpec((FFN_TM, D_MODEL), row),
        out_shape=jax.ShapeDtypeStruct((n, D_MODEL), F32),
        compiler_params=pltpu.CompilerParams(
            dimension_semantics=("arbitrary",), vmem_limit_bytes=V7X_VMEM_LIMIT_BYTES),
        name="ffn",
    )(x, norm_g, wg, wu, wd, final_g)


def _proj_body(x_ref, g_ref, w_ref, o_ref):
    h = _rmsnorm_f32(x_ref[...], g_ref[...]).astype(BF16)
    for c in range(IN_WIDTH // PROJ_TN):
        lo = c * PROJ_TN
        y = _dot(h, w_ref[:, lo:lo + PROJ_TN])
        if lo == O_NQ or lo == O_DQ:
            y = y * (NA_HEAD_DIM ** -0.5)
        o_ref[:, lo:lo + PROJ_TN] = y.astype(BF16)


def _proj(x1, norm_g, w_in):
    n = x1.shape[0]
    row = lambda i: (i, 0)
    return pl.pallas_call(
        _proj_body,
        grid=(n // PROJ_TM,),
        in_specs=[
            pl.BlockSpec((PROJ_TM, D_MODEL), row),
            _resident((1, D_MODEL)),
            _resident((D_MODEL, IN_WIDTH)),
        ],
        out_specs=pl.BlockSpec((PROJ_TM, IN_WIDTH), row),
        out_shape=jax.ShapeDtypeStruct((n, IN_WIDTH), BF16),
        compiler_params=pltpu.CompilerParams(
            dimension_semantics=("arbitrary",), vmem_limit_bytes=V7X_VMEM_LIMIT_BYTES),
        name="proj",
    )(x1, norm_g, w_in)


def _memkv_body(m_ref, g_ref, w_ref, k_ref, v_ref):
    h = _rmsnorm_f32(m_ref[...], g_ref[...]).astype(BF16)
    k_ref[...] = _dot(h, w_ref[:, :MEM_WIDTH]).astype(BF16)
    v_ref[...] = _dot(h, w_ref[:, MEM_WIDTH:]).astype(BF16)


def _memkv(mem2d, norm_g, w_kv, batch, m_tokens):
    row = lambda i: (i, 0)
    shp = jax.ShapeDtypeStruct((batch * m_tokens, MEM_WIDTH), BF16)
    return pl.pallas_call(
        _memkv_body,
        grid=(batch,),
        in_specs=[
            pl.BlockSpec((m_tokens, D_MODEL), row),
            _resident((1, D_MODEL)),
            _resident((D_MODEL, 2 * MEM_WIDTH)),
        ],
        out_specs=[pl.BlockSpec((m_tokens, MEM_WIDTH), row)] * 2,
        out_shape=[shp, shp],
        compiler_params=pltpu.CompilerParams(dimension_semantics=("arbitrary",)),
        name="memkv",
    )(mem2d, norm_g, w_kv)


RPB_ROWS = 2 * NA_WIN_ROWS - 1
RPB_COLS = 2 * NA_WIN_COLS - 1
NA_BAND_KEYS = NA_WIN_ROWS * GRID_W


def _na_body(rpb_ref, q_ref, k_ref, v_ref, o_ref, tile_ref, band_ref, *, rows):
    hp = pl.program_id(0)
    b = pl.program_id(1)
    wr = min(NA_WIN_ROWS, rows)

    @pl.when(b == 0)
    def _build_bias():
        c = lax.broadcasted_iota(jnp.int32, (GRID_W, LANES), 0)
        kc = lax.broadcasted_iota(jnp.int32, (GRID_W, LANES), 1) & (GRID_W - 1)
        rel = kc - c
        cs = jnp.clip(c - NA_WIN_COLS // 2, 0, GRID_W - NA_WIN_COLS)
        valid = (kc >= cs) & (kc < cs + NA_WIN_COLS)

        def tile_step(t, carry):
            hl = t // RPB_ROWS
            dr = t - hl * RPB_ROWS
            base = ((hp * 2 + hl) * RPB_ROWS + dr) * RPB_COLS
            acc = jnp.full((GRID_W, LANES), NEG_BIG, F32)
            for dc in range(RPB_COLS):
                acc = jnp.where(rel == dc - (NA_WIN_COLS - 1), rpb_ref[base + dc], acc)
            tile_ref[hl, dr] = jnp.where(valid, acc, NEG_BIG)
            return carry

        lax.fori_loop(0, 2 * RPB_ROWS, tile_step, 0)
        low = lax.broadcasted_iota(jnp.int32, (GRID_W, LANES), 1) < GRID_W
        for hl in range(2):
            for dr0 in range(NA_WIN_ROWS):
                for j in range(wr // 2):
                    band_ref[hl, dr0, :, j * LANES:(j + 1) * LANES] = jnp.where(
                        low, tile_ref[hl, dr0 + 2 * j], tile_ref[hl, dr0 + 2 * j + 1])

    lane = lax.broadcasted_iota(jnp.int32, (GRID_W, LANES), 1)
    low = lane < NA_HEAD_DIM

    def row_step(r, carry):
        rs = jnp.clip(r - wr // 2, 0, rows - wr)
        dr0 = rs - r + (NA_WIN_ROWS - 1)
        q2 = q_ref[pl.ds(pl.multiple_of(r * GRID_W, GRID_W), GRID_W), :]
        kb = k_ref[pl.ds(pl.multiple_of(rs * GRID_W, GRID_W), wr * GRID_W), :]
        vb = v_ref[pl.ds(pl.multiple_of(rs * GRID_W, GRID_W), wr * GRID_W), :]
        outs = []
        for hl in range(2):
            qm = jnp.where(low if hl == 0 else jnp.logical_not(low), q2, jnp.zeros_like(q2))
            logits = _dot_nt(qm, kb) + band_ref[hl, dr0]
            p = _softmax_rows(logits).astype(BF16)
            outs.append(_dot(p, vb))
        o = jnp.where(low, outs[0], outs[1])
        o_ref[pl.ds(pl.multiple_of(r * GRID_W, GRID_W), GRID_W), :] = o.astype(BF16)
        return carry

    lax.fori_loop(0, rows, row_step, 0)


def _na(proj, rpb_flat, batch, seq):
    rows = seq // GRID_W
    assert rows >= NA_WIN_ROWS and NA_WIN_ROWS % 2 == 0
    pairs = NA_HEADS // 2
    blk = lambda off: pl.BlockSpec((seq, LANES), lambda hp, b, off=off: (b, off + hp))
    return pl.pallas_call(
        functools.partial(_na_body, rows=rows),
        grid=(pairs, batch),
        in_specs=[
            pl.BlockSpec(memory_space=pltpu.SMEM),
            blk(O_NQ // LANES), blk(O_NK // LANES), blk(O_NV // LANES),
        ],
        out_specs=pl.BlockSpec((seq, LANES), lambda hp, b: (b, hp)),
        out_shape=jax.ShapeDtypeStruct((batch * seq, NA_WIDTH), BF16),
        scratch_shapes=[
            pltpu.VMEM((2, RPB_ROWS, GRID_W, LANES), F32),
            pltpu.VMEM((2, NA_WIN_ROWS, GRID_W, NA_BAND_KEYS), F32),
        ],
        compiler_params=pltpu.CompilerParams(
            dimension_semantics=("arbitrary", "arbitrary"), vmem_limit_bytes=V7X_VMEM_LIMIT_BYTES),
        name="na_attn",
    )(rpb_flat, proj, proj, proj)


def _diff_body(slope_ref, lq1_ref, lk1_ref, lq2_ref, lk2_ref, sub_ref, q_ref, k_ref, v_ref, o_ref,
               *, lam_init, seq):
    h = pl.program_id(0)
    qi = pl.program_id(2)
    slope = slope_ref[h]
    lam = (jnp.exp(jnp.sum(lq1_ref[...] * lk1_ref[...], axis=-1, keepdims=True))
           - jnp.exp(jnp.sum(lq2_ref[...] * lk2_ref[...], axis=-1, keepdims=True))
           + lam_init)
    q = q_ref[...]
    k = k_ref[...]
    lane = lax.broadcasted_iota(jnp.int32, q.shape, 1)
    zero = jnp.zeros_like(q)
    q1 = jnp.where(lane < DIFF_QK_DIM, q, zero)
    q2 = jnp.where(lane >= DIFF_QK_DIM, q, zero)
    qpos = (qi * DIFF_TQ + lax.broadcasted_iota(jnp.int32, (DIFF_TQ, seq), 0)).astype(F32)
    kpos = lax.broadcasted_iota(jnp.int32, (DIFF_TQ, seq), 1).astype(F32)
    alibi = (-slope) * jnp.abs(qpos - kpos)
    p1 = _softmax_rows(_dot_nt(q1, k) + alibi)
    p2 = _softmax_rows(_dot_nt(q2, k) + alibi)
    w = (p1 - lam * p2).astype(BF16)
    o = _dot(w, v_ref[...])
    o = _rmsnorm_f32(o, sub_ref[...]) * (1.0 - lam_init)
    o_ref[...] = o.astype(BF16)


def _diff(proj, slopes, lq1, lk1, lq2, lk2, subln, batch, seq, lam_init):
    nq = seq // DIFF_TQ
    vec = lambda w: pl.BlockSpec((1, w), lambda h, b, i: (0, 0))
    return pl.pallas_call(
        functools.partial(_diff_body, lam_init=lam_init, seq=seq),
        grid=(DIFF_HEADS, batch, nq),
        in_specs=[
            pl.BlockSpec(memory_space=pltpu.SMEM),
            vec(DIFF_QK_DIM), vec(DIFF_QK_DIM), vec(DIFF_QK_DIM), vec(DIFF_QK_DIM), vec(DIFF_V_DIM),
            pl.BlockSpec((DIFF_TQ, LANES), lambda h, b, i: (b * nq + i, O_DQ // LANES + h)),
            pl.BlockSpec((seq, LANES), lambda h, b, i: (b, O_DK // LANES + h)),
            pl.BlockSpec((seq, LANES), lambda h, b, i: (b, O_DV // LANES + h)),
        ],
        out_specs=pl.BlockSpec((DIFF_TQ, LANES), lambda h, b, i: (b * nq + i, h)),
        out_shape=jax.ShapeDtypeStruct((batch * seq, DIFF_V_WIDTH), BF16),
        compiler_params=pltpu.CompilerParams(
            dimension_semantics=("arbitrary", "arbitrary", "arbitrary"),
            vmem_limit_bytes=V7X_VMEM_LIMIT_BYTES),
        name="diff_attn",
    )(slopes, lq1, lk1, lq2, lk2, subln, proj, proj, proj)


def _merge_body(x_ref, mq_ref, ona_ref, odf_ref, mk_ref, mv_ref, g_ref, wgate_ref, bgate_ref,
                wna_ref, wdf_ref, wmem_ref, wout_ref, o_ref):
    x = x_ref[...]
    h = _rmsnorm_f32(x, g_ref[...]).astype(BF16)
    heads = []
    for hh in range(MEM_HEADS):
        sl = slice(hh * MEM_HEAD_DIM, (hh + 1) * MEM_HEAD_DIM)
        logits = _dot_nt(mq_ref[:, sl], mk_ref[:, sl]) * (MEM_HEAD_DIM ** -0.5)
        p = _softmax_rows(logits).astype(BF16)
        heads.append(_dot(p, mv_ref[:, sl]))
    o_mem = jnp.concatenate(heads, axis=-1).astype(BF16)
    branches = (
        _dot(ona_ref[...], wna_ref[...]),
        _dot(odf_ref[...], wdf_ref[...]),
        _dot(o_mem, wmem_ref[...]),
    )
    merged = jnp.zeros(x.shape, F32)
    for i, y in enumerate(branches):
        sl = slice(i * D_MODEL, (i + 1) * D_MODEL)
        gate = jax.nn.sigmoid(_dot(h, wgate_ref[:, sl]) + bgate_ref[:, sl])
        merged = merged + gate * y
    o_ref[...] = x + _dot(merged.astype(BF16), wout_ref[...])


def _merge(x1, proj, o_na, o_diff, mk, mv, norm_g, w_gate, b_gate, w_na, w_df, w_mem, w_out,
           seq, m_tokens):
    n = x1.shape[0]
    per_b = seq // MERGE_TM
    row = lambda i: (i, 0)
    return pl.pallas_call(
        _merge_body,
        grid=(n // MERGE_TM,),
        in_specs=[
            pl.BlockSpec((MERGE_TM, D_MODEL), row),
            pl.BlockSpec((MERGE_TM, MEM_WIDTH), lambda i: (i, O_MQ // MEM_WIDTH)),
            pl.BlockSpec((MERGE_TM, NA_WIDTH), row),
            pl.BlockSpec((MERGE_TM, DIFF_V_WIDTH), row),
            pl.BlockSpec((m_tokens, MEM_WIDTH), lambda i: (i // per_b, 0)),
            pl.BlockSpec((m_tokens, MEM_WIDTH), lambda i: (i // per_b, 0)),
            _resident((1, D_MODEL)),
            _resident((D_MODEL, 3 * D_MODEL)),
            _resident((1, 3 * D_MODEL)),
            _resident((NA_WIDTH, D_MODEL)),
            _resident((DIFF_V_WIDTH, D_MODEL)),
            _resident((MEM_WIDTH, D_MODEL)),
            _resident((D_MODEL, D_MODEL)),
        ],
        out_specs=pl.BlockSpec((MERGE_TM, D_MODEL), row),
        out_shape=jax.ShapeDtypeStruct((n, D_MODEL), F32),
        compiler_params=pltpu.CompilerParams(
            dimension_semantics=("arbitrary",), vmem_limit_bytes=V7X_VMEM_LIMIT_BYTES),
        name="merge",
    )(x1, proj, o_na, o_diff, mk, mv, norm_g, w_gate, b_gate, w_na, w_df, w_mem, w_out)


def kernel(x, mem, ffn1_norm, ffn1_w_gate, ffn1_w_up, ffn1_w_down, mix_norm, w_in, na_rpb,
           diff_lambda_q1, diff_lambda_k1, diff_lambda_q2, diff_lambda_k2, diff_subln,
           mem_norm, w_mem_kv, w_gate, b_gate, w_br_na, w_br_diff, w_br_mem, w_out,
           ffn2_norm, ffn2_w_gate, ffn2_w_up, ffn2_w_down, final_norm):
    batch, seq, d_model = x.shape
    m_tokens = mem.shape[1]
    depth = ffn1_norm.shape[0]
    assert d_model == D_MODEL and seq % GRID_W == 0
    assert seq % DIFF_TQ == 0 and seq % MERGE_TM == 0 and (batch * seq) % FFN_TM == 0
    slopes = jnp.asarray([2.0 ** (-8.0 * (i + 1) / DIFF_HEADS) for i in range(DIFF_HEADS)], F32)
    bf = lambda w: w.astype(BF16)
    vec = lambda v: v.reshape(1, -1).astype(F32)

    xt = x.reshape(batch * seq, d_model)
    mem2d = mem.reshape(batch * m_tokens, d_model)
    for l in range(depth):
        lam_init = 0.8 - 0.6 * math.exp(-0.3 * l)
        xt = _ffn(xt, vec(ffn1_norm[l]), bf(ffn1_w_gate[l]), bf(ffn1_w_up[l]), bf(ffn1_w_down[l]),
                  vec(final_norm), final_norm=False)
        proj = _proj(xt, vec(mix_norm[l]), bf(w_in[l]))
        mk, mv = _memkv(mem2d, vec(mem_norm[l]), bf(w_mem_kv[l]), batch, m_tokens)
        o_na = _na(proj, na_rpb[l].reshape(-1).astype(F32), batch, seq)
        o_diff = _diff(proj, slopes, vec(diff_lambda_q1[l]), vec(diff_lambda_k1[l]),
                       vec(diff_lambda_q2[l]), vec(diff_lambda_k2[l]), vec(diff_subln[l]),
                       batch, seq, lam_init)
        xt = _merge(xt, proj, o_na, o_diff, mk, mv, vec(mix_norm[l]), bf(w_gate[l]),
                    vec(b_gate[l]), bf(w_br_na[l]), bf(w_br_diff[l]), bf(w_br_mem[l]), bf(w_out[l]),
                    seq, m_tokens)
        last = l == depth - 1
        xt = _ffn(xt, vec(ffn2_norm[l]), bf(ffn2_w_gate[l]), bf(ffn2_w_up[l]), bf(ffn2_w_down[l]),
                  vec(final_norm), final_norm=last)
    return xt.reshape(batch, seq, d_model)
```

```python
import functools
import math

import jax
import jax.numpy as jnp
from jax import lax
from jax.experimental import pallas as pl
from jax.experimental.pallas import tpu as pltpu

F32 = jnp.float32
BF16 = jnp.bfloat16

D_MODEL = 1024
GRID_W = 64
NA_HEADS = 8
NA_HEAD_DIM = 64
NA_WIN_ROWS = 8
NA_WIN_COLS = 16
DIFF_HEADS = 4
DIFF_QK_DIM = 64
DIFF_V_DIM = 128
MEM_HEADS = 4
MEM_HEAD_DIM = 128
D_FF = 2816
NORM_EPS = 1e-6
NA_WIDTH = NA_HEADS * NA_HEAD_DIM
DIFF_QK_WIDTH = DIFF_HEADS * 2 * DIFF_QK_DIM
DIFF_V_WIDTH = DIFF_HEADS * DIFF_V_DIM
MEM_WIDTH = MEM_HEADS * MEM_HEAD_DIM
IN_WIDTH = 3 * NA_WIDTH + 2 * DIFF_QK_WIDTH + DIFF_V_WIDTH + MEM_WIDTH
O_NQ = 0
O_NK = O_NQ + NA_WIDTH
O_NV = O_NK + NA_WIDTH
O_DQ = O_NV + NA_WIDTH
O_DK = O_DQ + DIFF_QK_WIDTH
O_DV = O_DK + DIFF_QK_WIDTH
O_MQ = O_DV + DIFF_V_WIDTH

LANES = 128
V7X_VMEM_LIMIT_BYTES = 56 * 1024 * 1024

FFN_TM = 512
FFN_TF = 256
PROJ_TM = 512
PROJ_TN = 512
DIFF_TQ = 256
MERGE_TM = 256
NEG_BIG = -1e30


def _rmsnorm_f32(x, g):
    ms = jnp.mean(x * x, axis=-1, keepdims=True)
    return (x * lax.rsqrt(ms + NORM_EPS)) * g


def _softmax_rows(logits):
    m = jnp.max(logits, axis=-1, keepdims=True)
    e = jnp.exp(logits - m)
    s = jnp.sum(e, axis=-1, keepdims=True)
    return e * (1.0 / s)


def _dot_nt(a, b):
    return lax.dot_general(a, b, (((1,), (1,)), ((), ())), preferred_element_type=F32)


def _dot(a, b):
    return jnp.dot(a, b, preferred_element_type=F32)


def _resident(shape):
    nd = len(shape)
    return pl.BlockSpec(shape, lambda *_: (0,) * nd, pipeline_mode=pl.Buffered(1))


def _ffn_body(x_ref, g_ref, wg_ref, wu_ref, wd_ref, fg_ref, o_ref, *, final_norm):
    x = x_ref[...]
    h = _rmsnorm_f32(x, g_ref[...]).astype(BF16)
    acc = jnp.zeros(x.shape, F32)
    for c in range(D_FF // FFN_TF):
        sl = slice(c * FFN_TF, (c + 1) * FFN_TF)
        gate = _dot(h, wg_ref[:, sl])
        up = _dot(h, wu_ref[:, sl])
        act = (gate * jax.nn.sigmoid(gate) * up).astype(BF16)
        acc = acc + _dot(act, wd_ref[sl, :])
    y = x + 0.5 * acc
    if final_norm:
        y = _rmsnorm_f32(y, fg_ref[...])
    o_ref[...] = y


def _ffn(x, norm_g, wg, wu, wd, final_g, *, final_norm):
    n = x.shape[0]
    row = lambda i: (i, 0)
    return pl.pallas_call(
        functools.partial(_ffn_body, final_norm=final_norm),
        grid=(n // FFN_TM,),
        in_specs=[
            pl.BlockSpec((FFN_TM, D_MODEL), row),
            _resident((1, D_MODEL)),
            _resident((D_MODEL, D_FF)),
            _resident((D_MODEL, D_FF)),
            _resident((D_FF, D_MODEL)),
            _resident((1, D_MODEL)),
        ],
        out_specs=pl.BlockSpec((FFN_TM, D_MODEL), row),
        out_shape=jax.ShapeDtypeStruct((n, D_MODEL), F32),
        compiler_params=pltpu.CompilerParams(
            dimension_semantics=("arbitrary",), vmem_limit_bytes=V7X_VMEM_LIMIT_BYTES),
        name="ffn",
    )(x, norm_g, wg, wu, wd, final_g)


def _proj_body(x_ref, g_ref, w_ref, o_ref):
    h = _rmsnorm_f32(x_ref[...], g_ref[...]).astype(BF16)
    for c in range(IN_WIDTH // PROJ_TN):
        lo = c * PROJ_TN
        y = _dot(h, w_ref[:, lo:lo + PROJ_TN])
        if lo == O_NQ or lo == O_DQ:
            y = y * (NA_HEAD_DIM ** -0.5)
        o_ref[:, lo:lo + PROJ_TN] = y.astype(BF16)


def _proj(x1, norm_g, w_in):
    n = x1.shape[0]
    row = lambda i: (i, 0)
    return pl.pallas_call(
        _proj_body,
        grid=(n // PROJ_TM,),
        in_specs=[
            pl.BlockSpec((PROJ_TM, D_MODEL), row),
            _resident((1, D_MODEL)),
            _resident((D_MODEL, IN_WIDTH)),
        ],
        out_specs=pl.BlockSpec((PROJ_TM, IN_WIDTH), row),
        out_shape=jax.ShapeDtypeStruct((n, IN_WIDTH), BF16),
        compiler_params=pltpu.CompilerParams(
            dimension_semantics=("arbitrary",), vmem_limit_bytes=V7X_VMEM_LIMIT_BYTES),
        name="proj",
    )(x1, norm_g, w_in)


def _memkv_body(m_ref, g_ref, w_ref, k_ref, v_ref):
    h = _rmsnorm_f32(m_ref[...], g_ref[...]).astype(BF16)
    k_ref[...] = _dot(h, w_ref[:, :MEM_WIDTH]).astype(BF16)
    v_ref[...] = _dot(h, w_ref[:, MEM_WIDTH:]).astype(BF16)


def _memkv(mem2d, norm_g, w_kv, batch, m_tokens):
    row = lambda i: (i, 0)
    shp = jax.ShapeDtypeStruct((batch * m_tokens, MEM_WIDTH), BF16)
    return pl.pallas_call(
        _memkv_body,
        grid=(batch,),
        in_specs=[
            pl.BlockSpec((m_tokens, D_MODEL), row),
            _resident((1, D_MODEL)),
            _resident((D_MODEL, 2 * MEM_WIDTH)),
        ],
        out_specs=[pl.BlockSpec((m_tokens, MEM_WIDTH), row)] * 2,
        out_shape=[shp, shp],
        compiler_params=pltpu.CompilerParams(dimension_semantics=("arbitrary",)),
        name="memkv",
    )(mem2d, norm_g, w_kv)


RPB_ROWS = 2 * NA_WIN_ROWS - 1
RPB_COLS = 2 * NA_WIN_COLS - 1
NA_BAND_KEYS = NA_WIN_ROWS * GRID_W


def _na_body(rpb_ref, q_ref, k_ref, v_ref, o_ref, tile_ref, band_ref, lg_ref, p_ref, *, rows):
    hp = pl.program_id(0)
    b = pl.program_id(1)
    wr = min(NA_WIN_ROWS, rows)

    @pl.when(b == 0)
    def _build_bias():
        c = lax.broadcasted_iota(jnp.int32, (GRID_W, LANES), 0)
        kc = lax.broadcasted_iota(jnp.int32, (GRID_W, LANES), 1) & (GRID_W - 1)
        rel = kc - c
        cs = jnp.clip(c - NA_WIN_COLS // 2, 0, GRID_W - NA_WIN_COLS)
        valid = (kc >= cs) & (kc < cs + NA_WIN_COLS)

        def tile_step(t, carry):
            hl = t // RPB_ROWS
            dr = t - hl * RPB_ROWS
            base = ((hp * 2 + hl) * RPB_ROWS + dr) * RPB_COLS
            acc = jnp.full((GRID_W, LANES), NEG_BIG, F32)
            for dc in range(RPB_COLS):
                acc = jnp.where(rel == dc - (NA_WIN_COLS - 1), rpb_ref[base + dc], acc)
            tile_ref[hl, dr] = jnp.where(valid, acc, NEG_BIG)
            return carry

        lax.fori_loop(0, 2 * RPB_ROWS, tile_step, 0)
        low = lax.broadcasted_iota(jnp.int32, (GRID_W, LANES), 1) < GRID_W
        for hl in range(2):
            for dr0 in range(NA_WIN_ROWS):
                for j in range(wr // 2):
                    band_ref[dr0, hl * GRID_W:(hl + 1) * GRID_W, j * LANES:(j + 1) * LANES] = (
                        jnp.where(low, tile_ref[hl, dr0 + 2 * j], tile_ref[hl, dr0 + 2 * j + 1]))

    lane = lax.broadcasted_iota(jnp.int32, (GRID_W, LANES), 1)
    low = lane < NA_HEAD_DIM

    def band_rows(r):
        rs = jnp.clip(r - wr // 2, 0, rows - wr)
        return rs, pl.ds(pl.multiple_of(rs * GRID_W, GRID_W), wr * GRID_W)

    def query_rows(r):
        return pl.ds(pl.multiple_of(r * GRID_W, GRID_W), GRID_W)

    def logits_stage(r, slot):
        rs, keys = band_rows(r)
        q2 = q_ref[query_rows(r), :]
        zero = jnp.zeros_like(q2)
        q_st = jnp.concatenate([jnp.where(low, q2, zero), jnp.where(low, zero, q2)], axis=0)
        lg_ref[slot] = _dot_nt(q_st, k_ref[keys, :]) + band_ref[rs - r + (NA_WIN_ROWS - 1)]

    def softmax_stage(slot):
        p_ref[slot] = _softmax_rows(lg_ref[slot]).astype(BF16)

    def value_stage(r, slot):
        _, keys = band_rows(r)
        o_st = _dot(p_ref[slot], v_ref[keys, :])
        o = jnp.where(low, o_st[:GRID_W], o_st[GRID_W:])
        o_ref[query_rows(r), :] = o.astype(BF16)

    logits_stage(0, 0)
    logits_stage(1, 1)
    softmax_stage(0)

    def pair_step(t, carry):
        r = 1 + 2 * t
        logits_stage(r + 1, 0)
        softmax_stage(1)
        value_stage(r - 1, 0)
        logits_stage(r + 2, 1)
        softmax_stage(0)
        value_stage(r, 1)
        return carry

    lax.fori_loop(0, (rows - 2) // 2, pair_step, 0)
    softmax_stage(1)
    value_stage(rows - 2, 0)
    value_stage(rows - 1, 1)


def _na(proj, rpb_flat, batch, seq):
    rows = seq // GRID_W
    assert rows >= NA_WIN_ROWS and NA_WIN_ROWS % 2 == 0 and rows % 2 == 0 and rows >= 4
    pairs = NA_HEADS // 2
    blk = lambda off: pl.BlockSpec((seq, LANES), lambda hp, b, off=off: (b, off + hp))
    return pl.pallas_call(
        functools.partial(_na_body, rows=rows),
        grid=(pairs, batch),
        in_specs=[
            pl.BlockSpec(memory_space=pltpu.SMEM),
            blk(O_NQ // LANES), blk(O_NK // LANES), blk(O_NV // LANES),
        ],
        out_specs=pl.BlockSpec((seq, LANES), lambda hp, b: (b, hp)),
        out_shape=jax.ShapeDtypeStruct((batch * seq, NA_WIDTH), BF16),
        scratch_shapes=[
            pltpu.VMEM((2, RPB_ROWS, GRID_W, LANES), F32),
            pltpu.VMEM((NA_WIN_ROWS, 2 * GRID_W, NA_BAND_KEYS), F32),
            pltpu.VMEM((2, 2 * GRID_W, NA_BAND_KEYS), F32),
            pltpu.VMEM((2, 2 * GRID_W, NA_BAND_KEYS), BF16),
        ],
        compiler_params=pltpu.CompilerParams(
            dimension_semantics=("arbitrary", "arbitrary"), vmem_limit_bytes=V7X_VMEM_LIMIT_BYTES),
        name="na_attn",
    )(rpb_flat, proj, proj, proj)


def _diff_body(slope_ref, lq1_ref, lk1_ref, lq2_ref, lk2_ref, sub_ref, q_ref, k_ref, v_ref, o_ref,
               dist_ref, s_ref, m_ref, e_ref, v1_ref, *, lam_init, seq):
    h = pl.program_id(0)
    b = pl.program_id(1)
    nq = seq // DIFF_TQ

    @pl.when(b == 0)
    def _build_dist():
        slope = slope_ref[h]
        rel = (lax.broadcasted_iota(jnp.int32, (DIFF_TQ, DIFF_TQ), 0)
               - lax.broadcasted_iota(jnp.int32, (DIFF_TQ, DIFF_TQ), 1))
        for d in range(2 * nq - 1):
            off = (d - (nq - 1)) * DIFF_TQ
            dist_ref[d] = slope * jnp.abs((rel + off).astype(F32))

    lam = (jnp.exp(jnp.sum(lq1_ref[...] * lk1_ref[...], axis=-1, keepdims=True))
           - jnp.exp(jnp.sum(lq2_ref[...] * lk2_ref[...], axis=-1, keepdims=True))
           + lam_init)
    v = v_ref[...]
    ones_col = jnp.where(lax.broadcasted_iota(jnp.int32, v.shape, 1) == 0, 1.0, 0.0).astype(BF16)
    v1_ref[:, :DIFF_V_DIM] = v
    v1_ref[:, DIFF_V_DIM:] = ones_col
    lane = lax.broadcasted_iota(jnp.int32, (DIFF_TQ, LANES), 1)
    first_map = lane < DIFF_QK_DIM

    def q_rows(i):
        return pl.ds(pl.multiple_of(i * DIFF_TQ, DIFF_TQ), DIFF_TQ)

    def logits_stage(i, slot):
        q = q_ref[q_rows(i), :]
        zero = jnp.zeros_like(q)
        k = k_ref[...]
        dist = jnp.concatenate([dist_ref[i - c + (nq - 1)] for c in range(nq)], axis=1)
        for m, qm in enumerate((jnp.where(first_map, q, zero), jnp.where(first_map, zero, q))):
            s = _dot_nt(qm, k) - dist
            s_ref[slot, m] = s
            m_ref[slot, m] = jnp.max(s, axis=-1, keepdims=True)

    def exp_stage(slot):
        for m in range(2):
            e_ref[slot, m * DIFF_TQ:(m + 1) * DIFF_TQ, :] = (
                jnp.exp(s_ref[slot, m] - m_ref[slot, m]).astype(BF16))

    def value_stage(i, slot):
        acc = _dot(e_ref[slot], v1_ref[...])
        num1, den1 = acc[:DIFF_TQ, :DIFF_V_DIM], acc[:DIFF_TQ, DIFF_V_DIM:DIFF_V_DIM + 1]
        num2, den2 = acc[DIFF_TQ:, :DIFF_V_DIM], acc[DIFF_TQ:, DIFF_V_DIM:DIFF_V_DIM + 1]
        o = num1 * (1.0 / den1) - (lam * (1.0 / den2)) * num2
        o = _rmsnorm_f32(o, sub_ref[...]) * (1.0 - lam_init)
        o_ref[q_rows(i), :] = o.astype(BF16)

    logits_stage(0, 0)
    logits_stage(1, 1)
    exp_stage(0)

    def pair_step(t, carry):
        i = 1 + 2 * t
        logits_stage(i + 1, 0)
        exp_stage(1)
        value_stage(i - 1, 0)
        logits_stage(i + 2, 1)
        exp_stage(0)
        value_stage(i, 1)
        return carry

    lax.fori_loop(0, (nq - 2) // 2, pair_step, 0)
    exp_stage(1)
    value_stage(nq - 2, 0)
    value_stage(nq - 1, 1)


def _diff(proj, slopes, lq1, lk1, lq2, lk2, subln, batch, seq, lam_init):
    nq = seq // DIFF_TQ
    assert nq % 2 == 0 and nq >= 4
    vec = lambda w: pl.BlockSpec((1, w), lambda h, b: (0, 0))
    blk = lambda off: pl.BlockSpec((seq, LANES), lambda h, b, off=off: (b, off + h))
    return pl.pallas_call(
        functools.partial(_diff_body, lam_init=lam_init, seq=seq),
        grid=(DIFF_HEADS, batch),
        in_specs=[
            pl.BlockSpec(memory_space=pltpu.SMEM),
            vec(DIFF_QK_DIM), vec(DIFF_QK_DIM), vec(DIFF_QK_DIM), vec(DIFF_QK_DIM), vec(DIFF_V_DIM),
            blk(O_DQ // LANES), blk(O_DK // LANES), blk(O_DV // LANES),
        ],
        out_specs=pl.BlockSpec((seq, LANES), lambda h, b: (b, h)),
        out_shape=jax.ShapeDtypeStruct((batch * seq, DIFF_V_WIDTH), BF16),
        scratch_shapes=[
            pltpu.VMEM((2 * nq - 1, DIFF_TQ, DIFF_TQ), F32),
            pltpu.VMEM((2, 2, DIFF_TQ, seq), F32),
            pltpu.VMEM((2, 2, DIFF_TQ, 1), F32),
            pltpu.VMEM((2, 2 * DIFF_TQ, seq), BF16),
            pltpu.VMEM((seq, 2 * LANES), BF16),
        ],
        compiler_params=pltpu.CompilerParams(
            dimension_semantics=("arbitrary", "arbitrary"),
            vmem_limit_bytes=V7X_VMEM_LIMIT_BYTES),
        name="diff_attn",
    )(slopes, lq1, lk1, lq2, lk2, subln, proj, proj, proj)


def _merge_body(x_ref, mq_ref, ona_ref, odf_ref, mk_ref, mv_ref, g_ref, wgate_ref, bgate_ref,
                wna_ref, wdf_ref, wmem_ref, wout_ref, o_ref):
    x = x_ref[...]
    h = _rmsnorm_f32(x, g_ref[...]).astype(BF16)
    heads = []
    for hh in range(MEM_HEADS):
        sl = slice(hh * MEM_HEAD_DIM, (hh + 1) * MEM_HEAD_DIM)
        logits = _dot_nt(mq_ref[:, sl], mk_ref[:, sl]) * (MEM_HEAD_DIM ** -0.5)
        p = _softmax_rows(logits).astype(BF16)
        heads.append(_dot(p, mv_ref[:, sl]))
    o_mem = jnp.concatenate(heads, axis=-1).astype(BF16)
    branches = (
        _dot(ona_ref[...], wna_ref[...]),
        _dot(odf_ref[...], wdf_ref[...]),
        _dot(o_mem, wmem_ref[...]),
    )
    merged = jnp.zeros(x.shape, F32)
    for i, y in enumerate(branches):
        sl = slice(i * D_MODEL, (i + 1) * D_MODEL)
        gate = jax.nn.sigmoid(_dot(h, wgate_ref[:, sl]) + bgate_ref[:, sl])
        merged = merged + gate * y
    o_ref[...] = x + _dot(merged.astype(BF16), wout_ref[...])


def _merge(x1, proj, o_na, o_diff, mk, mv, norm_g, w_gate, b_gate, w_na, w_df, w_mem, w_out,
           seq, m_tokens):
    n = x1.shape[0]
    per_b = seq // MERGE_TM
    row = lambda i: (i, 0)
    return pl.pallas_call(
        _merge_body,
        grid=(n // MERGE_TM,),
        in_specs=[
            pl.BlockSpec((MERGE_TM, D_MODEL), row),
            pl.BlockSpec((MERGE_TM, MEM_WIDTH), lambda i: (i, O_MQ // MEM_WIDTH)),
            pl.BlockSpec((MERGE_TM, NA_WIDTH), row),
            pl.BlockSpec((MERGE_TM, DIFF_V_WIDTH), row),
            pl.BlockSpec((m_tokens, MEM_WIDTH), lambda i: (i // per_b, 0)),
            pl.BlockSpec((m_tokens, MEM_WIDTH), lambda i: (i // per_b, 0)),
            _resident((1, D_MODEL)),
            _resident((D_MODEL, 3 * D_MODEL)),
            _resident((1, 3 * D_MODEL)),
            _resident((NA_WIDTH, D_MODEL)),
            _resident((DIFF_V_WIDTH, D_MODEL)),
            _resident((MEM_WIDTH, D_MODEL)),
            _resident((D_MODEL, D_MODEL)),
        ],
        out_specs=pl.BlockSpec((MERGE_TM, D_MODEL), row),
        out_shape=jax.ShapeDtypeStruct((n, D_MODEL), F32),
        compiler_params=pltpu.CompilerParams(
            dimension_semantics=("arbitrary",), vmem_limit_bytes=V7X_VMEM_LIMIT_BYTES),
        name="merge",
    )(x1, proj, o_na, o_diff, mk, mv, norm_g, w_gate, b_gate, w_na, w_df, w_mem, w_out)


def kernel(x, mem, ffn1_norm, ffn1_w_gate, ffn1_w_up, ffn1_w_down, mix_norm, w_in, na_rpb,
           diff_lambda_q1, diff_lambda_k1, diff_lambda_q2, diff_lambda_k2, diff_subln,
           mem_norm, w_mem_kv, w_gate, b_gate, w_br_na, w_br_diff, w_br_mem, w_out,
           ffn2_norm, ffn2_w_gate, ffn2_w_up, ffn2_w_down, final_norm):
    batch, seq, d_model = x.shape
    m_tokens = mem.shape[1]
    depth = ffn1_norm.shape[0]
    assert d_model == D_MODEL and seq % GRID_W == 0
    assert seq % DIFF_TQ == 0 and seq % MERGE_TM == 0 and (batch * seq) % FFN_TM == 0
    slopes = jnp.asarray([2.0 ** (-8.0 * (i + 1) / DIFF_HEADS) for i in range(DIFF_HEADS)], F32)
    bf = lambda w: w.astype(BF16)
    vec = lambda v: v.reshape(1, -1).astype(F32)

    xt = x.reshape(batch * seq, d_model)
    mem2d = mem.reshape(batch * m_tokens, d_model)
    for l in range(depth):
        lam_init = 0.8 - 0.6 * math.exp(-0.3 * l)
        xt = _ffn(xt, vec(ffn1_norm[l]), bf(ffn1_w_gate[l]), bf(ffn1_w_up[l]), bf(ffn1_w_down[l]),
                  vec(final_norm), final_norm=False)
        proj = _proj(xt, vec(mix_norm[l]), bf(w_in[l]))
        mk, mv = _memkv(mem2d, vec(mem_norm[l]), bf(w_mem_kv[l]), batch, m_tokens)
        o_na = _na(proj, na_rpb[l].reshape(-1).astype(F32), batch, seq)
        o_diff = _diff(proj, slopes, vec(diff_lambda_q1[l]), vec(diff_lambda_k1[l]),
                       vec(diff_lambda_q2[l]), vec(diff_lambda_k2[l]), vec(diff_subln[l]),
                       batch, seq, lam_init)
        xt = _merge(xt, proj, o_na, o_diff, mk, mv, vec(mix_norm[l]), bf(w_gate[l]),
                    vec(b_gate[l]), bf(w_br_na[l]), bf(w_br_diff[l]), bf(w_br_mem[l]), bf(w_out[l]),
                    seq, m_tokens)
        last = l == depth - 1
        xt = _ffn(xt, vec(ffn2_norm[l]), bf(ffn2_w_gate[l]), bf(ffn2_w_up[l]), bf(ffn2_w_down[l]),
                  vec(final_norm), final_norm=last)
    return xt.reshape(batch, seq, d_model)
```

```python
import functools
import math

import jax
import jax.numpy as jnp
from jax import lax
from jax.experimental import pallas as pl
from jax.experimental.pallas import tpu as pltpu

F32 = jnp.float32
BF16 = jnp.bfloat16

D_MODEL = 1024
GRID_W = 64
NA_HEADS = 8
NA_HEAD_DIM = 64
NA_WIN_ROWS = 8
NA_WIN_COLS = 16
DIFF_HEADS = 4
DIFF_QK_DIM = 64
DIFF_V_DIM = 128
MEM_HEADS = 4
MEM_HEAD_DIM = 128
D_FF = 2816
NORM_EPS = 1e-6
NA_WIDTH = NA_HEADS * NA_HEAD_DIM
DIFF_QK_WIDTH = DIFF_HEADS * 2 * DIFF_QK_DIM
DIFF_V_WIDTH = DIFF_HEADS * DIFF_V_DIM
MEM_WIDTH = MEM_HEADS * MEM_HEAD_DIM
IN_WIDTH = 3 * NA_WIDTH + 2 * DIFF_QK_WIDTH + DIFF_V_WIDTH + MEM_WIDTH
O_NQ = 0
O_NK = O_NQ + NA_WIDTH
O_NV = O_NK + NA_WIDTH
O_DQ = O_NV + NA_WIDTH
O_DK = O_DQ + DIFF_QK_WIDTH
O_DV = O_DK + DIFF_QK_WIDTH
O_MQ = O_DV + DIFF_V_WIDTH

LANES = 128
V7X_VMEM_LIMIT_BYTES = 56 * 1024 * 1024

FFN_TM = 512
FFN_TF = 256
PROJ_TN = 512
DIFF_TQ = 256
DIFF_BATCH_CHUNK = 4
MERGE_TM = 512
NEG_BIG = -1e30


def _rmsnorm_f32(x, g):
    ms = jnp.mean(x * x, axis=-1, keepdims=True)
    return (x * lax.rsqrt(ms + NORM_EPS)) * g


def _softmax_rows(logits):
    m = jnp.max(logits, axis=-1, keepdims=True)
    e = jnp.exp(logits - m)
    s = jnp.sum(e, axis=-1, keepdims=True)
    return e * (1.0 / s)


def _dot_nt(a, b):
    return lax.dot_general(a, b, (((1,), (1,)), ((), ())), preferred_element_type=F32)


def _dot(a, b):
    return jnp.dot(a, b, preferred_element_type=F32)


def _resident(shape):
    nd = len(shape)
    return pl.BlockSpec(shape, lambda *_: (0,) * nd, pipeline_mode=pl.Buffered(1))


def _aligned(idx, multiple):
    return idx if isinstance(idx, int) else pl.multiple_of(idx, multiple)


def _software_pipeline(n_items, stages):
    depth = len(stages)
    assert n_items >= depth

    def trip(t, parity, valid):
        for k in reversed(range(depth)):
            if valid(k):
                stages[k](t - k, (parity - k) % 2)

    for t in range(depth - 1):
        trip(t, t % 2, lambda k, t=t: k <= t)
    start = depth - 1
    if (n_items - start) % 2:
        trip(start, start % 2, lambda k: True)
        start += 1

    def body(u, carry):
        t = start + 2 * u
        trip(t, start % 2, lambda k: True)
        trip(t + 1, (start + 1) % 2, lambda k: True)
        return carry

    lax.fori_loop(0, (n_items - start) // 2, body, 0)
    for t in range(n_items, n_items + depth - 1):
        trip(t, t % 2, lambda k, t=t: t - k < n_items)


def _swiglu_half_step(x, g_ref, wg_ref, wu_ref, wd_ref):
    h = _rmsnorm_f32(x, g_ref[...]).astype(BF16)
    acc = jnp.zeros(x.shape, F32)
    for c in range(D_FF // FFN_TF):
        sl = slice(c * FFN_TF, (c + 1) * FFN_TF)
        gate = _dot(h, wg_ref[:, sl])
        up = _dot(h, wu_ref[:, sl])
        act = (gate * jax.nn.sigmoid(gate) * up).astype(BF16)
        acc = acc + _dot(act, wd_ref[sl, :])
    return x + 0.5 * acc


def _ffn_proj_body(x_ref, g_ref, wg_ref, wu_ref, wd_ref, pg_ref, win_ref, x1_ref, proj_ref):
    x1 = _swiglu_half_step(x_ref[...], g_ref, wg_ref, wu_ref, wd_ref)
    x1_ref[...] = x1
    h = _rmsnorm_f32(x1, pg_ref[...]).astype(BF16)
    for c in range(IN_WIDTH // PROJ_TN):
        lo = c * PROJ_TN
        y = _dot(h, win_ref[:, lo:lo + PROJ_TN])
        if lo == O_NQ or lo == O_DQ:
            y = y * (NA_HEAD_DIM ** -0.5)
        proj_ref[:, lo:lo + PROJ_TN] = y.astype(BF16)


def _ffn_proj(x, norm_g, wg, wu, wd, mix_g, w_in):
    n = x.shape[0]
    row = lambda i: (i, 0)
    return pl.pallas_call(
        _ffn_proj_body,
        grid=(n // FFN_TM,),
        in_specs=[
            pl.BlockSpec((FFN_TM, D_MODEL), row),
            _resident((1, D_MODEL)),
            _resident((D_MODEL, D_FF)),
            _resident((D_MODEL, D_FF)),
            _resident((D_FF, D_MODEL)),
            _resident((1, D_MODEL)),
            _resident((D_MODEL, IN_WIDTH)),
        ],
        out_specs=[pl.BlockSpec((FFN_TM, D_MODEL), row), pl.BlockSpec((FFN_TM, IN_WIDTH), row)],
        out_shape=[jax.ShapeDtypeStruct((n, D_MODEL), F32),
                   jax.ShapeDtypeStruct((n, IN_WIDTH), BF16)],
        compiler_params=pltpu.CompilerParams(
            dimension_semantics=("arbitrary",), vmem_limit_bytes=V7X_VMEM_LIMIT_BYTES),
        name="ffn_proj",
    )(x, norm_g, wg, wu, wd, mix_g, w_in)


def _memkv_body(m_ref, g_ref, w_ref, k_ref, v_ref):
    h = _rmsnorm_f32(m_ref[...], g_ref[...]).astype(BF16)
    k_ref[...] = _dot(h, w_ref[:, :MEM_WIDTH]).astype(BF16)
    v_ref[...] = _dot(h, w_ref[:, MEM_WIDTH:]).astype(BF16)


def _memkv(mem2d, norm_g, w_kv, batch, m_tokens):
    row = lambda i: (i, 0)
    shp = jax.ShapeDtypeStruct((batch * m_tokens, MEM_WIDTH), BF16)
    return pl.pallas_call(
        _memkv_body,
        grid=(batch,),
        in_specs=[
            pl.BlockSpec((m_tokens, D_MODEL), row),
            _resident((1, D_MODEL)),
            _resident((D_MODEL, 2 * MEM_WIDTH)),
        ],
        out_specs=[pl.BlockSpec((m_tokens, MEM_WIDTH), row)] * 2,
        out_shape=[shp, shp],
        compiler_params=pltpu.CompilerParams(dimension_semantics=("arbitrary",)),
        name="memkv",
    )(mem2d, norm_g, w_kv)


RPB_ROWS = 2 * NA_WIN_ROWS - 1
RPB_COLS = 2 * NA_WIN_COLS - 1
NA_BAND_KEYS = NA_WIN_ROWS * GRID_W
NA_ITEM_ROWS = 2


def _na_body(rpb_ref, q_ref, k_ref, v_ref, o_ref, tile_ref, band_ref, lg_ref, e_ref, v1_ref,
             *, rows):
    hp = pl.program_id(0)
    b = pl.program_id(1)
    wr = min(NA_WIN_ROWS, rows)

    @pl.when(b == 0)
    def _build_bias():
        c = lax.broadcasted_iota(jnp.int32, (GRID_W, LANES), 0)
        kc = lax.broadcasted_iota(jnp.int32, (GRID_W, LANES), 1) & (GRID_W - 1)
        rel = kc - c
        cs = jnp.clip(c - NA_WIN_COLS // 2, 0, GRID_W - NA_WIN_COLS)
        valid = (kc >= cs) & (kc < cs + NA_WIN_COLS)

        def tile_step(t, carry):
            hl = t // RPB_ROWS
            dr = t - hl * RPB_ROWS
            base = ((hp * 2 + hl) * RPB_ROWS + dr) * RPB_COLS
            acc = jnp.full((GRID_W, LANES), NEG_BIG, F32)
            for dc in range(RPB_COLS):
                acc = jnp.where(rel == dc - (NA_WIN_COLS - 1), rpb_ref[base + dc], acc)
            tile_ref[hl, dr] = jnp.where(valid, acc, NEG_BIG)
            return carry

        lax.fori_loop(0, 2 * RPB_ROWS, tile_step, 0)
        low = lax.broadcasted_iota(jnp.int32, (GRID_W, LANES), 1) < GRID_W
        for hl in range(2):
            for dr0 in range(NA_WIN_ROWS):
                for j in range(wr // 2):
                    band_ref[dr0, hl * GRID_W:(hl + 1) * GRID_W, j * LANES:(j + 1) * LANES] = (
                        jnp.where(low, tile_ref[hl, dr0 + 2 * j], tile_ref[hl, dr0 + 2 * j + 1]))

    lane = lax.broadcasted_iota(jnp.int32, (GRID_W, LANES), 1)
    low = lane < NA_HEAD_DIM

    v = v_ref[...]
    v1_ref[:, :LANES] = v
    v1_ref[:, LANES:] = jnp.where(
        lax.broadcasted_iota(jnp.int32, v.shape, 1) == 0, 1.0, 0.0).astype(BF16)

    def band_rows(r):
        if isinstance(r, int):
            rs = min(max(r - wr // 2, 0), rows - wr)
        else:
            rs = jnp.clip(r - wr // 2, 0, rows - wr)
        return rs, pl.ds(_aligned(rs * GRID_W, GRID_W), wr * GRID_W)

    def query_rows(r):
        return pl.ds(_aligned(r * GRID_W, GRID_W), GRID_W)

    def logits_stage(item, slot):
        for rr in range(NA_ITEM_ROWS):
            r = item * NA_ITEM_ROWS + rr
            rs, keys = band_rows(r)
            q2 = q_ref[query_rows(r), :]
            zero = jnp.zeros_like(q2)
            q_st = jnp.concatenate([jnp.where(low, q2, zero), jnp.where(low, zero, q2)], axis=0)
            lg_ref[slot, rr] = (_dot_nt(q_st, k_ref[keys, :])
                                + band_ref[rs - r + (NA_WIN_ROWS - 1)])

    def exp_stage(item, slot):
        for rr in range(NA_ITEM_ROWS):
            lg = lg_ref[slot, rr]
            e_ref[slot, rr] = jnp.exp(lg - jnp.max(lg, axis=-1, keepdims=True)).astype(BF16)

    def value_stage(item, slot):
        for rr in range(NA_ITEM_ROWS):
            r = item * NA_ITEM_ROWS + rr
            _, keys = band_rows(r)
            acc = _dot(e_ref[slot, rr], v1_ref[keys, :])
            o_st = acc[:, :LANES] * (1.0 / acc[:, LANES:LANES + 1])
            o = jnp.where(low, o_st[:GRID_W], o_st[GRID_W:])
            o_ref[query_rows(r), :] = o.astype(BF16)

    _software_pipeline(rows // NA_ITEM_ROWS, (logits_stage, exp_stage, value_stage))


def _na(proj, rpb_flat, batch, seq):
    rows = seq // GRID_W
    assert rows >= NA_WIN_ROWS and NA_WIN_ROWS % 2 == 0 and rows % NA_ITEM_ROWS == 0
    pairs = NA_HEADS // 2
    blk = lambda off: pl.BlockSpec((seq, LANES), lambda hp, b, off=off: (b, off + hp))
    return pl.pallas_call(
        functools.partial(_na_body, rows=rows),
        grid=(pairs, batch),
        in_specs=[
            pl.BlockSpec(memory_space=pltpu.SMEM),
            blk(O_NQ // LANES), blk(O_NK // LANES), blk(O_NV // LANES),
        ],
        out_specs=pl.BlockSpec((seq, LANES), lambda hp, b: (b, hp)),
        out_shape=jax.ShapeDtypeStruct((batch * seq, NA_WIDTH), BF16),
        scratch_shapes=[
            pltpu.VMEM((2, RPB_ROWS, GRID_W, LANES), F32),
            pltpu.VMEM((NA_WIN_ROWS, 2 * GRID_W, NA_BAND_KEYS), F32),
            pltpu.VMEM((2, NA_ITEM_ROWS, 2 * GRID_W, NA_BAND_KEYS), F32),
            pltpu.VMEM((2, NA_ITEM_ROWS, 2 * GRID_W, NA_BAND_KEYS), BF16),
            pltpu.VMEM((seq, 2 * LANES), BF16),
        ],
        compiler_params=pltpu.CompilerParams(
            dimension_semantics=("arbitrary", "arbitrary"), vmem_limit_bytes=V7X_VMEM_LIMIT_BYTES),
        name="na_attn",
    )(rpb_flat, proj, proj, proj)


def _diff_body(slope_ref, lq1_ref, lk1_ref, lq2_ref, lk2_ref, sub_ref, q_ref, k_ref, v_ref, o_ref,
               dist_ref, v1_ref, s0_ref, s1_ref, m0_ref, m1_ref, e0_ref, e1_ref,
               *, lam_init, seq, nbatch):
    s_ref, m_ref, e_ref = (s0_ref, s1_ref), (m0_ref, m1_ref), (e0_ref, e1_ref)
    h = pl.program_id(0)
    b = pl.program_id(1)
    nq = seq // DIFF_TQ

    @pl.when(b == 0)
    def _build_dist():
        slope = slope_ref[h]
        rel = (lax.broadcasted_iota(jnp.int32, (DIFF_TQ, DIFF_TQ), 0)
               - lax.broadcasted_iota(jnp.int32, (DIFF_TQ, DIFF_TQ), 1))
        for d in range(2 * nq - 1):
            off = (d - (nq - 1)) * DIFF_TQ
            dist_ref[d] = slope * jnp.abs((rel + off).astype(F32))

    lam = (jnp.exp(jnp.sum(lq1_ref[...] * lk1_ref[...], axis=-1, keepdims=True))
           - jnp.exp(jnp.sum(lq2_ref[...] * lk2_ref[...], axis=-1, keepdims=True))
           + lam_init)
    lane = lax.broadcasted_iota(jnp.int32, (DIFF_TQ, LANES), 1)
    first_map = lane < DIFF_QK_DIM
    nq_shift = nq.bit_length() - 1

    def q_rows(j):
        return pl.ds(_aligned(j * DIFF_TQ, DIFF_TQ), DIFF_TQ)

    def batch_rows(j):
        bl = j // nq if isinstance(j, int) else lax.shift_right_logical(j, nq_shift)
        return pl.ds(_aligned(bl * seq, seq), seq)

    def logits_stage(j, slot):
        i = j % nq if isinstance(j, int) else j & (nq - 1)
        q = q_ref[q_rows(j), :]
        zero = jnp.zeros_like(q)
        k = k_ref[batch_rows(j), :]
        dist = jnp.concatenate([dist_ref[i - c + (nq - 1)] for c in range(nq)], axis=1)
        for m, qm in enumerate((jnp.where(first_map, q, zero), jnp.where(first_map, zero, q))):
            s = _dot_nt(qm, k) - dist
            s_ref[slot][m] = s
            m_ref[slot][m] = jnp.max(s, axis=-1, keepdims=True)

    def exp_stage(j, slot):
        for m in range(2):
            e_ref[slot][m * DIFF_TQ:(m + 1) * DIFF_TQ, :] = (
                jnp.exp(s_ref[slot][m] - m_ref[slot][m]).astype(BF16))

    def value_stage(j, slot):
        acc = _dot(e_ref[slot][...], v1_ref[batch_rows(j), :])
        num1, den1 = acc[:DIFF_TQ, :DIFF_V_DIM], acc[:DIFF_TQ, DIFF_V_DIM:DIFF_V_DIM + 1]
        num2, den2 = acc[DIFF_TQ:, :DIFF_V_DIM], acc[DIFF_TQ:, DIFF_V_DIM:DIFF_V_DIM + 1]
        o = num1 * (1.0 / den1) - (lam * (1.0 / den2)) * num2
        o = _rmsnorm_f32(o, sub_ref[...]) * (1.0 - lam_init)
        o_ref[q_rows(j), :] = o.astype(BF16)

    v = v_ref[...]
    v1_ref[:, :DIFF_V_DIM] = v
    v1_ref[:, DIFF_V_DIM:] = jnp.where(
        lax.broadcasted_iota(jnp.int32, v.shape, 1) == 0, 1.0, 0.0).astype(BF16)
    _software_pipeline(nbatch * nq, (logits_stage, exp_stage, value_stage))


def _diff(proj, slopes, lq1, lk1, lq2, lk2, subln, batch, seq, lam_init):
    nq = seq // DIFF_TQ
    nbatch = math.gcd(batch, DIFF_BATCH_CHUNK)
    assert nq & (nq - 1) == 0
    vec = lambda w: pl.BlockSpec((1, w), lambda h, b: (0, 0))
    blk = lambda off: pl.BlockSpec((nbatch * seq, LANES), lambda h, b, off=off: (b, off + h))
    return pl.pallas_call(
        functools.partial(_diff_body, lam_init=lam_init, seq=seq, nbatch=nbatch),
        grid=(DIFF_HEADS, batch // nbatch),
        in_specs=[
            pl.BlockSpec(memory_space=pltpu.SMEM),
            vec(DIFF_QK_DIM), vec(DIFF_QK_DIM), vec(DIFF_QK_DIM), vec(DIFF_QK_DIM), vec(DIFF_V_DIM),
            blk(O_DQ // LANES), blk(O_DK // LANES), blk(O_DV // LANES),
        ],
        out_specs=pl.BlockSpec((nbatch * seq, LANES), lambda h, b: (b, h)),
        out_shape=jax.ShapeDtypeStruct((batch * seq, DIFF_V_WIDTH), BF16),
        scratch_shapes=[
            pltpu.VMEM((2 * nq - 1, DIFF_TQ, DIFF_TQ), F32),
            pltpu.VMEM((nbatch * seq, 2 * LANES), BF16),
            pltpu.VMEM((2, DIFF_TQ, seq), F32), pltpu.VMEM((2, DIFF_TQ, seq), F32),
            pltpu.VMEM((2, DIFF_TQ, 1), F32), pltpu.VMEM((2, DIFF_TQ, 1), F32),
            pltpu.VMEM((2 * DIFF_TQ, seq), BF16), pltpu.VMEM((2 * DIFF_TQ, seq), BF16),
        ],
        compiler_params=pltpu.CompilerParams(
            dimension_semantics=("arbitrary", "arbitrary"),
            vmem_limit_bytes=V7X_VMEM_LIMIT_BYTES),
        name="diff_attn",
    )(slopes, lq1, lk1, lq2, lk2, subln, proj, proj, proj)


def _merge_ffn_body(x_ref, mq_ref, ona_ref, odf_ref, mk_ref, mv_ref, g_ref, wgate_ref, bgate_ref,
                    wna_ref, wdf_ref, wmem_ref, wout_ref, fg_ref, wg_ref, wu_ref, wd_ref, ng_ref,
                    o_ref, *, final_norm):
    x = x_ref[...]
    h = _rmsnorm_f32(x, g_ref[...]).astype(BF16)
    heads = []
    for hh in range(MEM_HEADS):
        sl = slice(hh * MEM_HEAD_DIM, (hh + 1) * MEM_HEAD_DIM)
        logits = _dot_nt(mq_ref[:, sl], mk_ref[:, sl]) * (MEM_HEAD_DIM ** -0.5)
        p = _softmax_rows(logits).astype(BF16)
        heads.append(_dot(p, mv_ref[:, sl]))
    o_mem = jnp.concatenate(heads, axis=-1).astype(BF16)
    branches = (
        _dot(ona_ref[...], wna_ref[...]),
        _dot(odf_ref[...], wdf_ref[...]),
        _dot(o_mem, wmem_ref[...]),
    )
    merged = jnp.zeros(x.shape, F32)
    for i, y in enumerate(branches):
        sl = slice(i * D_MODEL, (i + 1) * D_MODEL)
        gate = jax.nn.sigmoid(_dot(h, wgate_ref[:, sl]) + bgate_ref[:, sl])
        merged = merged + gate * y
    x2 = x + _dot(merged.astype(BF16), wout_ref[...])
    y = _swiglu_half_step(x2, fg_ref, wg_ref, wu_ref, wd_ref)
    if final_norm:
        y = _rmsnorm_f32(y, ng_ref[...])
    o_ref[...] = y


def _merge_ffn(x1, proj, o_na, o_diff, mk, mv, norm_g, w_gate, b_gate, w_na, w_df, w_mem, w_out,
               ffn_g, wg, wu, wd, final_g, seq, m_tokens, *, final_norm):
    n = x1.shape[0]
    per_b = seq // MERGE_TM
    row = lambda i: (i, 0)
    return pl.pallas_call(
        functools.partial(_merge_ffn_body, final_norm=final_norm),
        grid=(n // MERGE_TM,),
        in_specs=[
            pl.BlockSpec((MERGE_TM, D_MODEL), row),
            pl.BlockSpec((MERGE_TM, MEM_WIDTH), lambda i: (i, O_MQ // MEM_WIDTH)),
            pl.BlockSpec((MERGE_TM, NA_WIDTH), row),
            pl.BlockSpec((MERGE_TM, DIFF_V_WIDTH), row),
            pl.BlockSpec((m_tokens, MEM_WIDTH), lambda i: (i // per_b, 0)),
            pl.BlockSpec((m_tokens, MEM_WIDTH), lambda i: (i // per_b, 0)),
            _resident((1, D_MODEL)),
            _resident((D_MODEL, 3 * D_MODEL)),
            _resident((1, 3 * D_MODEL)),
            _resident((NA_WIDTH, D_MODEL)),
            _resident((DIFF_V_WIDTH, D_MODEL)),
            _resident((MEM_WIDTH, D_MODEL)),
            _resident((D_MODEL, D_MODEL)),
            _resident((1, D_MODEL)),
            _resident((D_MODEL, D_FF)),
            _resident((D_MODEL, D_FF)),
            _resident((D_FF, D_MODEL)),
            _resident((1, D_MODEL)),
        ],
        out_specs=pl.BlockSpec((MERGE_TM, D_MODEL), row),
        out_shape=jax.ShapeDtypeStruct((n, D_MODEL), F32),
        compiler_params=pltpu.CompilerParams(
            dimension_semantics=("arbitrary",), vmem_limit_bytes=V7X_VMEM_LIMIT_BYTES),
        name="merge_ffn",
    )(x1, proj, o_na, o_diff, mk, mv, norm_g, w_gate, b_gate, w_na, w_df, w_mem, w_out,
      ffn_g, wg, wu, wd, final_g)


def kernel(x, mem, ffn1_norm, ffn1_w_gate, ffn1_w_up, ffn1_w_down, mix_norm, w_in, na_rpb,
           diff_lambda_q1, diff_lambda_k1, diff_lambda_q2, diff_lambda_k2, diff_subln,
           mem_norm, w_mem_kv, w_gate, b_gate, w_br_na, w_br_diff, w_br_mem, w_out,
           ffn2_norm, ffn2_w_gate, ffn2_w_up, ffn2_w_down, final_norm):
    batch, seq, d_model = x.shape
    m_tokens = mem.shape[1]
    depth = ffn1_norm.shape[0]
    assert d_model == D_MODEL and seq % GRID_W == 0
    assert seq % DIFF_TQ == 0 and seq % MERGE_TM == 0 and (batch * seq) % FFN_TM == 0
    slopes = jnp.asarray([2.0 ** (-8.0 * (i + 1) / DIFF_HEADS) for i in range(DIFF_HEADS)], F32)
    bf = lambda w: w.astype(BF16)
    vec = lambda v: v.reshape(1, -1).astype(F32)

    xt = x.reshape(batch * seq, d_model)
    mem2d = mem.reshape(batch * m_tokens, d_model)
    for l in range(depth):
        lam_init = 0.8 - 0.6 * math.exp(-0.3 * l)
        xt, proj = _ffn_proj(xt, vec(ffn1_norm[l]), bf(ffn1_w_gate[l]), bf(ffn1_w_up[l]),
                             bf(ffn1_w_down[l]), vec(mix_norm[l]), bf(w_in[l]))
        mk, mv = _memkv(mem2d, vec(mem_norm[l]), bf(w_mem_kv[l]), batch, m_tokens)
        o_na = _na(proj, na_rpb[l].reshape(-1).astype(F32), batch, seq)
        o_diff = _diff(proj, slopes, vec(diff_lambda_q1[l]), vec(diff_lambda_k1[l]),
                       vec(diff_lambda_q2[l]), vec(diff_lambda_k2[l]), vec(diff_subln[l]),
                       batch, seq, lam_init)
        xt = _merge_ffn(xt, proj, o_na, o_diff, mk, mv, vec(mix_norm[l]), bf(w_gate[l]),
                        vec(b_gate[l]), bf(w_br_na[l]), bf(w_br_diff[l]), bf(w_br_mem[l]),
                        bf(w_out[l]), vec(ffn2_norm[l]), bf(ffn2_w_gate[l]), bf(ffn2_w_up[l]),
                        bf(ffn2_w_down[l]), vec(final_norm), seq, m_tokens,
                        final_norm=(l == depth - 1))
    return xt.reshape(batch, seq, d_model)
```

```python
import functools
import math

import jax
import jax.numpy as jnp
from jax import lax
from jax.experimental import pallas as pl
from jax.experimental.pallas import tpu as pltpu

F32 = jnp.float32
BF16 = jnp.bfloat16

D_MODEL = 1024
GRID_W = 64
NA_HEADS = 8
NA_HEAD_DIM = 64
NA_WIN_ROWS = 8
NA_WIN_COLS = 16
DIFF_HEADS = 4
DIFF_QK_DIM = 64
DIFF_V_DIM = 128
MEM_HEADS = 4
MEM_HEAD_DIM = 128
D_FF = 2816
NORM_EPS = 1e-6
NA_WIDTH = NA_HEADS * NA_HEAD_DIM
DIFF_QK_WIDTH = DIFF_HEADS * 2 * DIFF_QK_DIM
DIFF_V_WIDTH = DIFF_HEADS * DIFF_V_DIM
MEM_WIDTH = MEM_HEADS * MEM_HEAD_DIM
IN_WIDTH = 3 * NA_WIDTH + 2 * DIFF_QK_WIDTH + DIFF_V_WIDTH + MEM_WIDTH
O_NQ = 0
O_NK = O_NQ + NA_WIDTH
O_NV = O_NK + NA_WIDTH
O_DQ = O_NV + NA_WIDTH
O_DK = O_DQ + DIFF_QK_WIDTH
O_DV = O_DK + DIFF_QK_WIDTH
O_MQ = O_DV + DIFF_V_WIDTH

LANES = 128
V7X_VMEM_LIMIT_BYTES = 56 * 1024 * 1024

FFN_TM = 512
FFN_TF = 256
PROJ_TN = 512
DIFF_TQ = 256
DIFF_BATCH_CHUNK = 4
MERGE_TM = 512
NEG_BIG = -1e30


def _rmsnorm_f32(x, g):
    ms = jnp.mean(x * x, axis=-1, keepdims=True)
    return (x * lax.rsqrt(ms + NORM_EPS)) * g


def _softmax_rows(logits):
    m = jnp.max(logits, axis=-1, keepdims=True)
    e = jnp.exp(logits - m)
    s = jnp.sum(e, axis=-1, keepdims=True)
    return e * (1.0 / s)


def _dot_nt(a, b):
    return lax.dot_general(a, b, (((1,), (1,)), ((), ())), preferred_element_type=F32)


def _dot(a, b):
    return jnp.dot(a, b, preferred_element_type=F32)


def _resident(shape):
    nd = len(shape)
    return pl.BlockSpec(shape, lambda *_: (0,) * nd, pipeline_mode=pl.Buffered(1))


def _aligned(idx, multiple):
    return idx if isinstance(idx, int) else pl.multiple_of(idx, multiple)


def _software_pipeline(n_items, stages, order):
    depth = len(stages)
    assert n_items >= depth and sorted(order) == list(range(depth))

    def trip(t, parity, valid):
        for k in order:
            if valid(k):
                stages[k](t - k, (parity - k) % 2)

    for t in range(depth - 1):
        trip(t, t % 2, lambda k, t=t: k <= t)
    start = depth - 1
    if (n_items - start) % 2:
        trip(start, start % 2, lambda k: True)
        start += 1

    def body(u, carry):
        t = start + 2 * u
        trip(t, start % 2, lambda k: True)
        trip(t + 1, (start + 1) % 2, lambda k: True)
        return carry

    lax.fori_loop(0, (n_items - start) // 2, body, 0)
    for t in range(n_items, n_items + depth - 1):
        trip(t, t % 2, lambda k, t=t: t - k < n_items)


def _swiglu_half_step(x, g_ref, wg_ref, wu_ref, wd_ref):
    h = _rmsnorm_f32(x, g_ref[...]).astype(BF16)
    acc = jnp.zeros(x.shape, F32)
    for c in range(D_FF // FFN_TF):
        sl = slice(c * FFN_TF, (c + 1) * FFN_TF)
        gate = _dot(h, wg_ref[:, sl])
        up = _dot(h, wu_ref[:, sl])
        act = (gate * jax.nn.sigmoid(gate) * up).astype(BF16)
        acc = acc + _dot(act, wd_ref[sl, :])
    return x + 0.5 * acc


def _ffn_proj_body(x_ref, g_ref, wg_ref, wu_ref, wd_ref, pg_ref, win_ref, x1_ref, proj_ref):
    x1 = _swiglu_half_step(x_ref[...], g_ref, wg_ref, wu_ref, wd_ref)
    x1_ref[...] = x1
    h = _rmsnorm_f32(x1, pg_ref[...]).astype(BF16)
    for c in range(IN_WIDTH // PROJ_TN):
        lo = c * PROJ_TN
        y = _dot(h, win_ref[:, lo:lo + PROJ_TN])
        if lo == O_NQ or lo == O_DQ:
            y = y * (NA_HEAD_DIM ** -0.5)
        proj_ref[:, lo:lo + PROJ_TN] = y.astype(BF16)


def _ffn_proj(x, norm_g, wg, wu, wd, mix_g, w_in):
    n = x.shape[0]
    row = lambda i: (i, 0)
    return pl.pallas_call(
        _ffn_proj_body,
        grid=(n // FFN_TM,),
        in_specs=[
            pl.BlockSpec((FFN_TM, D_MODEL), row),
            _resident((1, D_MODEL)),
            _resident((D_MODEL, D_FF)),
            _resident((D_MODEL, D_FF)),
            _resident((D_FF, D_MODEL)),
            _resident((1, D_MODEL)),
            _resident((D_MODEL, IN_WIDTH)),
        ],
        out_specs=[pl.BlockSpec((FFN_TM, D_MODEL), row), pl.BlockSpec((FFN_TM, IN_WIDTH), row)],
        out_shape=[jax.ShapeDtypeStruct((n, D_MODEL), F32),
                   jax.ShapeDtypeStruct((n, IN_WIDTH), BF16)],
        compiler_params=pltpu.CompilerParams(
            dimension_semantics=("arbitrary",), vmem_limit_bytes=V7X_VMEM_LIMIT_BYTES),
        name="ffn_proj",
    )(x, norm_g, wg, wu, wd, mix_g, w_in)


def _memkv_body(m_ref, g_ref, w_ref, k_ref, v_ref):
    h = _rmsnorm_f32(m_ref[...], g_ref[...]).astype(BF16)
    k_ref[...] = _dot(h, w_ref[:, :MEM_WIDTH]).astype(BF16)
    v_ref[...] = _dot(h, w_ref[:, MEM_WIDTH:]).astype(BF16)


def _memkv(mem2d, norm_g, w_kv, batch, m_tokens):
    row = lambda i: (i, 0)
    shp = jax.ShapeDtypeStruct((batch * m_tokens, MEM_WIDTH), BF16)
    return pl.pallas_call(
        _memkv_body,
        grid=(batch,),
        in_specs=[
            pl.BlockSpec((m_tokens, D_MODEL), row),
            _resident((1, D_MODEL)),
            _resident((D_MODEL, 2 * MEM_WIDTH)),
        ],
        out_specs=[pl.BlockSpec((m_tokens, MEM_WIDTH), row)] * 2,
        out_shape=[shp, shp],
        compiler_params=pltpu.CompilerParams(dimension_semantics=("arbitrary",)),
        name="memkv",
    )(mem2d, norm_g, w_kv)


RPB_ROWS = 2 * NA_WIN_ROWS - 1
RPB_COLS = 2 * NA_WIN_COLS - 1
NA_BAND_KEYS = NA_WIN_ROWS * GRID_W
NA_BATCH_CHUNK = 4
NA_ITEM_ROWS = 4


def _na_body(rpb_ref, q_ref, k_ref, v_ref, o_ref, tile_ref, band_ref, v1_ref,
             lg0_ref, lg1_ref, m0_ref, m1_ref, a0_ref, a1_ref, *, rows, nbatch):
    lg_ref, m_ref, a_ref = (lg0_ref, lg1_ref), (m0_ref, m1_ref), (a0_ref, a1_ref)
    hp = pl.program_id(0)
    b = pl.program_id(1)
    wr = min(NA_WIN_ROWS, rows)

    @pl.when(b == 0)
    def _build_bias():
        c = lax.broadcasted_iota(jnp.int32, (GRID_W, LANES), 0)
        kc = lax.broadcasted_iota(jnp.int32, (GRID_W, LANES), 1) & (GRID_W - 1)
        rel = kc - c
        cs = jnp.clip(c - NA_WIN_COLS // 2, 0, GRID_W - NA_WIN_COLS)
        valid = (kc >= cs) & (kc < cs + NA_WIN_COLS)

        def tile_step(t, carry):
            hl = t // RPB_ROWS
            dr = t - hl * RPB_ROWS
            base = ((hp * 2 + hl) * RPB_ROWS + dr) * RPB_COLS
            acc = jnp.full((GRID_W, LANES), NEG_BIG, F32)
            for dc in range(RPB_COLS):
                acc = jnp.where(rel == dc - (NA_WIN_COLS - 1), rpb_ref[base + dc], acc)
            tile_ref[hl, dr] = jnp.where(valid, acc, NEG_BIG)
            return carry

        lax.fori_loop(0, 2 * RPB_ROWS, tile_step, 0)
        low = lax.broadcasted_iota(jnp.int32, (GRID_W, LANES), 1) < GRID_W
        for hl in range(2):
            for dr0 in range(NA_WIN_ROWS):
                for j in range(wr // 2):
                    band_ref[dr0, hl * GRID_W:(hl + 1) * GRID_W, j * LANES:(j + 1) * LANES] = (
                        jnp.where(low, tile_ref[hl, dr0 + 2 * j], tile_ref[hl, dr0 + 2 * j + 1]))

    lane = lax.broadcasted_iota(jnp.int32, (GRID_W, LANES), 1)
    low = lane < NA_HEAD_DIM

    v = v_ref[...]
    v1_ref[:, :LANES] = v
    v1_ref[:, LANES:] = jnp.where(
        lax.broadcasted_iota(jnp.int32, v.shape, 1) == 0, 1.0, 0.0).astype(BF16)

    items_per_batch = rows // NA_ITEM_ROWS
    ipb_shift = items_per_batch.bit_length() - 1

    def locate(item, rr):
        if isinstance(item, int):
            bl, it = divmod(item, items_per_batch)
        else:
            bl, it = lax.shift_right_logical(item, ipb_shift), item & (items_per_batch - 1)
        return it * NA_ITEM_ROWS + rr, bl * (rows * GRID_W)

    def band_rows(r, base):
        if isinstance(r, int):
            rs = min(max(r - wr // 2, 0), rows - wr)
        else:
            rs = jnp.clip(r - wr // 2, 0, rows - wr)
        return rs, pl.ds(_aligned(base + rs * GRID_W, GRID_W), wr * GRID_W)

    def query_rows(r, base):
        return pl.ds(_aligned(base + r * GRID_W, GRID_W), GRID_W)

    def logits_stage(item, slot):
        for rr in range(NA_ITEM_ROWS):
            r, base = locate(item, rr)
            rs, keys = band_rows(r, base)
            q2 = q_ref[query_rows(r, base), :]
            zero = jnp.zeros_like(q2)
            q_st = jnp.concatenate([jnp.where(low, q2, zero), jnp.where(low, zero, q2)], axis=0)
            lg = _dot_nt(q_st, k_ref[keys, :]) + band_ref[rs - r + (NA_WIN_ROWS - 1)]
            lg_ref[slot][rr] = lg
            lane_max = lg[:, :LANES]
            for c in range(1, NA_BAND_KEYS // LANES):
                lane_max = jnp.maximum(lane_max, lg[:, c * LANES:(c + 1) * LANES])
            m_ref[slot][rr] = lane_max

    def value_stage(item, slot):
        for rr in range(NA_ITEM_ROWS):
            r, base = locate(item, rr)
            _, keys = band_rows(r, base)
            row_max = jnp.max(m_ref[slot][rr], axis=-1, keepdims=True)
            e = jnp.exp(lg_ref[slot][rr] - row_max).astype(BF16)
            a_ref[slot][rr] = _dot(e, v1_ref[keys, :])

    def finish_stage(item, slot):
        for rr in range(NA_ITEM_ROWS):
            r, base = locate(item, rr)
            acc = a_ref[slot][rr]
            o_st = acc[:, :LANES] * (1.0 / acc[:, LANES:LANES + 1])
            o = jnp.where(low, o_st[:GRID_W], o_st[GRID_W:])
            o_ref[query_rows(r, base), :] = o.astype(BF16)

    _software_pipeline(nbatch * items_per_batch, (logits_stage, value_stage, finish_stage),
                       order=(2, 0, 1))


def _na(proj, rpb_flat, batch, seq):
    rows = seq // GRID_W
    items_per_batch = rows // NA_ITEM_ROWS
    nbatch = math.gcd(batch, NA_BATCH_CHUNK)
    assert rows >= NA_WIN_ROWS and NA_WIN_ROWS % 2 == 0 and rows % NA_ITEM_ROWS == 0
    assert items_per_batch & (items_per_batch - 1) == 0
    pairs = NA_HEADS // 2
    blk = lambda off: pl.BlockSpec((nbatch * seq, LANES), lambda hp, b, off=off: (b, off + hp))
    return pl.pallas_call(
        functools.partial(_na_body, rows=rows, nbatch=nbatch),
        grid=(pairs, batch // nbatch),
        in_specs=[
            pl.BlockSpec(memory_space=pltpu.SMEM),
            blk(O_NQ // LANES), blk(O_NK // LANES), blk(O_NV // LANES),
        ],
        out_specs=pl.BlockSpec((nbatch * seq, LANES), lambda hp, b: (b, hp)),
        out_shape=jax.ShapeDtypeStruct((batch * seq, NA_WIDTH), BF16),
        scratch_shapes=[
            pltpu.VMEM((2, RPB_ROWS, GRID_W, LANES), F32),
            pltpu.VMEM((NA_WIN_ROWS, 2 * GRID_W, NA_BAND_KEYS), F32),
            pltpu.VMEM((nbatch * seq, 2 * LANES), BF16),
            pltpu.VMEM((NA_ITEM_ROWS, 2 * GRID_W, NA_BAND_KEYS), F32),
            pltpu.VMEM((NA_ITEM_ROWS, 2 * GRID_W, NA_BAND_KEYS), F32),
            pltpu.VMEM((NA_ITEM_ROWS, 2 * GRID_W, LANES), F32),
            pltpu.VMEM((NA_ITEM_ROWS, 2 * GRID_W, LANES), F32),
            pltpu.VMEM((NA_ITEM_ROWS, 2 * GRID_W, 2 * LANES), F32),
            pltpu.VMEM((NA_ITEM_ROWS, 2 * GRID_W, 2 * LANES), F32),
        ],
        compiler_params=pltpu.CompilerParams(
            dimension_semantics=("arbitrary", "arbitrary"), vmem_limit_bytes=V7X_VMEM_LIMIT_BYTES),
        name="na_attn",
    )(rpb_flat, proj, proj, proj)


def _diff_body(slope_ref, lq1_ref, lk1_ref, lq2_ref, lk2_ref, sub_ref, q_ref, k_ref, v_ref, o_ref,
               dist_ref, v1_ref, s0_ref, s1_ref, m0_ref, m1_ref, a0_ref, a1_ref,
               *, lam_init, seq, nbatch):
    s_ref, m_ref, a_ref = (s0_ref, s1_ref), (m0_ref, m1_ref), (a0_ref, a1_ref)
    h = pl.program_id(0)
    b = pl.program_id(1)
    nq = seq // DIFF_TQ

    @pl.when(b == 0)
    def _build_dist():
        slope = slope_ref[h]
        rel = (lax.broadcasted_iota(jnp.int32, (DIFF_TQ, DIFF_TQ), 0)
               - lax.broadcasted_iota(jnp.int32, (DIFF_TQ, DIFF_TQ), 1))
        for d in range(2 * nq - 1):
            off = (d - (nq - 1)) * DIFF_TQ
            dist_ref[d] = slope * jnp.abs((rel + off).astype(F32))

    lam = (jnp.exp(jnp.sum(lq1_ref[...] * lk1_ref[...], axis=-1, keepdims=True))
           - jnp.exp(jnp.sum(lq2_ref[...] * lk2_ref[...], axis=-1, keepdims=True))
           + lam_init)
    lane = lax.broadcasted_iota(jnp.int32, (DIFF_TQ, LANES), 1)
    first_map = lane < DIFF_QK_DIM
    nq_shift = nq.bit_length() - 1

    def q_rows(j):
        return pl.ds(_aligned(j * DIFF_TQ, DIFF_TQ), DIFF_TQ)

    def batch_rows(j):
        bl = j // nq if isinstance(j, int) else lax.shift_right_logical(j, nq_shift)
        return pl.ds(_aligned(bl * seq, seq), seq)

    def logits_stage(j, slot):
        i = j % nq if isinstance(j, int) else j & (nq - 1)
        q = q_ref[q_rows(j), :]
        zero = jnp.zeros_like(q)
        k = k_ref[batch_rows(j), :]
        dist = jnp.concatenate([dist_ref[i - c + (nq - 1)] for c in range(nq)], axis=1)
        for m, qm in enumerate((jnp.where(first_map, q, zero), jnp.where(first_map, zero, q))):
            s = _dot_nt(qm, k) - dist
            s_ref[slot][m] = s
            lane_max = s[:, :LANES]
            for c in range(1, seq // LANES):
                lane_max = jnp.maximum(lane_max, s[:, c * LANES:(c + 1) * LANES])
            m_ref[slot][m] = lane_max

    def value_stage(j, slot):
        e = jnp.concatenate(
            [jnp.exp(s_ref[slot][m] - jnp.max(m_ref[slot][m], axis=-1, keepdims=True)).astype(BF16)
             for m in range(2)], axis=0)
        a_ref[slot][...] = _dot(e, v1_ref[batch_rows(j), :])

    def finish_stage(j, slot):
        acc = a_ref[slot][...]
        num1, den1 = acc[:DIFF_TQ, :DIFF_V_DIM], acc[:DIFF_TQ, DIFF_V_DIM:DIFF_V_DIM + 1]
        num2, den2 = acc[DIFF_TQ:, :DIFF_V_DIM], acc[DIFF_TQ:, DIFF_V_DIM:DIFF_V_DIM + 1]
        o = num1 * (1.0 / den1) - (lam * (1.0 / den2)) * num2
        o = _rmsnorm_f32(o, sub_ref[...]) * (1.0 - lam_init)
        o_ref[q_rows(j), :] = o.astype(BF16)

    v = v_ref[...]
    v1_ref[:, :DIFF_V_DIM] = v
    v1_ref[:, DIFF_V_DIM:] = jnp.where(
        lax.broadcasted_iota(jnp.int32, v.shape, 1) == 0, 1.0, 0.0).astype(BF16)
    _software_pipeline(nbatch * nq, (logits_stage, value_stage, finish_stage), order=(2, 0, 1))


def _diff(proj, slopes, lq1, lk1, lq2, lk2, subln, batch, seq, lam_init):
    nq = seq // DIFF_TQ
    nbatch = math.gcd(batch, DIFF_BATCH_CHUNK)
    assert nq & (nq - 1) == 0
    vec = lambda w: pl.BlockSpec((1, w), lambda h, b: (0, 0))
    blk = lambda off: pl.BlockSpec((nbatch * seq, LANES), lambda h, b, off=off: (b, off + h))
    return pl.pallas_call(
        functools.partial(_diff_body, lam_init=lam_init, seq=seq, nbatch=nbatch),
        grid=(DIFF_HEADS, batch // nbatch),
        in_specs=[
            pl.BlockSpec(memory_space=pltpu.SMEM),
            vec(DIFF_QK_DIM), vec(DIFF_QK_DIM), vec(DIFF_QK_DIM), vec(DIFF_QK_DIM), vec(DIFF_V_DIM),
            blk(O_DQ // LANES), blk(O_DK // LANES), blk(O_DV // LANES),
        ],
        out_specs=pl.BlockSpec((nbatch * seq, LANES), lambda h, b: (b, h)),
        out_shape=jax.ShapeDtypeStruct((batch * seq, DIFF_V_WIDTH), BF16),
        scratch_shapes=[
            pltpu.VMEM((2 * nq - 1, DIFF_TQ, DIFF_TQ), F32),
            pltpu.VMEM((nbatch * seq, 2 * LANES), BF16),
            pltpu.VMEM((2, DIFF_TQ, seq), F32), pltpu.VMEM((2, DIFF_TQ, seq), F32),
            pltpu.VMEM((2, DIFF_TQ, LANES), F32), pltpu.VMEM((2, DIFF_TQ, LANES), F32),
            pltpu.VMEM((2 * DIFF_TQ, 2 * LANES), F32), pltpu.VMEM((2 * DIFF_TQ, 2 * LANES), F32),
        ],
        compiler_params=pltpu.CompilerParams(
            dimension_semantics=("arbitrary", "arbitrary"),
            vmem_limit_bytes=V7X_VMEM_LIMIT_BYTES),
        name="diff_attn",
    )(slopes, lq1, lk1, lq2, lk2, subln, proj, proj, proj)


def _merge_ffn_body(x_ref, mq_ref, ona_ref, odf_ref, mk_ref, mv_ref, g_ref, wgate_ref, bgate_ref,
                    wna_ref, wdf_ref, wmem_ref, wout_ref, fg_ref, wg_ref, wu_ref, wd_ref, ng_ref,
                    o_ref, *, final_norm):
    x = x_ref[...]
    h = _rmsnorm_f32(x, g_ref[...]).astype(BF16)
    heads = []
    for hh in range(MEM_HEADS):
        sl = slice(hh * MEM_HEAD_DIM, (hh + 1) * MEM_HEAD_DIM)
        logits = _dot_nt(mq_ref[:, sl], mk_ref[:, sl]) * (MEM_HEAD_DIM ** -0.5)
        p = _softmax_rows(logits).astype(BF16)
        heads.append(_dot(p, mv_ref[:, sl]))
    o_mem = jnp.concatenate(heads, axis=-1).astype(BF16)
    branches = (
        _dot(ona_ref[...], wna_ref[...]),
        _dot(odf_ref[...], wdf_ref[...]),
        _dot(o_mem, wmem_ref[...]),
    )
    merged = jnp.zeros(x.shape, F32)
    for i, y in enumerate(branches):
        sl = slice(i * D_MODEL, (i + 1) * D_MODEL)
        gate = jax.nn.sigmoid(_dot(h, wgate_ref[:, sl]) + bgate_ref[:, sl])
        merged = merged + gate * y
    x2 = x + _dot(merged.astype(BF16), wout_ref[...])
    y = _swiglu_half_step(x2, fg_ref, wg_ref, wu_ref, wd_ref)
    if final_norm:
        y = _rmsnorm_f32(y, ng_ref[...])
    o_ref[...] = y


def _merge_ffn(x1, proj, o_na, o_diff, mk, mv, norm_g, w_gate, b_gate, w_na, w_df, w_mem, w_out,
               ffn_g, wg, wu, wd, final_g, seq, m_tokens, *, final_norm):
    n = x1.shape[0]
    per_b = seq // MERGE_TM
    row = lambda i: (i, 0)
    return pl.pallas_call(
        functools.partial(_merge_ffn_body, final_norm=final_norm),
        grid=(n // MERGE_TM,),
        in_specs=[
            pl.BlockSpec((MERGE_TM, D_MODEL), row),
            pl.BlockSpec((MERGE_TM, MEM_WIDTH), lambda i: (i, O_MQ // MEM_WIDTH)),
            pl.BlockSpec((MERGE_TM, NA_WIDTH), row),
            pl.BlockSpec((MERGE_TM, DIFF_V_WIDTH), row),
            pl.BlockSpec((m_tokens, MEM_WIDTH), lambda i: (i // per_b, 0)),
            pl.BlockSpec((m_tokens, MEM_WIDTH), lambda i: (i // per_b, 0)),
            _resident((1, D_MODEL)),
            _resident((D_MODEL, 3 * D_MODEL)),
            _resident((1, 3 * D_MODEL)),
            _resident((NA_WIDTH, D_MODEL)),
            _resident((DIFF_V_WIDTH, D_MODEL)),
            _resident((MEM_WIDTH, D_MODEL)),
            _resident((D_MODEL, D_MODEL)),
            _resident((1, D_MODEL)),
            _resident((D_MODEL, D_FF)),
            _resident((D_MODEL, D_FF)),
            _resident((D_FF, D_MODEL)),
            _resident((1, D_MODEL)),
        ],
        out_specs=pl.BlockSpec((MERGE_TM, D_MODEL), row),
        out_shape=jax.ShapeDtypeStruct((n, D_MODEL), F32),
        compiler_params=pltpu.CompilerParams(
            dimension_semantics=("arbitrary",), vmem_limit_bytes=V7X_VMEM_LIMIT_BYTES),
        name="merge_ffn",
    )(x1, proj, o_na, o_diff, mk, mv, norm_g, w_gate, b_gate, w_na, w_df, w_mem, w_out,
      ffn_g, wg, wu, wd, final_g)


def kernel(x, mem, ffn1_norm, ffn1_w_gate, ffn1_w_up, ffn1_w_down, mix_norm, w_in, na_rpb,
           diff_lambda_q1, diff_lambda_k1, diff_lambda_q2, diff_lambda_k2, diff_subln,
           mem_norm, w_mem_kv, w_gate, b_gate, w_br_na, w_br_diff, w_br_mem, w_out,
           ffn2_norm, ffn2_w_gate, ffn2_w_up, ffn2_w_down, final_norm):
    batch, seq, d_model = x.shape
    m_tokens = mem.shape[1]
    depth = ffn1_norm.shape[0]
    assert d_model == D_MODEL and seq % GRID_W == 0
    assert seq % DIFF_TQ == 0 and seq % MERGE_TM == 0 and (batch * seq) % FFN_TM == 0
    slopes = jnp.asarray([2.0 ** (-8.0 * (i + 1) / DIFF_HEADS) for i in range(DIFF_HEADS)], F32)
    bf = lambda w: w.astype(BF16)
    vec = lambda v: v.reshape(1, -1).astype(F32)

    xt = x.reshape(batch * seq, d_model)
    mem2d = mem.reshape(batch * m_tokens, d_model)
    for l in range(depth):
        lam_init = 0.8 - 0.6 * math.exp(-0.3 * l)
        xt, proj = _ffn_proj(xt, vec(ffn1_norm[l]), bf(ffn1_w_gate[l]), bf(ffn1_w_up[l]),
                             bf(ffn1_w_down[l]), vec(mix_norm[l]), bf(w_in[l]))
        mk, mv = _memkv(mem2d, vec(mem_norm[l]), bf(w_mem_kv[l]), batch, m_tokens)
        o_na = _na(proj, na_rpb[l].reshape(-1).astype(F32), batch, seq)
        o_diff = _diff(proj, slopes, vec(diff_lambda_q1[l]), vec(diff_lambda_k1[l]),
                       vec(diff_lambda_q2[l]), vec(diff_lambda_k2[l]), vec(diff_subln[l]),
                       batch, seq, lam_init)
        xt = _merge_ffn(xt, proj, o_na, o_diff, mk, mv, vec(mix_norm[l]), bf(w_gate[l]),
                        vec(b_gate[l]), bf(w_br_na[l]), bf(w_br_diff[l]), bf(w_br_mem[l]),
                        bf(w_out[l]), vec(ffn2_norm[l]), bf(ffn2_w_gate[l]), bf(ffn2_w_up[l]),
                        bf(ffn2_w_down[l]), vec(final_norm), seq, m_tokens,
                        final_norm=(l == depth - 1))
    return xt.reshape(batch, seq, d_model)
```

```python
import functools
import math

import jax
import jax.numpy as jnp
from jax import lax
from jax.experimental import pallas as pl
from jax.experimental.pallas import tpu as pltpu

F32 = jnp.float32
BF16 = jnp.bfloat16

D_MODEL = 1024
GRID_W = 64
NA_HEADS = 8
NA_HEAD_DIM = 64
NA_WIN_ROWS = 8
NA_WIN_COLS = 16
DIFF_HEADS = 4
DIFF_QK_DIM = 64
DIFF_V_DIM = 128
MEM_HEADS = 4
MEM_HEAD_DIM = 128
D_FF = 2816
NORM_EPS = 1e-6
NA_WIDTH = NA_HEADS * NA_HEAD_DIM
DIFF_QK_WIDTH = DIFF_HEADS * 2 * DIFF_QK_DIM
DIFF_V_WIDTH = DIFF_HEADS * DIFF_V_DIM
MEM_WIDTH = MEM_HEADS * MEM_HEAD_DIM
IN_WIDTH = 3 * NA_WIDTH + 2 * DIFF_QK_WIDTH + DIFF_V_WIDTH + MEM_WIDTH
O_NQ = 0
O_NK = O_NQ + NA_WIDTH
O_NV = O_NK + NA_WIDTH
O_DQ = O_NV + NA_WIDTH
O_DK = O_DQ + DIFF_QK_WIDTH
O_DV = O_DK + DIFF_QK_WIDTH
O_MQ = O_DV + DIFF_V_WIDTH

LANES = 128
V7X_VMEM_LIMIT_BYTES = 56 * 1024 * 1024

FFN_TM = 512
FFN_TF = 256
PROJ_TN = 512
DIFF_TQ = 512
DIFF_BATCH_CHUNK = 4
MERGE_TM = 512
NEG_BIG = -1e30


def _rmsnorm_f32(x, g):
    ms = jnp.mean(x * x, axis=-1, keepdims=True)
    return (x * lax.rsqrt(ms + NORM_EPS)) * g


def _softmax_rows(logits):
    m = jnp.max(logits, axis=-1, keepdims=True)
    e = jnp.exp(logits - m)
    s = jnp.sum(e, axis=-1, keepdims=True)
    return e * (1.0 / s)


def _dot_nt(a, b):
    return lax.dot_general(a, b, (((1,), (1,)), ((), ())), preferred_element_type=F32)


def _dot(a, b):
    return jnp.dot(a, b, preferred_element_type=F32)


def _resident(shape):
    nd = len(shape)
    return pl.BlockSpec(shape, lambda *_: (0,) * nd, pipeline_mode=pl.Buffered(1))


def _aligned(idx, multiple):
    return idx if isinstance(idx, int) else pl.multiple_of(idx, multiple)


def _software_pipeline(n_items, stages, order):
    depth = len(stages)
    assert n_items >= depth and sorted(order) == list(range(depth))

    def trip(t, parity, valid):
        for k in order:
            if valid(k):
                stages[k](t - k, (parity - k) % 2)

    for t in range(depth - 1):
        trip(t, t % 2, lambda k, t=t: k <= t)
    start = depth - 1
    if (n_items - start) % 2:
        trip(start, start % 2, lambda k: True)
        start += 1

    def body(u, carry):
        t = start + 2 * u
        trip(t, start % 2, lambda k: True)
        trip(t + 1, (start + 1) % 2, lambda k: True)
        return carry

    lax.fori_loop(0, (n_items - start) // 2, body, 0)
    for t in range(n_items, n_items + depth - 1):
        trip(t, t % 2, lambda k, t=t: t - k < n_items)


def _swiglu_half_step(x, g_ref, wg_ref, wu_ref, wd_ref):
    h = _rmsnorm_f32(x, g_ref[...]).astype(BF16)
    acc = jnp.zeros(x.shape, F32)
    for c in range(D_FF // FFN_TF):
        sl = slice(c * FFN_TF, (c + 1) * FFN_TF)
        gate = _dot(h, wg_ref[:, sl])
        up = _dot(h, wu_ref[:, sl])
        act = (gate * jax.nn.sigmoid(gate) * up).astype(BF16)
        acc = acc + _dot(act, wd_ref[sl, :])
    return x + 0.5 * acc


def _ffn_proj_body(x_ref, g_ref, wg_ref, wu_ref, wd_ref, pg_ref, win_ref, x1_ref, proj_ref):
    x1 = _swiglu_half_step(x_ref[...], g_ref, wg_ref, wu_ref, wd_ref)
    x1_ref[...] = x1
    h = _rmsnorm_f32(x1, pg_ref[...]).astype(BF16)
    for c in range(IN_WIDTH // PROJ_TN):
        lo = c * PROJ_TN
        y = _dot(h, win_ref[:, lo:lo + PROJ_TN])
        if lo == O_NQ or lo == O_DQ:
            y = y * (NA_HEAD_DIM ** -0.5)
        proj_ref[:, lo:lo + PROJ_TN] = y.astype(BF16)


def _ffn_proj(x, norm_g, wg, wu, wd, mix_g, w_in):
    n = x.shape[0]
    row = lambda i: (i, 0)
    return pl.pallas_call(
        _ffn_proj_body,
        grid=(n // FFN_TM,),
        in_specs=[
            pl.BlockSpec((FFN_TM, D_MODEL), row),
            _resident((1, D_MODEL)),
            _resident((D_MODEL, D_FF)),
            _resident((D_MODEL, D_FF)),
            _resident((D_FF, D_MODEL)),
            _resident((1, D_MODEL)),
            _resident((D_MODEL, IN_WIDTH)),
        ],
        out_specs=[pl.BlockSpec((FFN_TM, D_MODEL), row), pl.BlockSpec((FFN_TM, IN_WIDTH), row)],
        out_shape=[jax.ShapeDtypeStruct((n, D_MODEL), F32),
                   jax.ShapeDtypeStruct((n, IN_WIDTH), BF16)],
        compiler_params=pltpu.CompilerParams(
            dimension_semantics=("arbitrary",), vmem_limit_bytes=V7X_VMEM_LIMIT_BYTES),
        name="ffn_proj",
    )(x, norm_g, wg, wu, wd, mix_g, w_in)


def _memkv_body(m_ref, g_ref, w_ref, k_ref, v_ref):
    h = _rmsnorm_f32(m_ref[...], g_ref[...]).astype(BF16)
    k_ref[...] = _dot(h, w_ref[:, :MEM_WIDTH]).astype(BF16)
    v_ref[...] = _dot(h, w_ref[:, MEM_WIDTH:]).astype(BF16)


def _memkv(mem2d, norm_g, w_kv, batch, m_tokens):
    row = lambda i: (i, 0)
    shp = jax.ShapeDtypeStruct((batch * m_tokens, MEM_WIDTH), BF16)
    return pl.pallas_call(
        _memkv_body,
        grid=(batch,),
        in_specs=[
            pl.BlockSpec((m_tokens, D_MODEL), row),
            _resident((1, D_MODEL)),
            _resident((D_MODEL, 2 * MEM_WIDTH)),
        ],
        out_specs=[pl.BlockSpec((m_tokens, MEM_WIDTH), row)] * 2,
        out_shape=[shp, shp],
        compiler_params=pltpu.CompilerParams(dimension_semantics=("arbitrary",)),
        name="memkv",
    )(mem2d, norm_g, w_kv)


RPB_ROWS = 2 * NA_WIN_ROWS - 1
RPB_COLS = 2 * NA_WIN_COLS - 1
NA_BAND_KEYS = NA_WIN_ROWS * GRID_W
NA_BATCH_CHUNK = 4
NA_ITEM_ROWS = 4


def _na_body(rpb_ref, q_ref, k_ref, v_ref, o_ref, tile_ref, band_ref,
             lg0_ref, lg1_ref, m0_ref, m1_ref, a0_ref, a1_ref, *, rows, nbatch):
    lg_ref, m_ref, a_ref = (lg0_ref, lg1_ref), (m0_ref, m1_ref), (a0_ref, a1_ref)
    hp = pl.program_id(0)
    b = pl.program_id(1)
    wr = min(NA_WIN_ROWS, rows)

    @pl.when(b == 0)
    def _build_bias():
        c = lax.broadcasted_iota(jnp.int32, (GRID_W, LANES), 0)
        kc = lax.broadcasted_iota(jnp.int32, (GRID_W, LANES), 1) & (GRID_W - 1)
        rel = kc - c
        cs = jnp.clip(c - NA_WIN_COLS // 2, 0, GRID_W - NA_WIN_COLS)
        valid = (kc >= cs) & (kc < cs + NA_WIN_COLS)

        def tile_step(t, carry):
            hl = t // RPB_ROWS
            dr = t - hl * RPB_ROWS
            base = ((hp * 2 + hl) * RPB_ROWS + dr) * RPB_COLS
            acc = jnp.full((GRID_W, LANES), NEG_BIG, F32)
            for dc in range(RPB_COLS):
                acc = jnp.where(rel == dc - (NA_WIN_COLS - 1), rpb_ref[base + dc], acc)
            tile_ref[hl, dr] = jnp.where(valid, acc, NEG_BIG)
            return carry

        lax.fori_loop(0, 2 * RPB_ROWS, tile_step, 0)
        low = lax.broadcasted_iota(jnp.int32, (GRID_W, LANES), 1) < GRID_W
        for hl in range(2):
            for dr0 in range(NA_WIN_ROWS):
                for j in range(wr // 2):
                    band_ref[dr0, hl * GRID_W:(hl + 1) * GRID_W, j * LANES:(j + 1) * LANES] = (
                        jnp.where(low, tile_ref[hl, dr0 + 2 * j], tile_ref[hl, dr0 + 2 * j + 1]))

    lane = lax.broadcasted_iota(jnp.int32, (GRID_W, LANES), 1)
    low = lane < NA_HEAD_DIM

    items_per_batch = rows // NA_ITEM_ROWS
    ipb_shift = items_per_batch.bit_length() - 1

    def locate(item, rr):
        if isinstance(item, int):
            bl, it = divmod(item, items_per_batch)
        else:
            bl, it = lax.shift_right_logical(item, ipb_shift), item & (items_per_batch - 1)
        return it * NA_ITEM_ROWS + rr, bl * (rows * GRID_W)

    def band_rows(r, base):
        if isinstance(r, int):
            rs = min(max(r - wr // 2, 0), rows - wr)
        else:
            rs = jnp.clip(r - wr // 2, 0, rows - wr)
        return rs, pl.ds(_aligned(base + rs * GRID_W, GRID_W), wr * GRID_W)

    def query_rows(r, base):
        return pl.ds(_aligned(base + r * GRID_W, GRID_W), GRID_W)

    def logits_stage(item, slot):
        for rr in range(NA_ITEM_ROWS):
            r, base = locate(item, rr)
            rs, keys = band_rows(r, base)
            q2 = q_ref[query_rows(r, base), :]
            zero = jnp.zeros_like(q2)
            q_st = jnp.concatenate([jnp.where(low, q2, zero), jnp.where(low, zero, q2)], axis=0)
            lg = _dot_nt(q_st, k_ref[keys, :]) + band_ref[rs - r + (NA_WIN_ROWS - 1)]
            lg_ref[slot][rr] = lg
            lane_max = lg[:, :LANES]
            for c in range(1, NA_BAND_KEYS // LANES):
                lane_max = jnp.maximum(lane_max, lg[:, c * LANES:(c + 1) * LANES])
            m_ref[slot][rr] = lane_max

    def value_stage(item, slot):
        for rr in range(NA_ITEM_ROWS):
            r, base = locate(item, rr)
            _, keys = band_rows(r, base)
            row_max = jnp.max(m_ref[slot][rr], axis=-1, keepdims=True)
            e = jnp.exp(lg_ref[slot][rr] - row_max)
            lane_sum = e[:, :LANES]
            for c in range(1, NA_BAND_KEYS // LANES):
                lane_sum = lane_sum + e[:, c * LANES:(c + 1) * LANES]
            a_ref[slot][rr, :, :LANES] = _dot(e.astype(BF16), v_ref[keys, :])
            a_ref[slot][rr, :, LANES:] = lane_sum

    def finish_stage(item, slot):
        for rr in range(NA_ITEM_ROWS):
            r, base = locate(item, rr)
            acc = a_ref[slot][rr]
            o_st = acc[:, :LANES] * (1.0 / jnp.sum(acc[:, LANES:], axis=-1, keepdims=True))
            o = jnp.where(low, o_st[:GRID_W], o_st[GRID_W:])
            o_ref[query_rows(r, base), :] = o.astype(BF16)

    _software_pipeline(nbatch * items_per_batch, (logits_stage, value_stage, finish_stage),
                       order=(2, 0, 1))


def _na(proj, rpb_flat, batch, seq):
    rows = seq // GRID_W
    items_per_batch = rows // NA_ITEM_ROWS
    nbatch = math.gcd(batch, NA_BATCH_CHUNK)
    assert rows >= NA_WIN_ROWS and NA_WIN_ROWS % 2 == 0 and rows % NA_ITEM_ROWS == 0
    assert items_per_batch & (items_per_batch - 1) == 0
    pairs = NA_HEADS // 2
    blk = lambda off: pl.BlockSpec((nbatch * seq, LANES), lambda hp, b, off=off: (b, off + hp))
    return pl.pallas_call(
        functools.partial(_na_body, rows=rows, nbatch=nbatch),
        grid=(pairs, batch // nbatch),
        in_specs=[
            pl.BlockSpec(memory_space=pltpu.SMEM),
            blk(O_NQ // LANES), blk(O_NK // LANES), blk(O_NV // LANES),
        ],
        out_specs=pl.BlockSpec((nbatch * seq, LANES), lambda hp, b: (b, hp)),
        out_shape=jax.ShapeDtypeStruct((batch * seq, NA_WIDTH), BF16),
        scratch_shapes=[
            pltpu.VMEM((2, RPB_ROWS, GRID_W, LANES), F32),
            pltpu.VMEM((NA_WIN_ROWS, 2 * GRID_W, NA_BAND_KEYS), F32),
            pltpu.VMEM((NA_ITEM_ROWS, 2 * GRID_W, NA_BAND_KEYS), F32),
            pltpu.VMEM((NA_ITEM_ROWS, 2 * GRID_W, NA_BAND_KEYS), F32),
            pltpu.VMEM((NA_ITEM_ROWS, 2 * GRID_W, LANES), F32),
            pltpu.VMEM((NA_ITEM_ROWS, 2 * GRID_W, LANES), F32),
            pltpu.VMEM((NA_ITEM_ROWS, 2 * GRID_W, 2 * LANES), F32),
            pltpu.VMEM((NA_ITEM_ROWS, 2 * GRID_W, 2 * LANES), F32),
        ],
        compiler_params=pltpu.CompilerParams(
            dimension_semantics=("arbitrary", "arbitrary"), vmem_limit_bytes=V7X_VMEM_LIMIT_BYTES),
        name="na_attn",
    )(rpb_flat, proj, proj, proj)


def _diff_body(slope_ref, lq1_ref, lk1_ref, lq2_ref, lk2_ref, sub_ref, q_ref, k_ref, v_ref, o_ref,
               dist_ref, v1_ref, s0_ref, s1_ref, m0_ref, m1_ref, a0_ref, a1_ref,
               *, lam_init, seq, nbatch):
    s_ref, m_ref, a_ref = (s0_ref, s1_ref), (m0_ref, m1_ref), (a0_ref, a1_ref)
    h = pl.program_id(0)
    b = pl.program_id(1)
    nq = seq // DIFF_TQ

    @pl.when(b == 0)
    def _build_dist():
        slope = slope_ref[h]
        rel = (lax.broadcasted_iota(jnp.int32, (DIFF_TQ, DIFF_TQ), 0)
               - lax.broadcasted_iota(jnp.int32, (DIFF_TQ, DIFF_TQ), 1))
        for d in range(2 * nq - 1):
            off = (d - (nq - 1)) * DIFF_TQ
            dist_ref[d] = slope * jnp.abs((rel + off).astype(F32))

    lam = (jnp.exp(jnp.sum(lq1_ref[...] * lk1_ref[...], axis=-1, keepdims=True))
           - jnp.exp(jnp.sum(lq2_ref[...] * lk2_ref[...], axis=-1, keepdims=True))
           + lam_init)
    lane = lax.broadcasted_iota(jnp.int32, (DIFF_TQ, LANES), 1)
    first_map = lane < DIFF_QK_DIM
    nq_shift = nq.bit_length() - 1

    def q_rows(j):
        return pl.ds(_aligned(j * DIFF_TQ, DIFF_TQ), DIFF_TQ)

    def batch_rows(j):
        bl = j // nq if isinstance(j, int) else lax.shift_right_logical(j, nq_shift)
        return pl.ds(_aligned(bl * seq, seq), seq)

    def logits_stage(j, slot):
        i = j % nq if isinstance(j, int) else j & (nq - 1)
        q = q_ref[q_rows(j), :]
        zero = jnp.zeros_like(q)
        k = k_ref[batch_rows(j), :]
        dist = jnp.concatenate([dist_ref[i - c + (nq - 1)] for c in range(nq)], axis=1)
        for m, qm in enumerate((jnp.where(first_map, q, zero), jnp.where(first_map, zero, q))):
            s = _dot_nt(qm, k) - dist
            s_ref[slot][m] = s
            lane_max = s[:, :LANES]
            for c in range(1, seq // LANES):
                lane_max = jnp.maximum(lane_max, s[:, c * LANES:(c + 1) * LANES])
            m_ref[slot][m] = lane_max

    def value_stage(j, slot):
        e = jnp.concatenate(
            [jnp.exp(s_ref[slot][m] - jnp.max(m_ref[slot][m], axis=-1, keepdims=True)).astype(BF16)
             for m in range(2)], axis=0)
        a_ref[slot][...] = _dot(e, v1_ref[batch_rows(j), :])

    def finish_stage(j, slot):
        acc = a_ref[slot][...]
        num1, den1 = acc[:DIFF_TQ, :DIFF_V_DIM], acc[:DIFF_TQ, DIFF_V_DIM:DIFF_V_DIM + 1]
        num2, den2 = acc[DIFF_TQ:, :DIFF_V_DIM], acc[DIFF_TQ:, DIFF_V_DIM:DIFF_V_DIM + 1]
        o = num1 * (1.0 / den1) - (lam * (1.0 / den2)) * num2
        o = _rmsnorm_f32(o, sub_ref[...]) * (1.0 - lam_init)
        o_ref[q_rows(j), :] = o.astype(BF16)

    v = v_ref[...]
    v1_ref[:, :DIFF_V_DIM] = v
    v1_ref[:, DIFF_V_DIM:] = jnp.where(
        lax.broadcasted_iota(jnp.int32, v.shape, 1) == 0, 1.0, 0.0).astype(BF16)
    _software_pipeline(nbatch * nq, (logits_stage, value_stage, finish_stage), order=(2, 0, 1))


def _diff(proj, slopes, lq1, lk1, lq2, lk2, subln, batch, seq, lam_init):
    nq = seq // DIFF_TQ
    nbatch = math.gcd(batch, DIFF_BATCH_CHUNK)
    assert nq & (nq - 1) == 0
    vec = lambda w: pl.BlockSpec((1, w), lambda h, b: (0, 0))
    blk = lambda off: pl.BlockSpec((nbatch * seq, LANES), lambda h, b, off=off: (b, off + h))
    return pl.pallas_call(
        functools.partial(_diff_body, lam_init=lam_init, seq=seq, nbatch=nbatch),
        grid=(DIFF_HEADS, batch // nbatch),
        in_specs=[
            pl.BlockSpec(memory_space=pltpu.SMEM),
            vec(DIFF_QK_DIM), vec(DIFF_QK_DIM), vec(DIFF_QK_DIM), vec(DIFF_QK_DIM), vec(DIFF_V_DIM),
            blk(O_DQ // LANES), blk(O_DK // LANES), blk(O_DV // LANES),
        ],
        out_specs=pl.BlockSpec((nbatch * seq, LANES), lambda h, b: (b, h)),
        out_shape=jax.ShapeDtypeStruct((batch * seq, DIFF_V_WIDTH), BF16),
        scratch_shapes=[
            pltpu.VMEM((2 * nq - 1, DIFF_TQ, DIFF_TQ), F32),
            pltpu.VMEM((nbatch * seq, 2 * LANES), BF16),
            pltpu.VMEM((2, DIFF_TQ, seq), F32), pltpu.VMEM((2, DIFF_TQ, seq), F32),
            pltpu.VMEM((2, DIFF_TQ, LANES), F32), pltpu.VMEM((2, DIFF_TQ, LANES), F32),
            pltpu.VMEM((2 * DIFF_TQ, 2 * LANES), F32), pltpu.VMEM((2 * DIFF_TQ, 2 * LANES), F32),
        ],
        compiler_params=pltpu.CompilerParams(
            dimension_semantics=("arbitrary", "arbitrary"),
            vmem_limit_bytes=V7X_VMEM_LIMIT_BYTES),
        name="diff_attn",
    )(slopes, lq1, lk1, lq2, lk2, subln, proj, proj, proj)


def _merge_ffn_body(x_ref, mq_ref, ona_ref, odf_ref, mk_ref, mv_ref, g_ref, wgate_ref, bgate_ref,
                    wna_ref, wdf_ref, wmem_ref, wout_ref, fg_ref, wg_ref, wu_ref, wd_ref, ng_ref,
                    o_ref, *, final_norm):
    x = x_ref[...]
    h = _rmsnorm_f32(x, g_ref[...]).astype(BF16)
    heads = []
    for hh in range(MEM_HEADS):
        sl = slice(hh * MEM_HEAD_DIM, (hh + 1) * MEM_HEAD_DIM)
        logits = _dot_nt(mq_ref[:, sl], mk_ref[:, sl]) * (MEM_HEAD_DIM ** -0.5)
        p = _softmax_rows(logits).astype(BF16)
        heads.append(_dot(p, mv_ref[:, sl]))
    o_mem = jnp.concatenate(heads, axis=-1).astype(BF16)
    branches = (
        _dot(ona_ref[...], wna_ref[...]),
        _dot(odf_ref[...], wdf_ref[...]),
        _dot(o_mem, wmem_ref[...]),
    )
    merged = jnp.zeros(x.shape, F32)
    for i, y in enumerate(branches):
        sl = slice(i * D_MODEL, (i + 1) * D_MODEL)
        gate = jax.nn.sigmoid(_dot(h, wgate_ref[:, sl]) + bgate_ref[:, sl])
        merged = merged + gate * y
    x2 = x + _dot(merged.astype(BF16), wout_ref[...])
    y = _swiglu_half_step(x2, fg_ref, wg_ref, wu_ref, wd_ref)
    if final_norm:
        y = _rmsnorm_f32(y, ng_ref[...])
    o_ref[...] = y


def _merge_ffn(x1, proj, o_na, o_diff, mk, mv, norm_g, w_gate, b_gate, w_na, w_df, w_mem, w_out,
               ffn_g, wg, wu, wd, final_g, seq, m_tokens, *, final_norm):
    n = x1.shape[0]
    per_b = seq // MERGE_TM
    row = lambda i: (i, 0)
    return pl.pallas_call(
        functools.partial(_merge_ffn_body, final_norm=final_norm),
        grid=(n // MERGE_TM,),
        in_specs=[
            pl.BlockSpec((MERGE_TM, D_MODEL), row),
            pl.BlockSpec((MERGE_TM, MEM_WIDTH), lambda i: (i, O_MQ // MEM_WIDTH)),
            pl.BlockSpec((MERGE_TM, NA_WIDTH), row),
            pl.BlockSpec((MERGE_TM, DIFF_V_WIDTH), row),
            pl.BlockSpec((m_tokens, MEM_WIDTH), lambda i: (i // per_b, 0)),
            pl.BlockSpec((m_tokens, MEM_WIDTH), lambda i: (i // per_b, 0)),
            _resident((1, D_MODEL)),
            _resident((D_MODEL, 3 * D_MODEL)),
            _resident((1, 3 * D_MODEL)),
            _resident((NA_WIDTH, D_MODEL)),
            _resident((DIFF_V_WIDTH, D_MODEL)),
            _resident((MEM_WIDTH, D_MODEL)),
            _resident((D_MODEL, D_MODEL)),
            _resident((1, D_MODEL)),
            _resident((D_MODEL, D_FF)),
            _resident((D_MODEL, D_FF)),
            _resident((D_FF, D_MODEL)),
            _resident((1, D_MODEL)),
        ],
        out_specs=pl.BlockSpec((MERGE_TM, D_MODEL), row),
        out_shape=jax.ShapeDtypeStruct((n, D_MODEL), F32),
        compiler_params=pltpu.CompilerParams(
            dimension_semantics=("arbitrary",), vmem_limit_bytes=V7X_VMEM_LIMIT_BYTES),
        name="merge_ffn",
    )(x1, proj, o_na, o_diff, mk, mv, norm_g, w_gate, b_gate, w_na, w_df, w_mem, w_out,
      ffn_g, wg, wu, wd, final_g)


def kernel(x, mem, ffn1_norm, ffn1_w_gate, ffn1_w_up, ffn1_w_down, mix_norm, w_in, na_rpb,
           diff_lambda_q1, diff_lambda_k1, diff_lambda_q2, diff_lambda_k2, diff_subln,
           mem_norm, w_mem_kv, w_gate, b_gate, w_br_na, w_br_diff, w_br_mem, w_out,
           ffn2_norm, ffn2_w_gate, ffn2_w_up, ffn2_w_down, final_norm):
    batch, seq, d_model = x.shape
    m_tokens = mem.shape[1]
    depth = ffn1_norm.shape[0]
    assert d_model == D_MODEL and seq % GRID_W == 0
    assert seq % DIFF_TQ == 0 and seq % MERGE_TM == 0 and (batch * seq) % FFN_TM == 0
    slopes = jnp.asarray([2.0 ** (-8.0 * (i + 1) / DIFF_HEADS) for i in range(DIFF_HEADS)], F32)
    bf = lambda w: w.astype(BF16)
    vec = lambda v: v.reshape(1, -1).astype(F32)

    xt = x.reshape(batch * seq, d_model)
    mem2d = mem.reshape(batch * m_tokens, d_model)
    for l in range(depth):
        lam_init = 0.8 - 0.6 * math.exp(-0.3 * l)
        xt, proj = _ffn_proj(xt, vec(ffn1_norm[l]), bf(ffn1_w_gate[l]), bf(ffn1_w_up[l]),
                             bf(ffn1_w_down[l]), vec(mix_norm[l]), bf(w_in[l]))
        mk, mv = _memkv(mem2d, vec(mem_norm[l]), bf(w_mem_kv[l]), batch, m_tokens)
        o_na = _na(proj, na_rpb[l].reshape(-1).astype(F32), batch, seq)
        o_diff = _diff(proj, slopes, vec(diff_lambda_q1[l]), vec(diff_lambda_k1[l]),
                       vec(diff_lambda_q2[l]), vec(diff_lambda_k2[l]), vec(diff_subln[l]),
                       batch, seq, lam_init)
        xt = _merge_ffn(xt, proj, o_na, o_diff, mk, mv, vec(mix_norm[l]), bf(w_gate[l]),
                        vec(b_gate[l]), bf(w_br_na[l]), bf(w_br_diff[l]), bf(w_br_mem[l]),
                        bf(w_out[l]), vec(ffn2_norm[l]), bf(ffn2_w_gate[l]), bf(ffn2_w_up[l]),
                        bf(ffn2_w_down[l]), vec(final_norm), seq, m_tokens,
                        final_norm=(l == depth - 1))
    return xt.reshape(batch, seq, d_model)
```

```python
import functools
import math

import jax
import jax.numpy as jnp
from jax import lax
from jax.experimental import pallas as pl
from jax.experimental.pallas import tpu as pltpu

F32 = jnp.float32
BF16 = jnp.bfloat16

D_MODEL = 1024
GRID_W = 64
NA_HEADS = 8
NA_HEAD_DIM = 64
NA_WIN_ROWS = 8
NA_WIN_COLS = 16
DIFF_HEADS = 4
DIFF_QK_DIM = 64
DIFF_V_DIM = 128
MEM_HEADS = 4
MEM_HEAD_DIM = 128
D_FF = 2816
NORM_EPS = 1e-6
NA_WIDTH = NA_HEADS * NA_HEAD_DIM
DIFF_QK_WIDTH = DIFF_HEADS * 2 * DIFF_QK_DIM
DIFF_V_WIDTH = DIFF_HEADS * DIFF_V_DIM
MEM_WIDTH = MEM_HEADS * MEM_HEAD_DIM
IN_WIDTH = 3 * NA_WIDTH + 2 * DIFF_QK_WIDTH + DIFF_V_WIDTH + MEM_WIDTH
O_NQ = 0
O_NK = O_NQ + NA_WIDTH
O_NV = O_NK + NA_WIDTH
O_DQ = O_NV + NA_WIDTH
O_DK = O_DQ + DIFF_QK_WIDTH
O_DV = O_DK + DIFF_QK_WIDTH
O_MQ = O_DV + DIFF_V_WIDTH

LANES = 128
V7X_VMEM_LIMIT_BYTES = 56 * 1024 * 1024

FFN_TM = 512
STAGE_ROWS = 128
FFN_TF = 256
PROJ_TN = 512
DIFF_TQ = 512
DIFF_BATCH_CHUNK = 4
MERGE_TM = 512
NEG_BIG = -1e30


def _rmsnorm_f32(x, g):
    ms = jnp.mean(x * x, axis=-1, keepdims=True)
    return (x * lax.rsqrt(ms + NORM_EPS)) * g


def _softmax_rows(logits):
    m = jnp.max(logits, axis=-1, keepdims=True)
    e = jnp.exp(logits - m)
    s = jnp.sum(e, axis=-1, keepdims=True)
    return e * (1.0 / s)


def _dot_nt(a, b):
    return lax.dot_general(a, b, (((1,), (1,)), ((), ())), preferred_element_type=F32)


def _dot(a, b):
    return jnp.dot(a, b, preferred_element_type=F32)


def _resident(shape):
    nd = len(shape)
    return pl.BlockSpec(shape, lambda *_: (0,) * nd, pipeline_mode=pl.Buffered(1))


def _aligned(idx, multiple):
    return idx if isinstance(idx, int) else pl.multiple_of(idx, multiple)


def _software_pipeline(n_items, stages, order):
    depth = len(stages)
    assert n_items >= depth and sorted(order) == list(range(depth))

    def trip(t, parity, valid):
        for k in order:
            if valid(k):
                stages[k](t - k, (parity - k) % 2)

    for t in range(depth - 1):
        trip(t, t % 2, lambda k, t=t: k <= t)
    start = depth - 1
    if (n_items - start) % 2:
        trip(start, start % 2, lambda k: True)
        start += 1

    def body(u, carry):
        t = start + 2 * u
        trip(t, start % 2, lambda k: True)
        trip(t + 1, (start + 1) % 2, lambda k: True)
        return carry

    lax.fori_loop(0, (n_items - start) // 2, body, 0)
    for t in range(n_items, n_items + depth - 1):
        trip(t, t % 2, lambda k, t=t: t - k < n_items)


def _stage_copy(w_hbm, stage_ref, sem_ref, chunk, slot):
    width = w_hbm.shape[1]
    return pltpu.make_async_copy(
        w_hbm.at[pl.ds(chunk * STAGE_ROWS, STAGE_ROWS), :],
        stage_ref.at[slot, :, :width],
        sem_ref.at[slot])


def _load_weights_bf16(pairs, stage_ref, sem_ref):
    jobs = [(w, dst, c) for w, dst in pairs for c in range(w.shape[0] // STAGE_ROWS)]
    _stage_copy(jobs[0][0], stage_ref, sem_ref, jobs[0][2], 0).start()
    for n, (w, dst, c) in enumerate(jobs):
        slot = n % 2
        if n + 1 < len(jobs):
            nw, _, nc = jobs[n + 1]
            _stage_copy(nw, stage_ref, sem_ref, nc, 1 - slot).start()
        _stage_copy(w, stage_ref, sem_ref, c, slot).wait()
        dst[c * STAGE_ROWS:(c + 1) * STAGE_ROWS, :] = (
            stage_ref[slot, :, :w.shape[1]].astype(BF16))


_HBM = pl.BlockSpec(memory_space=pl.ANY)


def _swiglu_half_step(x, g_ref, wg_ref, wu_ref, wd_ref):
    h = _rmsnorm_f32(x, g_ref[...]).astype(BF16)
    acc = jnp.zeros(x.shape, F32)
    for c in range(D_FF // FFN_TF):
        sl = slice(c * FFN_TF, (c + 1) * FFN_TF)
        gate = _dot(h, wg_ref[:, sl])
        up = _dot(h, wu_ref[:, sl])
        act = (gate * jax.nn.sigmoid(gate) * up).astype(BF16)
        acc = acc + _dot(act, wd_ref[sl, :])
    return x + 0.5 * acc


def _ffn_proj_body(x_ref, g_ref, wg_hbm, wu_hbm, wd_hbm, pg_ref, win_hbm, x1_ref, proj_ref,
                   wg_ref, wu_ref, wd_ref, win_ref, stage_ref, sem_ref):
    @pl.when(pl.program_id(0) == 0)
    def _load_weights():
        _load_weights_bf16(((wg_hbm, wg_ref), (wu_hbm, wu_ref), (wd_hbm, wd_ref),
                            (win_hbm, win_ref)), stage_ref, sem_ref)

    x1 = _swiglu_half_step(x_ref[...], g_ref, wg_ref, wu_ref, wd_ref)
    x1_ref[...] = x1
    h = _rmsnorm_f32(x1, pg_ref[...]).astype(BF16)
    for c in range(IN_WIDTH // PROJ_TN):
        lo = c * PROJ_TN
        y = _dot(h, win_ref[:, lo:lo + PROJ_TN])
        if lo == O_NQ or lo == O_DQ:
            y = y * (NA_HEAD_DIM ** -0.5)
        proj_ref[:, lo:lo + PROJ_TN] = y.astype(BF16)


def _ffn_proj(x, norm_g, wg, wu, wd, mix_g, w_in):
    n = x.shape[0]
    row = lambda i: (i, 0)
    return pl.pallas_call(
        _ffn_proj_body,
        grid=(n // FFN_TM,),
        in_specs=[
            pl.BlockSpec((FFN_TM, D_MODEL), row),
            _resident((1, D_MODEL)),
            _HBM, _HBM, _HBM,
            _resident((1, D_MODEL)),
            _HBM,
        ],
        out_specs=[pl.BlockSpec((FFN_TM, D_MODEL), row), pl.BlockSpec((FFN_TM, IN_WIDTH), row)],
        out_shape=[jax.ShapeDtypeStruct((n, D_MODEL), F32),
                   jax.ShapeDtypeStruct((n, IN_WIDTH), BF16)],
        scratch_shapes=[
            pltpu.VMEM((D_MODEL, D_FF), BF16), pltpu.VMEM((D_MODEL, D_FF), BF16),
            pltpu.VMEM((D_FF, D_MODEL), BF16), pltpu.VMEM((D_MODEL, IN_WIDTH), BF16),
            pltpu.VMEM((2, STAGE_ROWS, IN_WIDTH), F32),
            pltpu.SemaphoreType.DMA((2,)),
        ],
        compiler_params=pltpu.CompilerParams(
            dimension_semantics=("arbitrary",), vmem_limit_bytes=V7X_VMEM_LIMIT_BYTES),
        name="ffn_proj",
    )(x, norm_g, wg, wu, wd, mix_g, w_in)


def _memkv_body(m_ref, g_ref, w_hbm, k_ref, v_ref, w_ref, stage_ref, sem_ref):
    @pl.when(pl.program_id(0) == 0)
    def _load_weights():
        _load_weights_bf16(((w_hbm, w_ref),), stage_ref, sem_ref)

    h = _rmsnorm_f32(m_ref[...], g_ref[...]).astype(BF16)
    k_ref[...] = _dot(h, w_ref[:, :MEM_WIDTH]).astype(BF16)
    v_ref[...] = _dot(h, w_ref[:, MEM_WIDTH:]).astype(BF16)


def _memkv(mem2d, norm_g, w_kv, batch, m_tokens):
    row = lambda i: (i, 0)
    shp = jax.ShapeDtypeStruct((batch * m_tokens, MEM_WIDTH), BF16)
    return pl.pallas_call(
        _memkv_body,
        grid=(batch,),
        in_specs=[
            pl.BlockSpec((m_tokens, D_MODEL), row),
            _resident((1, D_MODEL)),
            _HBM,
        ],
        out_specs=[pl.BlockSpec((m_tokens, MEM_WIDTH), row)] * 2,
        out_shape=[shp, shp],
        scratch_shapes=[
            pltpu.VMEM((D_MODEL, 2 * MEM_WIDTH), BF16),
            pltpu.VMEM((2, STAGE_ROWS, 2 * MEM_WIDTH), F32),
            pltpu.SemaphoreType.DMA((2,)),
        ],
        compiler_params=pltpu.CompilerParams(dimension_semantics=("arbitrary",)),
        name="memkv",
    )(mem2d, norm_g, w_kv)


RPB_ROWS = 2 * NA_WIN_ROWS - 1
RPB_COLS = 2 * NA_WIN_COLS - 1
NA_BAND_KEYS = NA_WIN_ROWS * GRID_W
NA_BATCH_CHUNK = 4
NA_ITEM_ROWS = 4


def _na_body(rpb_ref, q_ref, k_ref, v_ref, o_ref, tile_ref, band_ref,
             lg0_ref, lg1_ref, m0_ref, m1_ref, a0_ref, a1_ref, *, rows, nbatch):
    lg_ref, m_ref, a_ref = (lg0_ref, lg1_ref), (m0_ref, m1_ref), (a0_ref, a1_ref)
    hp = pl.program_id(0)
    b = pl.program_id(1)
    wr = min(NA_WIN_ROWS, rows)

    @pl.when(b == 0)
    def _build_bias():
        c = lax.broadcasted_iota(jnp.int32, (GRID_W, LANES), 0)
        kc = lax.broadcasted_iota(jnp.int32, (GRID_W, LANES), 1) & (GRID_W - 1)
        rel = kc - c
        cs = jnp.clip(c - NA_WIN_COLS // 2, 0, GRID_W - NA_WIN_COLS)
        valid = (kc >= cs) & (kc < cs + NA_WIN_COLS)

        def tile_step(t, carry):
            hl = t // RPB_ROWS
            dr = t - hl * RPB_ROWS
            base = ((hp * 2 + hl) * RPB_ROWS + dr) * RPB_COLS
            acc = jnp.full((GRID_W, LANES), NEG_BIG, F32)
            for dc in range(RPB_COLS):
                acc = jnp.where(rel == dc - (NA_WIN_COLS - 1), rpb_ref[base + dc], acc)
            tile_ref[hl, dr] = jnp.where(valid, acc, NEG_BIG)
            return carry

        lax.fori_loop(0, 2 * RPB_ROWS, tile_step, 0)
        low = lax.broadcasted_iota(jnp.int32, (GRID_W, LANES), 1) < GRID_W
        for hl in range(2):
            for dr0 in range(NA_WIN_ROWS):
                for j in range(wr // 2):
                    band_ref[dr0, hl * GRID_W:(hl + 1) * GRID_W, j * LANES:(j + 1) * LANES] = (
                        jnp.where(low, tile_ref[hl, dr0 + 2 * j], tile_ref[hl, dr0 + 2 * j + 1]))

    lane = lax.broadcasted_iota(jnp.int32, (GRID_W, LANES), 1)
    low = lane < NA_HEAD_DIM

    items_per_batch = rows // NA_ITEM_ROWS
    ipb_shift = items_per_batch.bit_length() - 1

    def locate(item, rr):
        if isinstance(item, int):
            bl, it = divmod(item, items_per_batch)
        else:
            bl, it = lax.shift_right_logical(item, ipb_shift), item & (items_per_batch - 1)
        return it * NA_ITEM_ROWS + rr, bl * (rows * GRID_W)

    def band_rows(r, base):
        if isinstance(r, int):
            rs = min(max(r - wr // 2, 0), rows - wr)
        else:
            rs = jnp.clip(r - wr // 2, 0, rows - wr)
        return rs, pl.ds(_aligned(base + rs * GRID_W, GRID_W), wr * GRID_W)

    def query_rows(r, base):
        return pl.ds(_aligned(base + r * GRID_W, GRID_W), GRID_W)

    def logits_stage(item, slot):
        for rr in range(NA_ITEM_ROWS):
            r, base = locate(item, rr)
            rs, keys = band_rows(r, base)
            q2 = q_ref[query_rows(r, base), :]
            zero = jnp.zeros_like(q2)
            q_st = jnp.concatenate([jnp.where(low, q2, zero), jnp.where(low, zero, q2)], axis=0)
            lg = _dot_nt(q_st, k_ref[keys, :]) + band_ref[rs - r + (NA_WIN_ROWS - 1)]
            lg_ref[slot][rr] = lg
            lane_max = lg[:, :LANES]
            for c in range(1, NA_BAND_KEYS // LANES):
                lane_max = jnp.maximum(lane_max, lg[:, c * LANES:(c + 1) * LANES])
            m_ref[slot][rr] = lane_max

    def value_stage(item, slot):
        for rr in range(NA_ITEM_ROWS):
            r, base = locate(item, rr)
            _, keys = band_rows(r, base)
            row_max = jnp.max(m_ref[slot][rr], axis=-1, keepdims=True)
            e = jnp.exp(lg_ref[slot][rr] - row_max)
            lane_sum = e[:, :LANES]
            for c in range(1, NA_BAND_KEYS // LANES):
                lane_sum = lane_sum + e[:, c * LANES:(c + 1) * LANES]
            a_ref[slot][rr, :, :LANES] = _dot(e.astype(BF16), v_ref[keys, :])
            a_ref[slot][rr, :, LANES:] = lane_sum

    def finish_stage(item, slot):
        for rr in range(NA_ITEM_ROWS):
            r, base = locate(item, rr)
            acc = a_ref[slot][rr]
            o_st = acc[:, :LANES] * (1.0 / jnp.sum(acc[:, LANES:], axis=-1, keepdims=True))
            o = jnp.where(low, o_st[:GRID_W], o_st[GRID_W:])
            o_ref[query_rows(r, base), :] = o.astype(BF16)

    _software_pipeline(nbatch * items_per_batch, (logits_stage, value_stage, finish_stage),
                       order=(2, 0, 1))


def _na(proj, rpb_flat, batch, seq):
    rows = seq // GRID_W
    items_per_batch = rows // NA_ITEM_ROWS
    nbatch = math.gcd(batch, NA_BATCH_CHUNK)
    assert rows >= NA_WIN_ROWS and NA_WIN_ROWS % 2 == 0 and rows % NA_ITEM_ROWS == 0
    assert items_per_batch & (items_per_batch - 1) == 0
    pairs = NA_HEADS // 2
    blk = lambda off: pl.BlockSpec((nbatch * seq, LANES), lambda hp, b, off=off: (b, off + hp))
    return pl.pallas_call(
        functools.partial(_na_body, rows=rows, nbatch=nbatch),
        grid=(pairs, batch // nbatch),
        in_specs=[
            pl.BlockSpec(memory_space=pltpu.SMEM),
            blk(O_NQ // LANES), blk(O_NK // LANES), blk(O_NV // LANES),
        ],
        out_specs=pl.BlockSpec((nbatch * seq, LANES), lambda hp, b: (b, hp)),
        out_shape=jax.ShapeDtypeStruct((batch * seq, NA_WIDTH), BF16),
        scratch_shapes=[
            pltpu.VMEM((2, RPB_ROWS, GRID_W, LANES), F32),
            pltpu.VMEM((NA_WIN_ROWS, 2 * GRID_W, NA_BAND_KEYS), F32),
            pltpu.VMEM((NA_ITEM_ROWS, 2 * GRID_W, NA_BAND_KEYS), F32),
            pltpu.VMEM((NA_ITEM_ROWS, 2 * GRID_W, NA_BAND_KEYS), F32),
            pltpu.VMEM((NA_ITEM_ROWS, 2 * GRID_W, LANES), F32),
            pltpu.VMEM((NA_ITEM_ROWS, 2 * GRID_W, LANES), F32),
            pltpu.VMEM((NA_ITEM_ROWS, 2 * GRID_W, 2 * LANES), F32),
            pltpu.VMEM((NA_ITEM_ROWS, 2 * GRID_W, 2 * LANES), F32),
        ],
        compiler_params=pltpu.CompilerParams(
            dimension_semantics=("arbitrary", "arbitrary"), vmem_limit_bytes=V7X_VMEM_LIMIT_BYTES),
        name="na_attn",
    )(rpb_flat, proj, proj, proj)


def _diff_body(slope_ref, lq1_ref, lk1_ref, lq2_ref, lk2_ref, sub_ref, q_ref, k_ref, v_ref, o_ref,
               dist_ref, v1_ref, s0_ref, s1_ref, m0_ref, m1_ref, a0_ref, a1_ref,
               *, lam_init, seq, nbatch):
    s_ref, m_ref, a_ref = (s0_ref, s1_ref), (m0_ref, m1_ref), (a0_ref, a1_ref)
    h = pl.program_id(0)
    b = pl.program_id(1)
    nq = seq // DIFF_TQ

    @pl.when(b == 0)
    def _build_dist():
        slope = slope_ref[h]
        rel = (lax.broadcasted_iota(jnp.int32, (DIFF_TQ, DIFF_TQ), 0)
               - lax.broadcasted_iota(jnp.int32, (DIFF_TQ, DIFF_TQ), 1))
        for d in range(2 * nq - 1):
            off = (d - (nq - 1)) * DIFF_TQ
            dist_ref[d] = slope * jnp.abs((rel + off).astype(F32))

    lam = (jnp.exp(jnp.sum(lq1_ref[...] * lk1_ref[...], axis=-1, keepdims=True))
           - jnp.exp(jnp.sum(lq2_ref[...] * lk2_ref[...], axis=-1, keepdims=True))
           + lam_init)
    lane = lax.broadcasted_iota(jnp.int32, (DIFF_TQ, LANES), 1)
    first_map = lane < DIFF_QK_DIM
    nq_shift = nq.bit_length() - 1

    def q_rows(j):
        return pl.ds(_aligned(j * DIFF_TQ, DIFF_TQ), DIFF_TQ)

    def batch_rows(j):
        bl = j // nq if isinstance(j, int) else lax.shift_right_logical(j, nq_shift)
        return pl.ds(_aligned(bl * seq, seq), seq)

    def logits_stage(j, slot):
        i = j % nq if isinstance(j, int) else j & (nq - 1)
        q = q_ref[q_rows(j), :]
        zero = jnp.zeros_like(q)
        k = k_ref[batch_rows(j), :]
        dist = jnp.concatenate([dist_ref[i - c + (nq - 1)] for c in range(nq)], axis=1)
        for m, qm in enumerate((jnp.where(first_map, q, zero), jnp.where(first_map, zero, q))):
            s = _dot_nt(qm, k) - dist
            s_ref[slot][m] = s
            lane_max = s[:, :LANES]
            for c in range(1, seq // LANES):
                lane_max = jnp.maximum(lane_max, s[:, c * LANES:(c + 1) * LANES])
            m_ref[slot][m] = lane_max

    def value_stage(j, slot):
        e = jnp.concatenate(
            [jnp.exp(s_ref[slot][m] - jnp.max(m_ref[slot][m], axis=-1, keepdims=True)).astype(BF16)
             for m in range(2)], axis=0)
        a_ref[slot][...] = _dot(e, v1_ref[batch_rows(j), :])

    def finish_stage(j, slot):
        acc = a_ref[slot][...]
        num1, den1 = acc[:DIFF_TQ, :DIFF_V_DIM], acc[:DIFF_TQ, DIFF_V_DIM:DIFF_V_DIM + 1]
        num2, den2 = acc[DIFF_TQ:, :DIFF_V_DIM], acc[DIFF_TQ:, DIFF_V_DIM:DIFF_V_DIM + 1]
        o = num1 * (1.0 / den1) - (lam * (1.0 / den2)) * num2
        o = _rmsnorm_f32(o, sub_ref[...]) * (1.0 - lam_init)
        o_ref[q_rows(j), :] = o.astype(BF16)

    v = v_ref[...]
    v1_ref[:, :DIFF_V_DIM] = v
    v1_ref[:, DIFF_V_DIM:] = jnp.where(
        lax.broadcasted_iota(jnp.int32, v.shape, 1) == 0, 1.0, 0.0).astype(BF16)
    _software_pipeline(nbatch * nq, (logits_stage, value_stage, finish_stage), order=(2, 0, 1))


def _diff(proj, slopes, lq1, lk1, lq2, lk2, subln, batch, seq, lam_init):
    nq = seq // DIFF_TQ
    nbatch = math.gcd(batch, DIFF_BATCH_CHUNK)
    assert nq & (nq - 1) == 0
    vec = lambda w: pl.BlockSpec((1, w), lambda h, b: (0, 0))
    blk = lambda off: pl.BlockSpec((nbatch * seq, LANES), lambda h, b, off=off: (b, off + h))
    return pl.pallas_call(
        functools.partial(_diff_body, lam_init=lam_init, seq=seq, nbatch=nbatch),
        grid=(DIFF_HEADS, batch // nbatch),
        in_specs=[
            pl.BlockSpec(memory_space=pltpu.SMEM),
            vec(DIFF_QK_DIM), vec(DIFF_QK_DIM), vec(DIFF_QK_DIM), vec(DIFF_QK_DIM), vec(DIFF_V_DIM),
            blk(O_DQ // LANES), blk(O_DK // LANES), blk(O_DV // LANES),
        ],
        out_specs=pl.BlockSpec((nbatch * seq, LANES), lambda h, b: (b, h)),
        out_shape=jax.ShapeDtypeStruct((batch * seq, DIFF_V_WIDTH), BF16),
        scratch_shapes=[
            pltpu.VMEM((2 * nq - 1, DIFF_TQ, DIFF_TQ), F32),
            pltpu.VMEM((nbatch * seq, 2 * LANES), BF16),
            pltpu.VMEM((2, DIFF_TQ, seq), F32), pltpu.VMEM((2, DIFF_TQ, seq), F32),
            pltpu.VMEM((2, DIFF_TQ, LANES), F32), pltpu.VMEM((2, DIFF_TQ, LANES), F32),
            pltpu.VMEM((2 * DIFF_TQ, 2 * LANES), F32), pltpu.VMEM((2 * DIFF_TQ, 2 * LANES), F32),
        ],
        compiler_params=pltpu.CompilerParams(
            dimension_semantics=("arbitrary", "arbitrary"),
            vmem_limit_bytes=V7X_VMEM_LIMIT_BYTES),
        name="diff_attn",
    )(slopes, lq1, lk1, lq2, lk2, subln, proj, proj, proj)


def _merge_ffn_body(x_ref, mq_ref, ona_ref, odf_ref, mk_ref, mv_ref, g_ref, wgate_hbm, bgate_ref,
                    wna_hbm, wdf_hbm, wmem_hbm, wout_hbm, fg_ref, wg_hbm, wu_hbm, wd_hbm, ng_ref,
                    o_ref, wgate_ref, wna_ref, wdf_ref, wmem_ref, wout_ref, wg_ref, wu_ref, wd_ref,
                    stage_ref, sem_ref, *, final_norm):
    @pl.when(pl.program_id(0) == 0)
    def _load_weights():
        _load_weights_bf16(
            ((wgate_hbm, wgate_ref), (wna_hbm, wna_ref), (wdf_hbm, wdf_ref), (wmem_hbm, wmem_ref),
             (wout_hbm, wout_ref), (wg_hbm, wg_ref), (wu_hbm, wu_ref), (wd_hbm, wd_ref)),
            stage_ref, sem_ref)

    x = x_ref[...]
    h = _rmsnorm_f32(x, g_ref[...]).astype(BF16)
    heads = []
    for hh in range(MEM_HEADS):
        sl = slice(hh * MEM_HEAD_DIM, (hh + 1) * MEM_HEAD_DIM)
        logits = _dot_nt(mq_ref[:, sl], mk_ref[:, sl]) * (MEM_HEAD_DIM ** -0.5)
        p = _softmax_rows(logits).astype(BF16)
        heads.append(_dot(p, mv_ref[:, sl]))
    o_mem = jnp.concatenate(heads, axis=-1).astype(BF16)
    branches = (
        _dot(ona_ref[...], wna_ref[...]),
        _dot(odf_ref[...], wdf_ref[...]),
        _dot(o_mem, wmem_ref[...]),
    )
    merged = jnp.zeros(x.shape, F32)
    for i, y in enumerate(branches):
        sl = slice(i * D_MODEL, (i + 1) * D_MODEL)
        gate = jax.nn.sigmoid(_dot(h, wgate_ref[:, sl]) + bgate_ref[:, sl])
        merged = merged + gate * y
    x2 = x + _dot(merged.astype(BF16), wout_ref[...])
    y = _swiglu_half_step(x2, fg_ref, wg_ref, wu_ref, wd_ref)
    if final_norm:
        y = _rmsnorm_f32(y, ng_ref[...])
    o_ref[...] = y


def _merge_ffn(x1, proj, o_na, o_diff, mk, mv, norm_g, w_gate, b_gate, w_na, w_df, w_mem, w_out,
               ffn_g, wg, wu, wd, final_g, seq, m_tokens, *, final_norm):
    n = x1.shape[0]
    per_b = seq // MERGE_TM
    row = lambda i: (i, 0)
    return pl.pallas_call(
        functools.partial(_merge_ffn_body, final_norm=final_norm),
        grid=(n // MERGE_TM,),
        in_specs=[
            pl.BlockSpec((MERGE_TM, D_MODEL), row),
            pl.BlockSpec((MERGE_TM, MEM_WIDTH), lambda i: (i, O_MQ // MEM_WIDTH)),
            pl.BlockSpec((MERGE_TM, NA_WIDTH), row),
            pl.BlockSpec((MERGE_TM, DIFF_V_WIDTH), row),
            pl.BlockSpec((m_tokens, MEM_WIDTH), lambda i: (i // per_b, 0)),
            pl.BlockSpec((m_tokens, MEM_WIDTH), lambda i: (i // per_b, 0)),
            _resident((1, D_MODEL)),
            _HBM,
            _resident((1, 3 * D_MODEL)),
            _HBM, _HBM, _HBM, _HBM,
            _resident((1, D_MODEL)),
            _HBM, _HBM, _HBM,
            _resident((1, D_MODEL)),
        ],
        out_specs=pl.BlockSpec((MERGE_TM, D_MODEL), row),
        out_shape=jax.ShapeDtypeStruct((n, D_MODEL), F32),
        scratch_shapes=[
            pltpu.VMEM((D_MODEL, 3 * D_MODEL), BF16),
            pltpu.VMEM((NA_WIDTH, D_MODEL), BF16), pltpu.VMEM((DIFF_V_WIDTH, D_MODEL), BF16),
            pltpu.VMEM((MEM_WIDTH, D_MODEL), BF16), pltpu.VMEM((D_MODEL, D_MODEL), BF16),
            pltpu.VMEM((D_MODEL, D_FF), BF16), pltpu.VMEM((D_MODEL, D_FF), BF16),
            pltpu.VMEM((D_FF, D_MODEL), BF16),
            pltpu.VMEM((2, STAGE_ROWS, 3 * D_MODEL), F32),
            pltpu.SemaphoreType.DMA((2,)),
        ],
        compiler_params=pltpu.CompilerParams(
            dimension_semantics=("arbitrary",), vmem_limit_bytes=V7X_VMEM_LIMIT_BYTES),
        name="merge_ffn",
    )(x1, proj, o_na, o_diff, mk, mv, norm_g, w_gate, b_gate, w_na, w_df, w_mem, w_out,
      ffn_g, wg, wu, wd, final_g)


def kernel(x, mem, ffn1_norm, ffn1_w_gate, ffn1_w_up, ffn1_w_down, mix_norm, w_in, na_rpb,
           diff_lambda_q1, diff_lambda_k1, diff_lambda_q2, diff_lambda_k2, diff_subln,
           mem_norm, w_mem_kv, w_gate, b_gate, w_br_na, w_br_diff, w_br_mem, w_out,
           ffn2_norm, ffn2_w_gate, ffn2_w_up, ffn2_w_down, final_norm):
    batch, seq, d_model = x.shape
    m_tokens = mem.shape[1]
    depth = ffn1_norm.shape[0]
    assert d_model == D_MODEL and seq % GRID_W == 0
    assert seq % DIFF_TQ == 0 and seq % MERGE_TM == 0 and (batch * seq) % FFN_TM == 0
    slopes = jnp.asarray([2.0 ** (-8.0 * (i + 1) / DIFF_HEADS) for i in range(DIFF_HEADS)], F32)
    w32 = lambda w: w.astype(F32)
    vec = lambda v: v.reshape(1, -1).astype(F32)

    xt = x.reshape(batch * seq, d_model)
    mem2d = mem.reshape(batch * m_tokens, d_model)
    for l in range(depth):
        lam_init = 0.8 - 0.6 * math.exp(-0.3 * l)
        xt, proj = _ffn_proj(xt, vec(ffn1_norm[l]), w32(ffn1_w_gate[l]), w32(ffn1_w_up[l]),
                             w32(ffn1_w_down[l]), vec(mix_norm[l]), w32(w_in[l]))
        mk, mv = _memkv(mem2d, vec(mem_norm[l]), w32(w_mem_kv[l]), batch, m_tokens)
        o_na = _na(proj, na_rpb[l].reshape(-1).astype(F32), batch, seq)
        o_diff = _diff(proj, slopes, vec(diff_lambda_q1[l]), vec(diff_lambda_k1[l]),
                       vec(diff_lambda_q2[l]), vec(diff_lambda_k2[l]), vec(diff_subln[l]),
                       batch, seq, lam_init)
        xt = _merge_ffn(xt, proj, o_na, o_diff, mk, mv, vec(mix_norm[l]), w32(w_gate[l]),
                        vec(b_gate[l]), w32(w_br_na[l]), w32(w_br_diff[l]), w32(w_br_mem[l]),
                        w32(w_out[l]), vec(ffn2_norm[l]), w32(ffn2_w_gate[l]), w32(ffn2_w_up[l]),
                        w32(ffn2_w_down[l]), vec(final_norm), seq, m_tokens,
                        final_norm=(l == depth - 1))
    return xt.reshape(batch, seq, d_model)
```

```python
import functools
import math

import jax
import jax.numpy as jnp
from jax import lax
from jax.experimental import pallas as pl
from jax.experimental.pallas import tpu as pltpu

F32 = jnp.float32
BF16 = jnp.bfloat16

D_MODEL = 1024
GRID_W = 64
NA_HEADS = 8
NA_HEAD_DIM = 64
NA_WIN_ROWS = 8
NA_WIN_COLS = 16
DIFF_HEADS = 4
DIFF_QK_DIM = 64
DIFF_V_DIM = 128
MEM_HEADS = 4
MEM_HEAD_DIM = 128
D_FF = 2816
NORM_EPS = 1e-6
NA_WIDTH = NA_HEADS * NA_HEAD_DIM
DIFF_QK_WIDTH = DIFF_HEADS * 2 * DIFF_QK_DIM
DIFF_V_WIDTH = DIFF_HEADS * DIFF_V_DIM
MEM_WIDTH = MEM_HEADS * MEM_HEAD_DIM
IN_WIDTH = 3 * NA_WIDTH + 2 * DIFF_QK_WIDTH + DIFF_V_WIDTH + MEM_WIDTH
O_NQ = 0
O_NK = O_NQ + NA_WIDTH
O_NV = O_NK + NA_WIDTH
O_DQ = O_NV + NA_WIDTH
O_DK = O_DQ + DIFF_QK_WIDTH
O_DV = O_DK + DIFF_QK_WIDTH
O_MQ = O_DV + DIFF_V_WIDTH

LANES = 128
V7X_VMEM_LIMIT_BYTES = 56 * 1024 * 1024

FFN_TM = 512
STAGE_ROWS = 128
STAGE_SLOTS = 4
FFN_TF = 256
PROJ_TN = 512
DIFF_TQ = 512
DIFF_BATCH_CHUNK = 4
MERGE_TM = 512
NEG_BIG = -1e30


def _rmsnorm_f32(x, g):
    ms = jnp.mean(x * x, axis=-1, keepdims=True)
    return (x * lax.rsqrt(ms + NORM_EPS)) * g


def _softmax_rows(logits):
    m = jnp.max(logits, axis=-1, keepdims=True)
    e = jnp.exp(logits - m)
    s = jnp.sum(e, axis=-1, keepdims=True)
    return e * (1.0 / s)


def _dot_nt(a, b):
    return lax.dot_general(a, b, (((1,), (1,)), ((), ())), preferred_element_type=F32)


def _dot(a, b):
    return jnp.dot(a, b, preferred_element_type=F32)


def _resident(shape):
    nd = len(shape)
    return pl.BlockSpec(shape, lambda *_: (0,) * nd, pipeline_mode=pl.Buffered(1))


def _aligned(idx, multiple):
    return idx if isinstance(idx, int) else pl.multiple_of(idx, multiple)


def _software_pipeline(n_items, stages, order):
    depth = len(stages)
    assert n_items >= depth and sorted(order) == list(range(depth))

    def trip(t, parity, valid):
        for k in order:
            if valid(k):
                stages[k](t - k, (parity - k) % 2)

    for t in range(depth - 1):
        trip(t, t % 2, lambda k, t=t: k <= t)
    start = depth - 1
    if (n_items - start) % 2:
        trip(start, start % 2, lambda k: True)
        start += 1

    def body(u, carry):
        t = start + 2 * u
        trip(t, start % 2, lambda k: True)
        trip(t + 1, (start + 1) % 2, lambda k: True)
        return carry

    lax.fori_loop(0, (n_items - start) // 2, body, 0)
    for t in range(n_items, n_items + depth - 1):
        trip(t, t % 2, lambda k, t=t: t - k < n_items)


def _stage_copy(w_hbm, stage_ref, sem_ref, chunk, slot):
    width = w_hbm.shape[1]
    return pltpu.make_async_copy(
        w_hbm.at[pl.ds(chunk * STAGE_ROWS, STAGE_ROWS), :],
        stage_ref.at[slot, :, :width],
        sem_ref.at[slot])


def _load_weights_bf16(pairs, stage_ref, sem_ref):
    jobs = [(w, dst, c) for w, dst in pairs for c in range(w.shape[0] // STAGE_ROWS)]
    ahead = STAGE_SLOTS - 1
    for n in range(min(ahead, len(jobs))):
        _stage_copy(jobs[n][0], stage_ref, sem_ref, jobs[n][2], n % STAGE_SLOTS).start()
    for n, (w, dst, c) in enumerate(jobs):
        slot = n % STAGE_SLOTS
        if n + ahead < len(jobs):
            nw, _, nc = jobs[n + ahead]
            _stage_copy(nw, stage_ref, sem_ref, nc, (n + ahead) % STAGE_SLOTS).start()
        _stage_copy(w, stage_ref, sem_ref, c, slot).wait()
        dst[c * STAGE_ROWS:(c + 1) * STAGE_ROWS, :] = (
            stage_ref[slot, :, :w.shape[1]].astype(BF16))


_HBM = pl.BlockSpec(memory_space=pl.ANY)


def _swiglu_half_step(x, g_ref, wg_ref, wu_ref, wd_ref):
    h = _rmsnorm_f32(x, g_ref[...]).astype(BF16)
    acc = jnp.zeros(x.shape, F32)
    for c in range(D_FF // FFN_TF):
        sl = slice(c * FFN_TF, (c + 1) * FFN_TF)
        gate = _dot(h, wg_ref[:, sl])
        up = _dot(h, wu_ref[:, sl])
        act = (gate * jax.nn.sigmoid(gate) * up).astype(BF16)
        acc = acc + _dot(act, wd_ref[sl, :])
    return x + 0.5 * acc


def _ffn_proj_body(x_ref, g_ref, wg_hbm, wu_hbm, wd_hbm, pg_ref, win_hbm, x1_ref, proj_ref,
                   wg_ref, wu_ref, wd_ref, win_ref, stage_ref, sem_ref):
    @pl.when(pl.program_id(0) == 0)
    def _load_weights():
        _load_weights_bf16(((wg_hbm, wg_ref), (wu_hbm, wu_ref), (wd_hbm, wd_ref),
                            (win_hbm, win_ref)), stage_ref, sem_ref)

    x1 = _swiglu_half_step(x_ref[...], g_ref, wg_ref, wu_ref, wd_ref)
    x1_ref[...] = x1
    h = _rmsnorm_f32(x1, pg_ref[...]).astype(BF16)
    for c in range(IN_WIDTH // PROJ_TN):
        lo = c * PROJ_TN
        y = _dot(h, win_ref[:, lo:lo + PROJ_TN])
        if lo == O_NQ or lo == O_DQ:
            y = y * (NA_HEAD_DIM ** -0.5)
        proj_ref[:, lo:lo + PROJ_TN] = y.astype(BF16)


def _ffn_proj(x, norm_g, wg, wu, wd, mix_g, w_in):
    n = x.shape[0]
    row = lambda i: (i, 0)
    return pl.pallas_call(
        _ffn_proj_body,
        grid=(n // FFN_TM,),
        in_specs=[
            pl.BlockSpec((FFN_TM, D_MODEL), row),
            _resident((1, D_MODEL)),
            _HBM, _HBM, _HBM,
            _resident((1, D_MODEL)),
            _HBM,
        ],
        out_specs=[pl.BlockSpec((FFN_TM, D_MODEL), row), pl.BlockSpec((FFN_TM, IN_WIDTH), row)],
        out_shape=[jax.ShapeDtypeStruct((n, D_MODEL), F32),
                   jax.ShapeDtypeStruct((n, IN_WIDTH), BF16)],
        scratch_shapes=[
            pltpu.VMEM((D_MODEL, D_FF), BF16), pltpu.VMEM((D_MODEL, D_FF), BF16),
            pltpu.VMEM((D_FF, D_MODEL), BF16), pltpu.VMEM((D_MODEL, IN_WIDTH), BF16),
            pltpu.VMEM((STAGE_SLOTS, STAGE_ROWS,IN_WIDTH), F32),
            pltpu.SemaphoreType.DMA((STAGE_SLOTS,)),
        ],
        compiler_params=pltpu.CompilerParams(
            dimension_semantics=("arbitrary",), vmem_limit_bytes=V7X_VMEM_LIMIT_BYTES),
        name="ffn_proj",
    )(x, norm_g, wg, wu, wd, mix_g, w_in)


def _memkv_body(m_ref, g_ref, w_hbm, k_ref, v_ref, w_ref, stage_ref, sem_ref):
    @pl.when(pl.program_id(0) == 0)
    def _load_weights():
        _load_weights_bf16(((w_hbm, w_ref),), stage_ref, sem_ref)

    h = _rmsnorm_f32(m_ref[...], g_ref[...]).astype(BF16)
    k_ref[...] = _dot(h, w_ref[:, :MEM_WIDTH]).astype(BF16)
    v_ref[...] = _dot(h, w_ref[:, MEM_WIDTH:]).astype(BF16)


def _memkv(mem2d, norm_g, w_kv, batch, m_tokens):
    row = lambda i: (i, 0)
    shp = jax.ShapeDtypeStruct((batch * m_tokens, MEM_WIDTH), BF16)
    return pl.pallas_call(
        _memkv_body,
        grid=(batch,),
        in_specs=[
            pl.BlockSpec((m_tokens, D_MODEL), row),
            _resident((1, D_MODEL)),
            _HBM,
        ],
        out_specs=[pl.BlockSpec((m_tokens, MEM_WIDTH), row)] * 2,
        out_shape=[shp, shp],
        scratch_shapes=[
            pltpu.VMEM((D_MODEL, 2 * MEM_WIDTH), BF16),
            pltpu.VMEM((STAGE_SLOTS, STAGE_ROWS,2 * MEM_WIDTH), F32),
            pltpu.SemaphoreType.DMA((STAGE_SLOTS,)),
        ],
        compiler_params=pltpu.CompilerParams(dimension_semantics=("arbitrary",)),
        name="memkv",
    )(mem2d, norm_g, w_kv)


RPB_ROWS = 2 * NA_WIN_ROWS - 1
RPB_COLS = 2 * NA_WIN_COLS - 1
NA_BAND_KEYS = NA_WIN_ROWS * GRID_W
NA_BATCH_CHUNK = 4
NA_ITEM_ROWS = 4


def _na_body(rpb_ref, q_ref, k_ref, v_ref, o_ref, tile_ref, band_ref,
             lg0_ref, lg1_ref, m0_ref, m1_ref, a0_ref, a1_ref, *, rows, nbatch):
    lg_ref, m_ref, a_ref = (lg0_ref, lg1_ref), (m0_ref, m1_ref), (a0_ref, a1_ref)
    hp = pl.program_id(0)
    b = pl.program_id(1)
    wr = min(NA_WIN_ROWS, rows)

    @pl.when(b == 0)
    def _build_bias():
        c = lax.broadcasted_iota(jnp.int32, (GRID_W, LANES), 0)
        kc = lax.broadcasted_iota(jnp.int32, (GRID_W, LANES), 1) & (GRID_W - 1)
        rel = kc - c
        cs = jnp.clip(c - NA_WIN_COLS // 2, 0, GRID_W - NA_WIN_COLS)
        valid = (kc >= cs) & (kc < cs + NA_WIN_COLS)

        def tile_step(t, carry):
            hl = t // RPB_ROWS
            dr = t - hl * RPB_ROWS
            base = ((hp * 2 + hl) * RPB_ROWS + dr) * RPB_COLS
            acc = jnp.full((GRID_W, LANES), NEG_BIG, F32)
            for dc in range(RPB_COLS):
                acc = jnp.where(rel == dc - (NA_WIN_COLS - 1), rpb_ref[base + dc], acc)
            tile_ref[hl, dr] = jnp.where(valid, acc, NEG_BIG)
            return carry

        lax.fori_loop(0, 2 * RPB_ROWS, tile_step, 0)
        low = lax.broadcasted_iota(jnp.int32, (GRID_W, LANES), 1) < GRID_W
        for hl in range(2):
            for dr0 in range(NA_WIN_ROWS):
                for j in range(wr // 2):
                    band_ref[dr0, hl * GRID_W:(hl + 1) * GRID_W, j * LANES:(j + 1) * LANES] = (
                        jnp.where(low, tile_ref[hl, dr0 + 2 * j], tile_ref[hl, dr0 + 2 * j + 1]))

    lane = lax.broadcasted_iota(jnp.int32, (GRID_W, LANES), 1)
    low = lane < NA_HEAD_DIM

    items_per_batch = rows // NA_ITEM_ROWS
    ipb_shift = items_per_batch.bit_length() - 1

    def locate(item, rr):
        if isinstance(item, int):
            bl, it = divmod(item, items_per_batch)
        else:
            bl, it = lax.shift_right_logical(item, ipb_shift), item & (items_per_batch - 1)
        return it * NA_ITEM_ROWS + rr, bl * (rows * GRID_W)

    def band_rows(r, base):
        if isinstance(r, int):
            rs = min(max(r - wr // 2, 0), rows - wr)
        else:
            rs = jnp.clip(r - wr // 2, 0, rows - wr)
        return rs, pl.ds(_aligned(base + rs * GRID_W, GRID_W), wr * GRID_W)

    def query_rows(r, base):
        return pl.ds(_aligned(base + r * GRID_W, GRID_W), GRID_W)

    def logits_stage(item, slot):
        for rr in range(NA_ITEM_ROWS):
            r, base = locate(item, rr)
            rs, keys = band_rows(r, base)
            q2 = q_ref[query_rows(r, base), :]
            zero = jnp.zeros_like(q2)
            q_st = jnp.concatenate([jnp.where(low, q2, zero), jnp.where(low, zero, q2)], axis=0)
            lg = _dot_nt(q_st, k_ref[keys, :]) + band_ref[rs - r + (NA_WIN_ROWS - 1)]
            lg_ref[slot][rr] = lg
            lane_max = lg[:, :LANES]
            for c in range(1, NA_BAND_KEYS // LANES):
                lane_max = jnp.maximum(lane_max, lg[:, c * LANES:(c + 1) * LANES])
            m_ref[slot][rr] = lane_max

    def value_stage(item, slot):
        for rr in range(NA_ITEM_ROWS):
            r, base = locate(item, rr)
            _, keys = band_rows(r, base)
            row_max = jnp.max(m_ref[slot][rr], axis=-1, keepdims=True)
            e = jnp.exp(lg_ref[slot][rr] - row_max)
            lane_sum = e[:, :LANES]
            for c in range(1, NA_BAND_KEYS // LANES):
                lane_sum = lane_sum + e[:, c * LANES:(c + 1) * LANES]
            a_ref[slot][rr, :, :LANES] = _dot(e.astype(BF16), v_ref[keys, :])
            a_ref[slot][rr, :, LANES:] = lane_sum

    def finish_stage(item, slot):
        for rr in range(NA_ITEM_ROWS):
            r, base = locate(item, rr)
            acc = a_ref[slot][rr]
            o_st = acc[:, :LANES] * (1.0 / jnp.sum(acc[:, LANES:], axis=-1, keepdims=True))
            o = jnp.where(low, o_st[:GRID_W], o_st[GRID_W:])
            o_ref[query_rows(r, base), :] = o.astype(BF16)

    _software_pipeline(nbatch * items_per_batch, (logits_stage, value_stage, finish_stage),
                       order=(2, 0, 1))


def _na(proj, rpb_flat, batch, seq):
    rows = seq // GRID_W
    items_per_batch = rows // NA_ITEM_ROWS
    nbatch = math.gcd(batch, NA_BATCH_CHUNK)
    assert rows >= NA_WIN_ROWS and NA_WIN_ROWS % 2 == 0 and rows % NA_ITEM_ROWS == 0
    assert items_per_batch & (items_per_batch - 1) == 0
    pairs = NA_HEADS // 2
    blk = lambda off: pl.BlockSpec((nbatch * seq, LANES), lambda hp, b, off=off: (b, off + hp))
    return pl.pallas_call(
        functools.partial(_na_body, rows=rows, nbatch=nbatch),
        grid=(pairs, batch // nbatch),
        in_specs=[
            pl.BlockSpec(memory_space=pltpu.SMEM),
            blk(O_NQ // LANES), blk(O_NK // LANES), blk(O_NV // LANES),
        ],
        out_specs=pl.BlockSpec((nbatch * seq, LANES), lambda hp, b: (b, hp)),
        out_shape=jax.ShapeDtypeStruct((batch * seq, NA_WIDTH), BF16),
        scratch_shapes=[
            pltpu.VMEM((2, RPB_ROWS, GRID_W, LANES), F32),
            pltpu.VMEM((NA_WIN_ROWS, 2 * GRID_W, NA_BAND_KEYS), F32),
            pltpu.VMEM((NA_ITEM_ROWS, 2 * GRID_W, NA_BAND_KEYS), F32),
            pltpu.VMEM((NA_ITEM_ROWS, 2 * GRID_W, NA_BAND_KEYS), F32),
            pltpu.VMEM((NA_ITEM_ROWS, 2 * GRID_W, LANES), F32),
            pltpu.VMEM((NA_ITEM_ROWS, 2 * GRID_W, LANES), F32),
            pltpu.VMEM((NA_ITEM_ROWS, 2 * GRID_W, 2 * LANES), F32),
            pltpu.VMEM((NA_ITEM_ROWS, 2 * GRID_W, 2 * LANES), F32),
        ],
        compiler_params=pltpu.CompilerParams(
            dimension_semantics=("arbitrary", "arbitrary"), vmem_limit_bytes=V7X_VMEM_LIMIT_BYTES),
        name="na_attn",
    )(rpb_flat, proj, proj, proj)


def _diff_body(slope_ref, lq1_ref, lk1_ref, lq2_ref, lk2_ref, sub_ref, q_ref, k_ref, v_ref, o_ref,
               dist_ref, v1_ref, s0_ref, s1_ref, m0_ref, m1_ref, a0_ref, a1_ref,
               *, lam_init, seq, nbatch):
    s_ref, m_ref, a_ref = (s0_ref, s1_ref), (m0_ref, m1_ref), (a0_ref, a1_ref)
    h = pl.program_id(0)
    b = pl.program_id(1)
    nq = seq // DIFF_TQ

    @pl.when(b == 0)
    def _build_dist():
        slope = slope_ref[h]
        rel = (lax.broadcasted_iota(jnp.int32, (DIFF_TQ, DIFF_TQ), 0)
               - lax.broadcasted_iota(jnp.int32, (DIFF_TQ, DIFF_TQ), 1))
        for d in range(2 * nq - 1):
            off = (d - (nq - 1)) * DIFF_TQ
            dist_ref[d] = slope * jnp.abs((rel + off).astype(F32))

    lam = (jnp.exp(jnp.sum(lq1_ref[...] * lk1_ref[...], axis=-1, keepdims=True))
           - jnp.exp(jnp.sum(lq2_ref[...] * lk2_ref[...], axis=-1, keepdims=True))
           + lam_init)
    lane = lax.broadcasted_iota(jnp.int32, (DIFF_TQ, LANES), 1)
    first_map = lane < DIFF_QK_DIM
    nq_shift = nq.bit_length() - 1

    def q_rows(j):
        return pl.ds(_aligned(j * DIFF_TQ, DIFF_TQ), DIFF_TQ)

    def batch_rows(j):
        bl = j // nq if isinstance(j, int) else lax.shift_right_logical(j, nq_shift)
        return pl.ds(_aligned(bl * seq, seq), seq)

    def logits_stage(j, slot):
        i = j % nq if isinstance(j, int) else j & (nq - 1)
        q = q_ref[q_rows(j), :]
        zero = jnp.zeros_like(q)
        k = k_ref[batch_rows(j), :]
        dist = jnp.concatenate([dist_ref[i - c + (nq - 1)] for c in range(nq)], axis=1)
        for m, qm in enumerate((jnp.where(first_map, q, zero), jnp.where(first_map, zero, q))):
            s = _dot_nt(qm, k) - dist
            s_ref[slot][m] = s
            lane_max = s[:, :LANES]
            for c in range(1, seq // LANES):
                lane_max = jnp.maximum(lane_max, s[:, c * LANES:(c + 1) * LANES])
            m_ref[slot][m] = lane_max

    def value_stage(j, slot):
        e = jnp.concatenate(
            [jnp.exp(s_ref[slot][m] - jnp.max(m_ref[slot][m], axis=-1, keepdims=True)).astype(BF16)
             for m in range(2)], axis=0)
        a_ref[slot][...] = _dot(e, v1_ref[batch_rows(j), :])

    def finish_stage(j, slot):
        acc = a_ref[slot][...]
        num1, den1 = acc[:DIFF_TQ, :DIFF_V_DIM], acc[:DIFF_TQ, DIFF_V_DIM:DIFF_V_DIM + 1]
        num2, den2 = acc[DIFF_TQ:, :DIFF_V_DIM], acc[DIFF_TQ:, DIFF_V_DIM:DIFF_V_DIM + 1]
        o = num1 * (1.0 / den1) - (lam * (1.0 / den2)) * num2
        o = _rmsnorm_f32(o, sub_ref[...]) * (1.0 - lam_init)
        o_ref[q_rows(j), :] = o.astype(BF16)

    v = v_ref[...]
    v1_ref[:, :DIFF_V_DIM] = v
    v1_ref[:, DIFF_V_DIM:] = jnp.where(
        lax.broadcasted_iota(jnp.int32, v.shape, 1) == 0, 1.0, 0.0).astype(BF16)
    _software_pipeline(nbatch * nq, (logits_stage, value_stage, finish_stage), order=(2, 0, 1))


def _diff(proj, slopes, lq1, lk1, lq2, lk2, subln, batch, seq, lam_init):
    nq = seq // DIFF_TQ
    nbatch = math.gcd(batch, DIFF_BATCH_CHUNK)
    assert nq & (nq - 1) == 0
    vec = lambda w: pl.BlockSpec((1, w), lambda h, b: (0, 0))
    blk = lambda off: pl.BlockSpec((nbatch * seq, LANES), lambda h, b, off=off: (b, off + h))
    return pl.pallas_call(
        functools.partial(_diff_body, lam_init=lam_init, seq=seq, nbatch=nbatch),
        grid=(DIFF_HEADS, batch // nbatch),
        in_specs=[
            pl.BlockSpec(memory_space=pltpu.SMEM),
            vec(DIFF_QK_DIM), vec(DIFF_QK_DIM), vec(DIFF_QK_DIM), vec(DIFF_QK_DIM), vec(DIFF_V_DIM),
            blk(O_DQ // LANES), blk(O_DK // LANES), blk(O_DV // LANES),
        ],
        out_specs=pl.BlockSpec((nbatch * seq, LANES), lambda h, b: (b, h)),
        out_shape=jax.ShapeDtypeStruct((batch * seq, DIFF_V_WIDTH), BF16),
        scratch_shapes=[
            pltpu.VMEM((2 * nq - 1, DIFF_TQ, DIFF_TQ), F32),
            pltpu.VMEM((nbatch * seq, 2 * LANES), BF16),
            pltpu.VMEM((2, DIFF_TQ, seq), F32), pltpu.VMEM((2, DIFF_TQ, seq), F32),
            pltpu.VMEM((2, DIFF_TQ, LANES), F32), pltpu.VMEM((2, DIFF_TQ, LANES), F32),
            pltpu.VMEM((2 * DIFF_TQ, 2 * LANES), F32), pltpu.VMEM((2 * DIFF_TQ, 2 * LANES), F32),
        ],
        compiler_params=pltpu.CompilerParams(
            dimension_semantics=("arbitrary", "arbitrary"),
            vmem_limit_bytes=V7X_VMEM_LIMIT_BYTES),
        name="diff_attn",
    )(slopes, lq1, lk1, lq2, lk2, subln, proj, proj, proj)


def _merge_ffn_body(x_ref, mq_ref, ona_ref, odf_ref, mk_ref, mv_ref, g_ref, wgate_hbm, bgate_ref,
                    wna_hbm, wdf_hbm, wmem_hbm, wout_hbm, fg_ref, wg_hbm, wu_hbm, wd_hbm, ng_ref,
                    o_ref, wgate_ref, wna_ref, wdf_ref, wmem_ref, wout_ref, wg_ref, wu_ref, wd_ref,
                    stage_ref, sem_ref, *, final_norm):
    @pl.when(pl.program_id(0) == 0)
    def _load_weights():
        _load_weights_bf16(
            ((wgate_hbm, wgate_ref), (wna_hbm, wna_ref), (wdf_hbm, wdf_ref), (wmem_hbm, wmem_ref),
             (wout_hbm, wout_ref), (wg_hbm, wg_ref), (wu_hbm, wu_ref), (wd_hbm, wd_ref)),
            stage_ref, sem_ref)

    x = x_ref[...]
    h = _rmsnorm_f32(x, g_ref[...]).astype(BF16)
    heads = []
    for hh in range(MEM_HEADS):
        sl = slice(hh * MEM_HEAD_DIM, (hh + 1) * MEM_HEAD_DIM)
        logits = _dot_nt(mq_ref[:, sl], mk_ref[:, sl]) * (MEM_HEAD_DIM ** -0.5)
        p = _softmax_rows(logits).astype(BF16)
        heads.append(_dot(p, mv_ref[:, sl]))
    o_mem = jnp.concatenate(heads, axis=-1).astype(BF16)
    branches = (
        _dot(ona_ref[...], wna_ref[...]),
        _dot(odf_ref[...], wdf_ref[...]),
        _dot(o_mem, wmem_ref[...]),
    )
    merged = jnp.zeros(x.shape, F32)
    for i, y in enumerate(branches):
        sl = slice(i * D_MODEL, (i + 1) * D_MODEL)
        gate = jax.nn.sigmoid(_dot(h, wgate_ref[:, sl]) + bgate_ref[:, sl])
        merged = merged + gate * y
    x2 = x + _dot(merged.astype(BF16), wout_ref[...])
    y = _swiglu_half_step(x2, fg_ref, wg_ref, wu_ref, wd_ref)
    if final_norm:
        y = _rmsnorm_f32(y, ng_ref[...])
    o_ref[...] = y


def _merge_ffn(x1, proj, o_na, o_diff, mk, mv, norm_g, w_gate, b_gate, w_na, w_df, w_mem, w_out,
               ffn_g, wg, wu, wd, final_g, seq, m_tokens, *, final_norm):
    n = x1.shape[0]
    per_b = seq // MERGE_TM
    row = lambda i: (i, 0)
    return pl.pallas_call(
        functools.partial(_merge_ffn_body, final_norm=final_norm),
        grid=(n // MERGE_TM,),
        in_specs=[
            pl.BlockSpec((MERGE_TM, D_MODEL), row),
            pl.BlockSpec((MERGE_TM, MEM_WIDTH), lambda i: (i, O_MQ // MEM_WIDTH)),
            pl.BlockSpec((MERGE_TM, NA_WIDTH), row),
            pl.BlockSpec((MERGE_TM, DIFF_V_WIDTH), row),
            pl.BlockSpec((m_tokens, MEM_WIDTH), lambda i: (i // per_b, 0)),
            pl.BlockSpec((m_tokens, MEM_WIDTH), lambda i: (i // per_b, 0)),
            _resident((1, D_MODEL)),
            _HBM,
            _resident((1, 3 * D_MODEL)),
            _HBM, _HBM, _HBM, _HBM,
            _resident((1, D_MODEL)),
            _HBM, _HBM, _HBM,
            _resident((1, D_MODEL)),
        ],
        out_specs=pl.BlockSpec((MERGE_TM, D_MODEL), row),
        out_shape=jax.ShapeDtypeStruct((n, D_MODEL), F32),
        scratch_shapes=[
            pltpu.VMEM((D_MODEL, 3 * D_MODEL), BF16),
            pltpu.VMEM((NA_WIDTH, D_MODEL), BF16), pltpu.VMEM((DIFF_V_WIDTH, D_MODEL), BF16),
            pltpu.VMEM((MEM_WIDTH, D_MODEL), BF16), pltpu.VMEM((D_MODEL, D_MODEL), BF16),
            pltpu.VMEM((D_MODEL, D_FF), BF16), pltpu.VMEM((D_MODEL, D_FF), BF16),
            pltpu.VMEM((D_FF, D_MODEL), BF16),
            pltpu.VMEM((STAGE_SLOTS, STAGE_ROWS,3 * D_MODEL), F32),
            pltpu.SemaphoreType.DMA((STAGE_SLOTS,)),
        ],
        compiler_params=pltpu.CompilerParams(
            dimension_semantics=("arbitrary",), vmem_limit_bytes=V7X_VMEM_LIMIT_BYTES),
        name="merge_ffn",
    )(x1, proj, o_na, o_diff, mk, mv, norm_g, w_gate, b_gate, w_na, w_df, w_mem, w_out,
      ffn_g, wg, wu, wd, final_g)


def kernel(x, mem, ffn1_norm, ffn1_w_gate, ffn1_w_up, ffn1_w_down, mix_norm, w_in, na_rpb,
           diff_lambda_q1, diff_lambda_k1, diff_lambda_q2, diff_lambda_k2, diff_subln,
           mem_norm, w_mem_kv, w_gate, b_gate, w_br_na, w_br_diff, w_br_mem, w_out,
           ffn2_norm, ffn2_w_gate, ffn2_w_up, ffn2_w_down, final_norm):
    batch, seq, d_model = x.shape
    m_tokens = mem.shape[1]
    depth = ffn1_norm.shape[0]
    assert d_model == D_MODEL and seq % GRID_W == 0
    assert seq % DIFF_TQ == 0 and seq % MERGE_TM == 0 and (batch * seq) % FFN_TM == 0
    slopes = jnp.asarray([2.0 ** (-8.0 * (i + 1) / DIFF_HEADS) for i in range(DIFF_HEADS)], F32)
    w32 = lambda w: w.astype(F32)
    vec = lambda v: v.reshape(1, -1).astype(F32)

    xt = x.reshape(batch * seq, d_model)
    mem2d = mem.reshape(batch * m_tokens, d_model)
    for l in range(depth):
        lam_init = 0.8 - 0.6 * math.exp(-0.3 * l)
        xt, proj = _ffn_proj(xt, vec(ffn1_norm[l]), w32(ffn1_w_gate[l]), w32(ffn1_w_up[l]),
                             w32(ffn1_w_down[l]), vec(mix_norm[l]), w32(w_in[l]))
        mk, mv = _memkv(mem2d, vec(mem_norm[l]), w32(w_mem_kv[l]), batch, m_tokens)
        o_na = _na(proj, na_rpb[l].reshape(-1).astype(F32), batch, seq)
        o_diff = _diff(proj, slopes, vec(diff_lambda_q1[l]), vec(diff_lambda_k1[l]),
                       vec(diff_lambda_q2[l]), vec(diff_lambda_k2[l]), vec(diff_subln[l]),
                       batch, seq, lam_init)
        xt = _merge_ffn(xt, proj, o_na, o_diff, mk, mv, vec(mix_norm[l]), w32(w_gate[l]),
                        vec(b_gate[l]), w32(w_br_na[l]), w32(w_br_diff[l]), w32(w_br_mem[l]),
                        w32(w_out[l]), vec(ffn2_norm[l]), w32(ffn2_w_gate[l]), w32(ffn2_w_up[l]),
                        w32(ffn2_w_down[l]), vec(final_norm), seq, m_tokens,
                        final_norm=(l == depth - 1))
    return xt.reshape(batch, seq, d_model)
```

```python
import functools
import math

import jax
import jax.numpy as jnp
from jax import lax
from jax.experimental import pallas as pl
from jax.experimental.pallas import tpu as pltpu

F32 = jnp.float32
BF16 = jnp.bfloat16

D_MODEL = 1024
GRID_W = 64
NA_HEADS = 8
NA_HEAD_DIM = 64
NA_WIN_ROWS = 8
NA_WIN_COLS = 16
DIFF_HEADS = 4
DIFF_QK_DIM = 64
DIFF_V_DIM = 128
MEM_HEADS = 4
MEM_HEAD_DIM = 128
D_FF = 2816
NORM_EPS = 1e-6
NA_WIDTH = NA_HEADS * NA_HEAD_DIM
DIFF_QK_WIDTH = DIFF_HEADS * 2 * DIFF_QK_DIM
DIFF_V_WIDTH = DIFF_HEADS * DIFF_V_DIM
MEM_WIDTH = MEM_HEADS * MEM_HEAD_DIM
IN_WIDTH = 3 * NA_WIDTH + 2 * DIFF_QK_WIDTH + DIFF_V_WIDTH + MEM_WIDTH
O_NQ = 0
O_NK = O_NQ + NA_WIDTH
O_NV = O_NK + NA_WIDTH
O_DQ = O_NV + NA_WIDTH
O_DK = O_DQ + DIFF_QK_WIDTH
O_DV = O_DK + DIFF_QK_WIDTH
O_MQ = O_DV + DIFF_V_WIDTH

LANES = 128
V7X_VMEM_LIMIT_BYTES = 56 * 1024 * 1024

FFN_TM = 512
STAGE_ROWS = 128
STAGE_SLOTS = 4
FFN_TF = 256
PROJ_TN = 512
DIFF_TQ = 512
DIFF_BATCH_CHUNK = 4
MERGE_TM = 512
NEG_BIG = -1e30


def _rmsnorm_f32(x, g):
    ms = jnp.mean(x * x, axis=-1, keepdims=True)
    return (x * lax.rsqrt(ms + NORM_EPS)) * g


def _softmax_rows(logits):
    m = jnp.max(logits, axis=-1, keepdims=True)
    e = jnp.exp(logits - m)
    s = jnp.sum(e, axis=-1, keepdims=True)
    return e * (1.0 / s)


def _dot_nt(a, b):
    return lax.dot_general(a, b, (((1,), (1,)), ((), ())), preferred_element_type=F32)


def _dot(a, b):
    return jnp.dot(a, b, preferred_element_type=F32)


def _resident(shape):
    nd = len(shape)
    return pl.BlockSpec(shape, lambda *_: (0,) * nd, pipeline_mode=pl.Buffered(1))


def _aligned(idx, multiple):
    return idx if isinstance(idx, int) else pl.multiple_of(idx, multiple)


def _software_pipeline(n_items, stages, order):
    depth = len(stages)
    assert n_items >= depth and sorted(order) == list(range(depth))

    def trip(t, parity, valid):
        for k in order:
            if valid(k):
                stages[k](t - k, (parity - k) % 2)

    for t in range(depth - 1):
        trip(t, t % 2, lambda k, t=t: k <= t)
    start = depth - 1
    if (n_items - start) % 2:
        trip(start, start % 2, lambda k: True)
        start += 1

    def body(u, carry):
        t = start + 2 * u
        trip(t, start % 2, lambda k: True)
        trip(t + 1, (start + 1) % 2, lambda k: True)
        return carry

    lax.fori_loop(0, (n_items - start) // 2, body, 0)
    for t in range(n_items, n_items + depth - 1):
        trip(t, t % 2, lambda k, t=t: t - k < n_items)


def _stage_copy(w_hbm, stage_ref, sem_ref, chunk, slot):
    width = w_hbm.shape[1]
    return pltpu.make_async_copy(
        w_hbm.at[pl.ds(chunk * STAGE_ROWS, STAGE_ROWS), :],
        stage_ref.at[slot, :, :width],
        sem_ref.at[slot])


def _load_weights_bf16(pairs, stage_ref, sem_ref):
    jobs = [(w, dst, c) for w, dst in pairs for c in range(w.shape[0] // STAGE_ROWS)]
    ahead = STAGE_SLOTS - 1
    def start(n):
        w, _, c = jobs[n]
        _stage_copy(w, stage_ref, sem_ref, c, n % STAGE_SLOTS).start(priority=n % 2)

    for n in range(min(ahead, len(jobs))):
        start(n)
    for n, (w, dst, c) in enumerate(jobs):
        slot = n % STAGE_SLOTS
        if n + ahead < len(jobs):
            start(n + ahead)
        _stage_copy(w, stage_ref, sem_ref, c, slot).wait()
        dst[c * STAGE_ROWS:(c + 1) * STAGE_ROWS, :] = (
            stage_ref[slot, :, :w.shape[1]].astype(BF16))


_HBM = pl.BlockSpec(memory_space=pl.ANY)


def _swiglu_half_step(x, g_ref, wg_ref, wu_ref, wd_ref):
    h = _rmsnorm_f32(x, g_ref[...]).astype(BF16)
    acc = jnp.zeros(x.shape, F32)
    for c in range(D_FF // FFN_TF):
        sl = slice(c * FFN_TF, (c + 1) * FFN_TF)
        gate = _dot(h, wg_ref[:, sl])
        up = _dot(h, wu_ref[:, sl])
        act = (gate * jax.nn.sigmoid(gate) * up).astype(BF16)
        acc = acc + _dot(act, wd_ref[sl, :])
    return x + 0.5 * acc


def _ffn_proj_body(x_ref, g_ref, wg_hbm, wu_hbm, wd_hbm, pg_ref, win_hbm, x1_ref, proj_ref,
                   wg_ref, wu_ref, wd_ref, win_ref, stage_ref, sem_ref):
    @pl.when(pl.program_id(0) == 0)
    def _load_weights():
        _load_weights_bf16(((wg_hbm, wg_ref), (wu_hbm, wu_ref), (wd_hbm, wd_ref),
                            (win_hbm, win_ref)), stage_ref, sem_ref)

    x1 = _swiglu_half_step(x_ref[...], g_ref, wg_ref, wu_ref, wd_ref)
    x1_ref[...] = x1
    h = _rmsnorm_f32(x1, pg_ref[...]).astype(BF16)
    for c in range(IN_WIDTH // PROJ_TN):
        lo = c * PROJ_TN
        y = _dot(h, win_ref[:, lo:lo + PROJ_TN])
        if lo == O_NQ or lo == O_DQ:
            y = y * (NA_HEAD_DIM ** -0.5)
        proj_ref[:, lo:lo + PROJ_TN] = y.astype(BF16)


def _ffn_proj(x, norm_g, wg, wu, wd, mix_g, w_in):
    n = x.shape[0]
    row = lambda i: (i, 0)
    return pl.pallas_call(
        _ffn_proj_body,
        grid=(n // FFN_TM,),
        in_specs=[
            pl.BlockSpec((FFN_TM, D_MODEL), row),
            _resident((1, D_MODEL)),
            _HBM, _HBM, _HBM,
            _resident((1, D_MODEL)),
            _HBM,
        ],
        out_specs=[pl.BlockSpec((FFN_TM, D_MODEL), row), pl.BlockSpec((FFN_TM, IN_WIDTH), row)],
        out_shape=[jax.ShapeDtypeStruct((n, D_MODEL), F32),
                   jax.ShapeDtypeStruct((n, IN_WIDTH), BF16)],
        scratch_shapes=[
            pltpu.VMEM((D_MODEL, D_FF), BF16), pltpu.VMEM((D_MODEL, D_FF), BF16),
            pltpu.VMEM((D_FF, D_MODEL), BF16), pltpu.VMEM((D_MODEL, IN_WIDTH), BF16),
            pltpu.VMEM((STAGE_SLOTS, STAGE_ROWS,IN_WIDTH), F32),
            pltpu.SemaphoreType.DMA((STAGE_SLOTS,)),
        ],
        compiler_params=pltpu.CompilerParams(
            dimension_semantics=("arbitrary",), vmem_limit_bytes=V7X_VMEM_LIMIT_BYTES),
        name="ffn_proj",
    )(x, norm_g, wg, wu, wd, mix_g, w_in)


def _memkv_body(m_ref, g_ref, w_hbm, k_ref, v_ref, w_ref, stage_ref, sem_ref):
    @pl.when(pl.program_id(0) == 0)
    def _load_weights():
        _load_weights_bf16(((w_hbm, w_ref),), stage_ref, sem_ref)

    h = _rmsnorm_f32(m_ref[...], g_ref[...]).astype(BF16)
    k_ref[...] = _dot(h, w_ref[:, :MEM_WIDTH]).astype(BF16)
    v_ref[...] = _dot(h, w_ref[:, MEM_WIDTH:]).astype(BF16)


def _memkv(mem2d, norm_g, w_kv, batch, m_tokens):
    row = lambda i: (i, 0)
    shp = jax.ShapeDtypeStruct((batch * m_tokens, MEM_WIDTH), BF16)
    return pl.pallas_call(
        _memkv_body,
        grid=(batch,),
        in_specs=[
            pl.BlockSpec((m_tokens, D_MODEL), row),
            _resident((1, D_MODEL)),
            _HBM,
        ],
        out_specs=[pl.BlockSpec((m_tokens, MEM_WIDTH), row)] * 2,
        out_shape=[shp, shp],
        scratch_shapes=[
            pltpu.VMEM((D_MODEL, 2 * MEM_WIDTH), BF16),
            pltpu.VMEM((STAGE_SLOTS, STAGE_ROWS,2 * MEM_WIDTH), F32),
            pltpu.SemaphoreType.DMA((STAGE_SLOTS,)),
        ],
        compiler_params=pltpu.CompilerParams(dimension_semantics=("arbitrary",)),
        name="memkv",
    )(mem2d, norm_g, w_kv)


RPB_ROWS = 2 * NA_WIN_ROWS - 1
RPB_COLS = 2 * NA_WIN_COLS - 1
NA_BAND_KEYS = NA_WIN_ROWS * GRID_W
NA_BATCH_CHUNK = 4
NA_ITEM_ROWS = 4


def _na_body(rpb_ref, q_ref, k_ref, v_ref, o_ref, tile_ref, band_ref,
             lg0_ref, lg1_ref, m0_ref, m1_ref, a0_ref, a1_ref, *, rows, nbatch):
    lg_ref, m_ref, a_ref = (lg0_ref, lg1_ref), (m0_ref, m1_ref), (a0_ref, a1_ref)
    hp = pl.program_id(0)
    b = pl.program_id(1)
    wr = min(NA_WIN_ROWS, rows)

    @pl.when(b == 0)
    def _build_bias():
        c = lax.broadcasted_iota(jnp.int32, (GRID_W, LANES), 0)
        kc = lax.broadcasted_iota(jnp.int32, (GRID_W, LANES), 1) & (GRID_W - 1)
        rel = kc - c
        cs = jnp.clip(c - NA_WIN_COLS // 2, 0, GRID_W - NA_WIN_COLS)
        valid = (kc >= cs) & (kc < cs + NA_WIN_COLS)

        def tile_step(t, carry):
            hl = t // RPB_ROWS
            dr = t - hl * RPB_ROWS
            base = ((hp * 2 + hl) * RPB_ROWS + dr) * RPB_COLS
            acc = jnp.full((GRID_W, LANES), NEG_BIG, F32)
            for dc in range(RPB_COLS):
                acc = jnp.where(rel == dc - (NA_WIN_COLS - 1), rpb_ref[base + dc], acc)
            tile_ref[hl, dr] = jnp.where(valid, acc, NEG_BIG)
            return carry

        lax.fori_loop(0, 2 * RPB_ROWS, tile_step, 0)
        low = lax.broadcasted_iota(jnp.int32, (GRID_W, LANES), 1) < GRID_W
        for hl in range(2):
            for dr0 in range(NA_WIN_ROWS):
                for j in range(wr // 2):
                    band_ref[dr0, hl * GRID_W:(hl + 1) * GRID_W, j * LANES:(j + 1) * LANES] = (
                        jnp.where(low, tile_ref[hl, dr0 + 2 * j], tile_ref[hl, dr0 + 2 * j + 1]))

    lane = lax.broadcasted_iota(jnp.int32, (GRID_W, LANES), 1)
    low = lane < NA_HEAD_DIM

    items_per_batch = rows // NA_ITEM_ROWS
    ipb_shift = items_per_batch.bit_length() - 1

    def locate(item, rr):
        if isinstance(item, int):
            bl, it = divmod(item, items_per_batch)
        else:
            bl, it = lax.shift_right_logical(item, ipb_shift), item & (items_per_batch - 1)
        return it * NA_ITEM_ROWS + rr, bl * (rows * GRID_W)

    def band_rows(r, base):
        if isinstance(r, int):
            rs = min(max(r - wr // 2, 0), rows - wr)
        else:
            rs = jnp.clip(r - wr // 2, 0, rows - wr)
        return rs, pl.ds(_aligned(base + rs * GRID_W, GRID_W), wr * GRID_W)

    def query_rows(r, base):
        return pl.ds(_aligned(base + r * GRID_W, GRID_W), GRID_W)

    def logits_stage(item, slot):
        for rr in range(NA_ITEM_ROWS):
            r, base = locate(item, rr)
            rs, keys = band_rows(r, base)
            q2 = q_ref[query_rows(r, base), :]
            zero = jnp.zeros_like(q2)
            q_st = jnp.concatenate([jnp.where(low, q2, zero), jnp.where(low, zero, q2)], axis=0)
            lg = _dot_nt(q_st, k_ref[keys, :]) + band_ref[rs - r + (NA_WIN_ROWS - 1)]
            lg_ref[slot][rr] = lg
            lane_max = lg[:, :LANES]
            for c in range(1, NA_BAND_KEYS // LANES):
                lane_max = jnp.maximum(lane_max, lg[:, c * LANES:(c + 1) * LANES])
            m_ref[slot][rr] = lane_max

    def value_stage(item, slot):
        for rr in range(NA_ITEM_ROWS):
            r, base = locate(item, rr)
            _, keys = band_rows(r, base)
            row_max = jnp.max(m_ref[slot][rr], axis=-1, keepdims=True)
            e = jnp.exp(lg_ref[slot][rr] - row_max)
            lane_sum = e[:, :LANES]
            for c in range(1, NA_BAND_KEYS // LANES):
                lane_sum = lane_sum + e[:, c * LANES:(c + 1) * LANES]
            a_ref[slot][rr, :, :LANES] = _dot(e.astype(BF16), v_ref[keys, :])
            a_ref[slot][rr, :, LANES:] = lane_sum

    def finish_stage(item, slot):
        for rr in range(NA_ITEM_ROWS):
            r, base = locate(item, rr)
            acc = a_ref[slot][rr]
            o_st = acc[:, :LANES] * (1.0 / jnp.sum(acc[:, LANES:], axis=-1, keepdims=True))
            o = jnp.where(low, o_st[:GRID_W], o_st[GRID_W:])
            o_ref[query_rows(r, base), :] = o.astype(BF16)

    _software_pipeline(nbatch * items_per_batch, (logits_stage, value_stage, finish_stage),
                       order=(2, 0, 1))


def _na(proj, rpb_flat, batch, seq):
    rows = seq // GRID_W
    items_per_batch = rows // NA_ITEM_ROWS
    nbatch = math.gcd(batch, NA_BATCH_CHUNK)
    assert rows >= NA_WIN_ROWS and NA_WIN_ROWS % 2 == 0 and rows % NA_ITEM_ROWS == 0
    assert items_per_batch & (items_per_batch - 1) == 0
    pairs = NA_HEADS // 2
    blk = lambda off: pl.BlockSpec((nbatch * seq, LANES), lambda hp, b, off=off: (b, off + hp))
    return pl.pallas_call(
        functools.partial(_na_body, rows=rows, nbatch=nbatch),
        grid=(pairs, batch // nbatch),
        in_specs=[
            pl.BlockSpec(memory_space=pltpu.SMEM),
            blk(O_NQ // LANES), blk(O_NK // LANES), blk(O_NV // LANES),
        ],
        out_specs=pl.BlockSpec((nbatch * seq, LANES), lambda hp, b: (b, hp)),
        out_shape=jax.ShapeDtypeStruct((batch * seq, NA_WIDTH), BF16),
        scratch_shapes=[
            pltpu.VMEM((2, RPB_ROWS, GRID_W, LANES), F32),
            pltpu.VMEM((NA_WIN_ROWS, 2 * GRID_W, NA_BAND_KEYS), F32),
            pltpu.VMEM((NA_ITEM_ROWS, 2 * GRID_W, NA_BAND_KEYS), F32),
            pltpu.VMEM((NA_ITEM_ROWS, 2 * GRID_W, NA_BAND_KEYS), F32),
            pltpu.VMEM((NA_ITEM_ROWS, 2 * GRID_W, LANES), F32),
            pltpu.VMEM((NA_ITEM_ROWS, 2 * GRID_W, LANES), F32),
            pltpu.VMEM((NA_ITEM_ROWS, 2 * GRID_W, 2 * LANES), F32),
            pltpu.VMEM((NA_ITEM_ROWS, 2 * GRID_W, 2 * LANES), F32),
        ],
        compiler_params=pltpu.CompilerParams(
            dimension_semantics=("arbitrary", "arbitrary"), vmem_limit_bytes=V7X_VMEM_LIMIT_BYTES),
        name="na_attn",
    )(rpb_flat, proj, proj, proj)


def _diff_body(slope_ref, lq1_ref, lk1_ref, lq2_ref, lk2_ref, sub_ref, q_ref, k_ref, v_ref, o_ref,
               dist_ref, v1_ref, s0_ref, s1_ref, m0_ref, m1_ref, a0_ref, a1_ref,
               *, lam_init, seq, nbatch):
    s_ref, m_ref, a_ref = (s0_ref, s1_ref), (m0_ref, m1_ref), (a0_ref, a1_ref)
    h = pl.program_id(0)
    b = pl.program_id(1)
    nq = seq // DIFF_TQ

    @pl.when(b == 0)
    def _build_dist():
        slope = slope_ref[h]
        rel = (lax.broadcasted_iota(jnp.int32, (DIFF_TQ, DIFF_TQ), 0)
               - lax.broadcasted_iota(jnp.int32, (DIFF_TQ, DIFF_TQ), 1))
        for d in range(2 * nq - 1):
            off = (d - (nq - 1)) * DIFF_TQ
            dist_ref[d] = slope * jnp.abs((rel + off).astype(F32))

    lam = (jnp.exp(jnp.sum(lq1_ref[...] * lk1_ref[...], axis=-1, keepdims=True))
           - jnp.exp(jnp.sum(lq2_ref[...] * lk2_ref[...], axis=-1, keepdims=True))
           + lam_init)
    lane = lax.broadcasted_iota(jnp.int32, (DIFF_TQ, LANES), 1)
    first_map = lane < DIFF_QK_DIM
    nq_shift = nq.bit_length() - 1

    def q_rows(j):
        return pl.ds(_aligned(j * DIFF_TQ, DIFF_TQ), DIFF_TQ)

    def batch_rows(j):
        bl = j // nq if isinstance(j, int) else lax.shift_right_logical(j, nq_shift)
        return pl.ds(_aligned(bl * seq, seq), seq)

    def logits_stage(j, slot):
        i = j % nq if isinstance(j, int) else j & (nq - 1)
        q = q_ref[q_rows(j), :]
        zero = jnp.zeros_like(q)
        k = k_ref[batch_rows(j), :]
        dist = jnp.concatenate([dist_ref[i - c + (nq - 1)] for c in range(nq)], axis=1)
        for m, qm in enumerate((jnp.where(first_map, q, zero), jnp.where(first_map, zero, q))):
            s = _dot_nt(qm, k) - dist
            s_ref[slot][m] = s
            lane_max = s[:, :LANES]
            for c in range(1, seq // LANES):
                lane_max = jnp.maximum(lane_max, s[:, c * LANES:(c + 1) * LANES])
            m_ref[slot][m] = lane_max

    def value_stage(j, slot):
        e = jnp.concatenate(
            [jnp.exp(s_ref[slot][m] - jnp.max(m_ref[slot][m], axis=-1, keepdims=True)).astype(BF16)
             for m in range(2)], axis=0)
        a_ref[slot][...] = _dot(e, v1_ref[batch_rows(j), :])

    def finish_stage(j, slot):
        acc = a_ref[slot][...]
        num1, den1 = acc[:DIFF_TQ, :DIFF_V_DIM], acc[:DIFF_TQ, DIFF_V_DIM:DIFF_V_DIM + 1]
        num2, den2 = acc[DIFF_TQ:, :DIFF_V_DIM], acc[DIFF_TQ:, DIFF_V_DIM:DIFF_V_DIM + 1]
        o = num1 * (1.0 / den1) - (lam * (1.0 / den2)) * num2
        o = _rmsnorm_f32(o, sub_ref[...]) * (1.0 - lam_init)
        o_ref[q_rows(j), :] = o.astype(BF16)

    v = v_ref[...]
    v1_ref[:, :DIFF_V_DIM] = v
    v1_ref[:, DIFF_V_DIM:] = jnp.where(
        lax.broadcasted_iota(jnp.int32, v.shape, 1) == 0, 1.0, 0.0).astype(BF16)
    _software_pipeline(nbatch * nq, (logits_stage, value_stage, finish_stage), order=(2, 0, 1))


def _diff(proj, slopes, lq1, lk1, lq2, lk2, subln, batch, seq, lam_init):
    nq = seq // DIFF_TQ
    nbatch = math.gcd(batch, DIFF_BATCH_CHUNK)
    assert nq & (nq - 1) == 0
    vec = lambda w: pl.BlockSpec((1, w), lambda h, b: (0, 0))
    blk = lambda off: pl.BlockSpec((nbatch * seq, LANES), lambda h, b, off=off: (b, off + h))
    return pl.pallas_call(
        functools.partial(_diff_body, lam_init=lam_init, seq=seq, nbatch=nbatch),
        grid=(DIFF_HEADS, batch // nbatch),
        in_specs=[
            pl.BlockSpec(memory_space=pltpu.SMEM),
            vec(DIFF_QK_DIM), vec(DIFF_QK_DIM), vec(DIFF_QK_DIM), vec(DIFF_QK_DIM), vec(DIFF_V_DIM),
            blk(O_DQ // LANES), blk(O_DK // LANES), blk(O_DV // LANES),
        ],
        out_specs=pl.BlockSpec((nbatch * seq, LANES), lambda h, b: (b, h)),
        out_shape=jax.ShapeDtypeStruct((batch * seq, DIFF_V_WIDTH), BF16),
        scratch_shapes=[
            pltpu.VMEM((2 * nq - 1, DIFF_TQ, DIFF_TQ), F32),
            pltpu.VMEM((nbatch * seq, 2 * LANES), BF16),
            pltpu.VMEM((2, DIFF_TQ, seq), F32), pltpu.VMEM((2, DIFF_TQ, seq), F32),
            pltpu.VMEM((2, DIFF_TQ, LANES), F32), pltpu.VMEM((2, DIFF_TQ, LANES), F32),
            pltpu.VMEM((2 * DIFF_TQ, 2 * LANES), F32), pltpu.VMEM((2 * DIFF_TQ, 2 * LANES), F32),
        ],
        compiler_params=pltpu.CompilerParams(
            dimension_semantics=("arbitrary", "arbitrary"),
            vmem_limit_bytes=V7X_VMEM_LIMIT_BYTES),
        name="diff_attn",
    )(slopes, lq1, lk1, lq2, lk2, subln, proj, proj, proj)


def _merge_ffn_body(x_ref, mq_ref, ona_ref, odf_ref, mk_ref, mv_ref, g_ref, wgate_hbm, bgate_ref,
                    wna_hbm, wdf_hbm, wmem_hbm, wout_hbm, fg_ref, wg_hbm, wu_hbm, wd_hbm, ng_ref,
                    o_ref, wgate_ref, wna_ref, wdf_ref, wmem_ref, wout_ref, wg_ref, wu_ref, wd_ref,
                    stage_ref, sem_ref, *, final_norm):
    @pl.when(pl.program_id(0) == 0)
    def _load_weights():
        _load_weights_bf16(
            ((wgate_hbm, wgate_ref), (wna_hbm, wna_ref), (wdf_hbm, wdf_ref), (wmem_hbm, wmem_ref),
             (wout_hbm, wout_ref), (wg_hbm, wg_ref), (wu_hbm, wu_ref), (wd_hbm, wd_ref)),
            stage_ref, sem_ref)

    x = x_ref[...]
    h = _rmsnorm_f32(x, g_ref[...]).astype(BF16)
    heads = []
    for hh in range(MEM_HEADS):
        sl = slice(hh * MEM_HEAD_DIM, (hh + 1) * MEM_HEAD_DIM)
        logits = _dot_nt(mq_ref[:, sl], mk_ref[:, sl]) * (MEM_HEAD_DIM ** -0.5)
        p = _softmax_rows(logits).astype(BF16)
        heads.append(_dot(p, mv_ref[:, sl]))
    o_mem = jnp.concatenate(heads, axis=-1).astype(BF16)
    branches = (
        _dot(ona_ref[...], wna_ref[...]),
        _dot(odf_ref[...], wdf_ref[...]),
        _dot(o_mem, wmem_ref[...]),
    )
    merged = jnp.zeros(x.shape, F32)
    for i, y in enumerate(branches):
        sl = slice(i * D_MODEL, (i + 1) * D_MODEL)
        gate = jax.nn.sigmoid(_dot(h, wgate_ref[:, sl]) + bgate_ref[:, sl])
        merged = merged + gate * y
    x2 = x + _dot(merged.astype(BF16), wout_ref[...])
    y = _swiglu_half_step(x2, fg_ref, wg_ref, wu_ref, wd_ref)
    if final_norm:
        y = _rmsnorm_f32(y, ng_ref[...])
    o_ref[...] = y


def _merge_ffn(x1, proj, o_na, o_diff, mk, mv, norm_g, w_gate, b_gate, w_na, w_df, w_mem, w_out,
               ffn_g, wg, wu, wd, final_g, seq, m_tokens, *, final_norm):
    n = x1.shape[0]
    per_b = seq // MERGE_TM
    row = lambda i: (i, 0)
    return pl.pallas_call(
        functools.partial(_merge_ffn_body, final_norm=final_norm),
        grid=(n // MERGE_TM,),
        in_specs=[
            pl.BlockSpec((MERGE_TM, D_MODEL), row),
            pl.BlockSpec((MERGE_TM, MEM_WIDTH), lambda i: (i, O_MQ // MEM_WIDTH)),
            pl.BlockSpec((MERGE_TM, NA_WIDTH), row),
            pl.BlockSpec((MERGE_TM, DIFF_V_WIDTH), row),
            pl.BlockSpec((m_tokens, MEM_WIDTH), lambda i: (i // per_b, 0)),
            pl.BlockSpec((m_tokens, MEM_WIDTH), lambda i: (i // per_b, 0)),
            _resident((1, D_MODEL)),
            _HBM,
            _resident((1, 3 * D_MODEL)),
            _HBM, _HBM, _HBM, _HBM,
            _resident((1, D_MODEL)),
            _HBM, _HBM, _HBM,
            _resident((1, D_MODEL)),
        ],
        out_specs=pl.BlockSpec((MERGE_TM, D_MODEL), row),
        out_shape=jax.ShapeDtypeStruct((n, D_MODEL), F32),
        scratch_shapes=[
            pltpu.VMEM((D_MODEL, 3 * D_MODEL), BF16),
            pltpu.VMEM((NA_WIDTH, D_MODEL), BF16), pltpu.VMEM((DIFF_V_WIDTH, D_MODEL), BF16),
            pltpu.VMEM((MEM_WIDTH, D_MODEL), BF16), pltpu.VMEM((D_MODEL, D_MODEL), BF16),
            pltpu.VMEM((D_MODEL, D_FF), BF16), pltpu.VMEM((D_MODEL, D_FF), BF16),
            pltpu.VMEM((D_FF, D_MODEL), BF16),
            pltpu.VMEM((STAGE_SLOTS, STAGE_ROWS,3 * D_MODEL), F32),
            pltpu.SemaphoreType.DMA((STAGE_SLOTS,)),
        ],
        compiler_params=pltpu.CompilerParams(
            dimension_semantics=("arbitrary",), vmem_limit_bytes=V7X_VMEM_LIMIT_BYTES),
        name="merge_ffn",
    )(x1, proj, o_na, o_diff, mk, mv, norm_g, w_gate, b_gate, w_na, w_df, w_mem, w_out,
      ffn_g, wg, wu, wd, final_g)


def kernel(x, mem, ffn1_norm, ffn1_w_gate, ffn1_w_up, ffn1_w_down, mix_norm, w_in, na_rpb,
           diff_lambda_q1, diff_lambda_k1, diff_lambda_q2, diff_lambda_k2, diff_subln,
           mem_norm, w_mem_kv, w_gate, b_gate, w_br_na, w_br_diff, w_br_mem, w_out,
           ffn2_norm, ffn2_w_gate, ffn2_w_up, ffn2_w_down, final_norm):
    batch, seq, d_model = x.shape
    m_tokens = mem.shape[1]
    depth = ffn1_norm.shape[0]
    assert d_model == D_MODEL and seq % GRID_W == 0
    assert seq % DIFF_TQ == 0 and seq % MERGE_TM == 0 and (batch * seq) % FFN_TM == 0
    slopes = jnp.asarray([2.0 ** (-8.0 * (i + 1) / DIFF_HEADS) for i in range(DIFF_HEADS)], F32)
    w32 = lambda w: w.astype(F32)
    vec = lambda v: v.reshape(1, -1).astype(F32)

    xt = x.reshape(batch * seq, d_model)
    mem2d = mem.reshape(batch * m_tokens, d_model)
    for l in range(depth):
        lam_init = 0.8 - 0.6 * math.exp(-0.3 * l)
        xt, proj = _ffn_proj(xt, vec(ffn1_norm[l]), w32(ffn1_w_gate[l]), w32(ffn1_w_up[l]),
                             w32(ffn1_w_down[l]), vec(mix_norm[l]), w32(w_in[l]))
        mk, mv = _memkv(mem2d, vec(mem_norm[l]), w32(w_mem_kv[l]), batch, m_tokens)
        o_na = _na(proj, na_rpb[l].reshape(-1).astype(F32), batch, seq)
        o_diff = _diff(proj, slopes, vec(diff_lambda_q1[l]), vec(diff_lambda_k1[l]),
                       vec(diff_lambda_q2[l]), vec(diff_lambda_k2[l]), vec(diff_subln[l]),
                       batch, seq, lam_init)
        xt = _merge_ffn(xt, proj, o_na, o_diff, mk, mv, vec(mix_norm[l]), w32(w_gate[l]),
                        vec(b_gate[l]), w32(w_br_na[l]), w32(w_br_diff[l]), w32(w_br_mem[l]),
                        w32(w_out[l]), vec(ffn2_norm[l]), w32(ffn2_w_gate[l]), w32(ffn2_w_up[l]),
                        w32(ffn2_w_down[l]), vec(final_norm), seq, m_tokens,
                        final_norm=(l == depth - 1))
    return xt.reshape(batch, seq, d_model)
```

```python
import functools
import math

import jax
import jax.numpy as jnp
from jax import lax
from jax.experimental import pallas as pl
from jax.experimental.pallas import tpu as pltpu

F32 = jnp.float32
BF16 = jnp.bfloat16

D_MODEL = 1024
GRID_W = 64
NA_HEADS = 8
NA_HEAD_DIM = 64
NA_WIN_ROWS = 8
NA_WIN_COLS = 16
DIFF_HEADS = 4
DIFF_QK_DIM = 64
DIFF_V_DIM = 128
MEM_HEADS = 4
MEM_HEAD_DIM = 128
D_FF = 2816
NORM_EPS = 1e-6
NA_WIDTH = NA_HEADS * NA_HEAD_DIM
DIFF_QK_WIDTH = DIFF_HEADS * 2 * DIFF_QK_DIM
DIFF_V_WIDTH = DIFF_HEADS * DIFF_V_DIM
MEM_WIDTH = MEM_HEADS * MEM_HEAD_DIM
IN_WIDTH = 3 * NA_WIDTH + 2 * DIFF_QK_WIDTH + DIFF_V_WIDTH + MEM_WIDTH
O_NQ = 0
O_NK = O_NQ + NA_WIDTH
O_NV = O_NK + NA_WIDTH
O_DQ = O_NV + NA_WIDTH
O_DK = O_DQ + DIFF_QK_WIDTH
O_DV = O_DK + DIFF_QK_WIDTH
O_MQ = O_DV + DIFF_V_WIDTH

LANES = 128
V7X_VMEM_LIMIT_BYTES = 56 * 1024 * 1024

FFN_TM = 512
STAGE_ROWS = 128
STAGE_SLOTS = 4
FFN_TF = 256
PROJ_TN = 512
DIFF_TQ = 512
DIFF_BATCH_CHUNK = 4
MERGE_TM = 512
NEG_BIG = -1e30


def _rmsnorm_f32(x, g):
    ms = jnp.mean(x * x, axis=-1, keepdims=True)
    return (x * lax.rsqrt(ms + NORM_EPS)) * g


def _softmax_rows(logits):
    m = jnp.max(logits, axis=-1, keepdims=True)
    e = jnp.exp(logits - m)
    s = jnp.sum(e, axis=-1, keepdims=True)
    return e * (1.0 / s)


def _dot_nt(a, b):
    return lax.dot_general(a, b, (((1,), (1,)), ((), ())), preferred_element_type=F32)


def _dot(a, b):
    return jnp.dot(a, b, preferred_element_type=F32)


def _resident(shape):
    nd = len(shape)
    return pl.BlockSpec(shape, lambda *_: (0,) * nd, pipeline_mode=pl.Buffered(1))


def _aligned(idx, multiple):
    return idx if isinstance(idx, int) else pl.multiple_of(idx, multiple)


def _software_pipeline(n_items, stages, order):
    depth = len(stages)
    assert n_items >= depth and sorted(order) == list(range(depth))

    def trip(t, parity, valid):
        for k in order:
            if valid(k):
                stages[k](t - k, (parity - k) % 2)

    for t in range(depth - 1):
        trip(t, t % 2, lambda k, t=t: k <= t)
    start = depth - 1
    if (n_items - start) % 2:
        trip(start, start % 2, lambda k: True)
        start += 1

    def body(u, carry):
        t = start + 2 * u
        trip(t, start % 2, lambda k: True)
        trip(t + 1, (start + 1) % 2, lambda k: True)
        return carry

    lax.fori_loop(0, (n_items - start) // 2, body, 0)
    for t in range(n_items, n_items + depth - 1):
        trip(t, t % 2, lambda k, t=t: t - k < n_items)


def _stage_copy(w_hbm, stage_ref, sem_ref, chunk, slot):
    width = w_hbm.shape[1]
    return pltpu.make_async_copy(
        w_hbm.at[pl.ds(chunk * STAGE_ROWS, STAGE_ROWS), :],
        stage_ref.at[slot, :, :width],
        sem_ref.at[slot])


def _load_weights_bf16(pairs, stage_ref, sem_ref):
    jobs = [(w, dst, c) for w, dst in pairs for c in range(w.shape[0] // STAGE_ROWS)]
    ahead = STAGE_SLOTS - 1
    for n in range(min(ahead, len(jobs))):
        _stage_copy(jobs[n][0], stage_ref, sem_ref, jobs[n][2], n % STAGE_SLOTS).start()
    for n, (w, dst, c) in enumerate(jobs):
        slot = n % STAGE_SLOTS
        if n + ahead < len(jobs):
            nw, _, nc = jobs[n + ahead]
            _stage_copy(nw, stage_ref, sem_ref, nc, (n + ahead) % STAGE_SLOTS).start()
        _stage_copy(w, stage_ref, sem_ref, c, slot).wait()
        dst[c * STAGE_ROWS:(c + 1) * STAGE_ROWS, :] = (
            stage_ref[slot, :, :w.shape[1]].astype(BF16))


_HBM = pl.BlockSpec(memory_space=pl.ANY)


def _swiglu_half_step(x, g_ref, wg_ref, wu_ref, wd_ref):
    h = _rmsnorm_f32(x, g_ref[...]).astype(BF16)
    acc = jnp.zeros(x.shape, F32)
    for c in range(D_FF // FFN_TF):
        sl = slice(c * FFN_TF, (c + 1) * FFN_TF)
        gate = _dot(h, wg_ref[:, sl])
        up = _dot(h, wu_ref[:, sl])
        act = (gate * jax.nn.sigmoid(gate) * up).astype(BF16)
        acc = acc + _dot(act, wd_ref[sl, :])
    return x + 0.5 * acc


def _ffn_proj_body(x_ref, g_ref, wg_hbm, wu_hbm, wd_hbm, pg_ref, win_hbm, x1_ref, proj_ref,
                   wg_ref, wu_ref, wd_ref, win_ref, stage_ref, sem_ref):
    @pl.when(pl.program_id(0) == 0)
    def _load_weights():
        _load_weights_bf16(((wg_hbm, wg_ref), (wu_hbm, wu_ref), (wd_hbm, wd_ref),
                            (win_hbm, win_ref)), stage_ref, sem_ref)

    x1 = _swiglu_half_step(x_ref[...], g_ref, wg_ref, wu_ref, wd_ref)
    x1_ref[...] = x1
    h = _rmsnorm_f32(x1, pg_ref[...]).astype(BF16)
    for c in range(IN_WIDTH // PROJ_TN):
        lo = c * PROJ_TN
        y = _dot(h, win_ref[:, lo:lo + PROJ_TN])
        if lo == O_NQ or lo == O_DQ:
            y = y * (NA_HEAD_DIM ** -0.5)
        proj_ref[:, lo:lo + PROJ_TN] = y.astype(BF16)


def _ffn_proj(x, norm_g, wg, wu, wd, mix_g, w_in):
    n = x.shape[0]
    row = lambda i: (i, 0)
    return pl.pallas_call(
        _ffn_proj_body,
        grid=(n // FFN_TM,),
        in_specs=[
            pl.BlockSpec((FFN_TM, D_MODEL), row),
            _resident((1, D_MODEL)),
            _HBM, _HBM, _HBM,
            _resident((1, D_MODEL)),
            _HBM,
        ],
        out_specs=[pl.BlockSpec((FFN_TM, D_MODEL), row), pl.BlockSpec((FFN_TM, IN_WIDTH), row)],
        out_shape=[jax.ShapeDtypeStruct((n, D_MODEL), F32),
                   jax.ShapeDtypeStruct((n, IN_WIDTH), BF16)],
        scratch_shapes=[
            pltpu.VMEM((D_MODEL, D_FF), BF16), pltpu.VMEM((D_MODEL, D_FF), BF16),
            pltpu.VMEM((D_FF, D_MODEL), BF16), pltpu.VMEM((D_MODEL, IN_WIDTH), BF16),
            pltpu.VMEM((STAGE_SLOTS, STAGE_ROWS,IN_WIDTH), F32),
            pltpu.SemaphoreType.DMA((STAGE_SLOTS,)),
        ],
        compiler_params=pltpu.CompilerParams(
            dimension_semantics=("arbitrary",), vmem_limit_bytes=V7X_VMEM_LIMIT_BYTES),
        name="ffn_proj",
    )(x, norm_g, wg, wu, wd, mix_g, w_in)


def _memkv_body(m_ref, g_ref, w_hbm, k_ref, v_ref, w_ref, stage_ref, sem_ref):
    @pl.when(pl.program_id(0) == 0)
    def _load_weights():
        _load_weights_bf16(((w_hbm, w_ref),), stage_ref, sem_ref)

    h = _rmsnorm_f32(m_ref[...], g_ref[...]).astype(BF16)
    k_ref[...] = _dot(h, w_ref[:, :MEM_WIDTH]).astype(BF16)
    v_ref[...] = _dot(h, w_ref[:, MEM_WIDTH:]).astype(BF16)


def _memkv(mem2d, norm_g, w_kv, batch, m_tokens):
    row = lambda i: (i, 0)
    shp = jax.ShapeDtypeStruct((batch * m_tokens, MEM_WIDTH), BF16)
    return pl.pallas_call(
        _memkv_body,
        grid=(batch,),
        in_specs=[
            pl.BlockSpec((m_tokens, D_MODEL), row),
            _resident((1, D_MODEL)),
            _HBM,
        ],
        out_specs=[pl.BlockSpec((m_tokens, MEM_WIDTH), row)] * 2,
        out_shape=[shp, shp],
        scratch_shapes=[
            pltpu.VMEM((D_MODEL, 2 * MEM_WIDTH), BF16),
            pltpu.VMEM((STAGE_SLOTS, STAGE_ROWS,2 * MEM_WIDTH), F32),
            pltpu.SemaphoreType.DMA((STAGE_SLOTS,)),
        ],
        compiler_params=pltpu.CompilerParams(dimension_semantics=("arbitrary",)),
        name="memkv",
    )(mem2d, norm_g, w_kv)


RPB_ROWS = 2 * NA_WIN_ROWS - 1
RPB_COLS = 2 * NA_WIN_COLS - 1
NA_BAND_KEYS = NA_WIN_ROWS * GRID_W
NA_BATCH_CHUNK = 4
NA_ITEM_ROWS = 4


def _na_body(rpb_ref, q_ref, k_ref, v_ref, o_ref, tile_ref, band_ref,
             lg0_ref, lg1_ref, m0_ref, m1_ref, a0_ref, a1_ref, *, rows, nbatch):
    lg_ref, m_ref, a_ref = (lg0_ref, lg1_ref), (m0_ref, m1_ref), (a0_ref, a1_ref)
    hp = pl.program_id(0)
    b = pl.program_id(1)
    wr = min(NA_WIN_ROWS, rows)

    @pl.when(b == 0)
    def _build_bias():
        c = lax.broadcasted_iota(jnp.int32, (GRID_W, LANES), 0)
        kc = lax.broadcasted_iota(jnp.int32, (GRID_W, LANES), 1) & (GRID_W - 1)
        rel = kc - c
        cs = jnp.clip(c - NA_WIN_COLS // 2, 0, GRID_W - NA_WIN_COLS)
        valid = (kc >= cs) & (kc < cs + NA_WIN_COLS)

        def tile_step(t, carry):
            hl = t // RPB_ROWS
            dr = t - hl * RPB_ROWS
            base = ((hp * 2 + hl) * RPB_ROWS + dr) * RPB_COLS
            acc = jnp.full((GRID_W, LANES), NEG_BIG, F32)
            for dc in range(RPB_COLS):
                acc = jnp.where(rel == dc - (NA_WIN_COLS - 1), rpb_ref[base + dc], acc)
            tile_ref[hl, dr] = jnp.where(valid, acc, NEG_BIG)
            return carry

        lax.fori_loop(0, 2 * RPB_ROWS, tile_step, 0)
        low = lax.broadcasted_iota(jnp.int32, (GRID_W, LANES), 1) < GRID_W
        for hl in range(2):
            for dr0 in range(NA_WIN_ROWS):
                for j in range(wr // 2):
                    band_ref[dr0, hl * GRID_W:(hl + 1) * GRID_W, j * LANES:(j + 1) * LANES] = (
                        jnp.where(low, tile_ref[hl, dr0 + 2 * j], tile_ref[hl, dr0 + 2 * j + 1]))

    lane = lax.broadcasted_iota(jnp.int32, (GRID_W, LANES), 1)
    low = lane < NA_HEAD_DIM

    items_per_batch = rows // NA_ITEM_ROWS
    ipb_shift = items_per_batch.bit_length() - 1

    def locate(item, rr):
        if isinstance(item, int):
            bl, it = divmod(item, items_per_batch)
        else:
            bl, it = lax.shift_right_logical(item, ipb_shift), item & (items_per_batch - 1)
        return it * NA_ITEM_ROWS + rr, bl * (rows * GRID_W)

    def band_rows(r, base):
        if isinstance(r, int):
            rs = min(max(r - wr // 2, 0), rows - wr)
        else:
            rs = jnp.clip(r - wr // 2, 0, rows - wr)
        return rs, pl.ds(_aligned(base + rs * GRID_W, GRID_W), wr * GRID_W)

    def query_rows(r, base):
        return pl.ds(_aligned(base + r * GRID_W, GRID_W), GRID_W)

    def logits_stage(item, slot):
        for rr in range(NA_ITEM_ROWS):
            r, base = locate(item, rr)
            rs, keys = band_rows(r, base)
            q2 = q_ref[query_rows(r, base), :]
            zero = jnp.zeros_like(q2)
            q_st = jnp.concatenate([jnp.where(low, q2, zero), jnp.where(low, zero, q2)], axis=0)
            lg = _dot_nt(q_st, k_ref[keys, :]) + band_ref[rs - r + (NA_WIN_ROWS - 1)]
            lg_ref[slot][rr] = lg
            lane_max = lg[:, :LANES]
            for c in range(1, NA_BAND_KEYS // LANES):
                lane_max = jnp.maximum(lane_max, lg[:, c * LANES:(c + 1) * LANES])
            m_ref[slot][rr] = lane_max

    def value_stage(item, slot):
        for rr in range(NA_ITEM_ROWS):
            r, base = locate(item, rr)
            _, keys = band_rows(r, base)
            row_max = jnp.max(m_ref[slot][rr], axis=-1, keepdims=True)
            e = jnp.exp(lg_ref[slot][rr] - row_max)
            lane_sum = e[:, :LANES]
            for c in range(1, NA_BAND_KEYS // LANES):
                lane_sum = lane_sum + e[:, c * LANES:(c + 1) * LANES]
            a_ref[slot][rr, :, :LANES] = _dot(e.astype(BF16), v_ref[keys, :])
            a_ref[slot][rr, :, LANES:] = lane_sum

    def finish_stage(item, slot):
        for rr in range(NA_ITEM_ROWS):
            r, base = locate(item, rr)
            acc = a_ref[slot][rr]
            o_st = acc[:, :LANES] * (1.0 / jnp.sum(acc[:, LANES:], axis=-1, keepdims=True))
            o = jnp.where(low, o_st[:GRID_W], o_st[GRID_W:])
            o_ref[query_rows(r, base), :] = o.astype(BF16)

    _software_pipeline(nbatch * items_per_batch, (logits_stage, value_stage, finish_stage),
                       order=(2, 0, 1))


def _na(proj, rpb_flat, batch, seq):
    rows = seq // GRID_W
    items_per_batch = rows // NA_ITEM_ROWS
    nbatch = math.gcd(batch, NA_BATCH_CHUNK)
    assert rows >= NA_WIN_ROWS and NA_WIN_ROWS % 2 == 0 and rows % NA_ITEM_ROWS == 0
    assert items_per_batch & (items_per_batch - 1) == 0
    pairs = NA_HEADS // 2
    blk = lambda off: pl.BlockSpec((nbatch * seq, LANES), lambda hp, b, off=off: (b, off + hp))
    return pl.pallas_call(
        functools.partial(_na_body, rows=rows, nbatch=nbatch),
        grid=(pairs, batch // nbatch),
        in_specs=[
            pl.BlockSpec(memory_space=pltpu.SMEM),
            blk(O_NQ // LANES), blk(O_NK // LANES), blk(O_NV // LANES),
        ],
        out_specs=pl.BlockSpec((nbatch * seq, LANES), lambda hp, b: (b, hp)),
        out_shape=jax.ShapeDtypeStruct((batch * seq, NA_WIDTH), BF16),
        scratch_shapes=[
            pltpu.VMEM((2, RPB_ROWS, GRID_W, LANES), F32),
            pltpu.VMEM((NA_WIN_ROWS, 2 * GRID_W, NA_BAND_KEYS), F32),
            pltpu.VMEM((NA_ITEM_ROWS, 2 * GRID_W, NA_BAND_KEYS), F32),
            pltpu.VMEM((NA_ITEM_ROWS, 2 * GRID_W, NA_BAND_KEYS), F32),
            pltpu.VMEM((NA_ITEM_ROWS, 2 * GRID_W, LANES), F32),
            pltpu.VMEM((NA_ITEM_ROWS, 2 * GRID_W, LANES), F32),
            pltpu.VMEM((NA_ITEM_ROWS, 2 * GRID_W, 2 * LANES), F32),
            pltpu.VMEM((NA_ITEM_ROWS, 2 * GRID_W, 2 * LANES), F32),
        ],
        compiler_params=pltpu.CompilerParams(
            dimension_semantics=("arbitrary", "arbitrary"), vmem_limit_bytes=V7X_VMEM_LIMIT_BYTES),
        name="na_attn",
    )(rpb_flat, proj, proj, proj)


def _diff_body(slope_ref, lq1_ref, lk1_ref, lq2_ref, lk2_ref, sub_ref, q_ref, k_ref, v_ref, o_ref,
               dist_ref, v1_ref, kt_ref, s0_ref, s1_ref, m0_ref, m1_ref, a0_ref, a1_ref,
               *, lam_init, seq, nbatch):
    s_ref, m_ref, a_ref = (s0_ref, s1_ref), (m0_ref, m1_ref), (a0_ref, a1_ref)
    h = pl.program_id(0)
    b = pl.program_id(1)
    nq = seq // DIFF_TQ

    @pl.when(b == 0)
    def _build_dist():
        slope = slope_ref[h]
        rel = (lax.broadcasted_iota(jnp.int32, (DIFF_TQ, DIFF_TQ), 0)
               - lax.broadcasted_iota(jnp.int32, (DIFF_TQ, DIFF_TQ), 1))
        for d in range(2 * nq - 1):
            off = (d - (nq - 1)) * DIFF_TQ
            dist_ref[d] = slope * jnp.abs((rel + off).astype(F32))

    lam = (jnp.exp(jnp.sum(lq1_ref[...] * lk1_ref[...], axis=-1, keepdims=True))
           - jnp.exp(jnp.sum(lq2_ref[...] * lk2_ref[...], axis=-1, keepdims=True))
           + lam_init)
    lane = lax.broadcasted_iota(jnp.int32, (DIFF_TQ, LANES), 1)
    first_map = lane < DIFF_QK_DIM
    nq_shift = nq.bit_length() - 1

    def q_rows(j):
        return pl.ds(_aligned(j * DIFF_TQ, DIFF_TQ), DIFF_TQ)

    def batch_rows(j):
        bl = j // nq if isinstance(j, int) else lax.shift_right_logical(j, nq_shift)
        return pl.ds(_aligned(bl * seq, seq), seq)

    def logits_stage(j, slot):
        i = j % nq if isinstance(j, int) else j & (nq - 1)
        q = q_ref[q_rows(j), :]
        zero = jnp.zeros_like(q)
        k_t = kt_ref[j // nq if isinstance(j, int) else lax.shift_right_logical(j, nq_shift)]
        dist = jnp.concatenate([dist_ref[i - c + (nq - 1)] for c in range(nq)], axis=1)
        for m, qm in enumerate((jnp.where(first_map, q, zero), jnp.where(first_map, zero, q))):
            s = _dot(qm, k_t) - dist
            s_ref[slot][m] = s
            lane_max = s[:, :LANES]
            for c in range(1, seq // LANES):
                lane_max = jnp.maximum(lane_max, s[:, c * LANES:(c + 1) * LANES])
            m_ref[slot][m] = lane_max

    def value_stage(j, slot):
        e = jnp.concatenate(
            [jnp.exp(s_ref[slot][m] - jnp.max(m_ref[slot][m], axis=-1, keepdims=True)).astype(BF16)
             for m in range(2)], axis=0)
        a_ref[slot][...] = _dot(e, v1_ref[batch_rows(j), :])

    def finish_stage(j, slot):
        acc = a_ref[slot][...]
        num1, den1 = acc[:DIFF_TQ, :DIFF_V_DIM], acc[:DIFF_TQ, DIFF_V_DIM:DIFF_V_DIM + 1]
        num2, den2 = acc[DIFF_TQ:, :DIFF_V_DIM], acc[DIFF_TQ:, DIFF_V_DIM:DIFF_V_DIM + 1]
        o = num1 * (1.0 / den1) - (lam * (1.0 / den2)) * num2
        o = _rmsnorm_f32(o, sub_ref[...]) * (1.0 - lam_init)
        o_ref[q_rows(j), :] = o.astype(BF16)

    v = v_ref[...]
    v1_ref[:, :DIFF_V_DIM] = v
    v1_ref[:, DIFF_V_DIM:] = jnp.where(
        lax.broadcasted_iota(jnp.int32, v.shape, 1) == 0, 1.0, 0.0).astype(BF16)
    for bl in range(nbatch):
        kt_ref[bl] = k_ref[bl * seq:(bl + 1) * seq, :].T
    _software_pipeline(nbatch * nq, (logits_stage, value_stage, finish_stage), order=(2, 0, 1))


def _diff(proj, slopes, lq1, lk1, lq2, lk2, subln, batch, seq, lam_init):
    nq = seq // DIFF_TQ
    nbatch = math.gcd(batch, DIFF_BATCH_CHUNK)
    assert nq & (nq - 1) == 0
    vec = lambda w: pl.BlockSpec((1, w), lambda h, b: (0, 0))
    blk = lambda off: pl.BlockSpec((nbatch * seq, LANES), lambda h, b, off=off: (b, off + h))
    return pl.pallas_call(
        functools.partial(_diff_body, lam_init=lam_init, seq=seq, nbatch=nbatch),
        grid=(DIFF_HEADS, batch // nbatch),
        in_specs=[
            pl.BlockSpec(memory_space=pltpu.SMEM),
            vec(DIFF_QK_DIM), vec(DIFF_QK_DIM), vec(DIFF_QK_DIM), vec(DIFF_QK_DIM), vec(DIFF_V_DIM),
            blk(O_DQ // LANES), blk(O_DK // LANES), blk(O_DV // LANES),
        ],
        out_specs=pl.BlockSpec((nbatch * seq, LANES), lambda h, b: (b, h)),
        out_shape=jax.ShapeDtypeStruct((batch * seq, DIFF_V_WIDTH), BF16),
        scratch_shapes=[
            pltpu.VMEM((2 * nq - 1, DIFF_TQ, DIFF_TQ), F32),
            pltpu.VMEM((nbatch * seq, 2 * LANES), BF16),
            pltpu.VMEM((nbatch, LANES, seq), BF16),
            pltpu.VMEM((2, DIFF_TQ, seq), F32), pltpu.VMEM((2, DIFF_TQ, seq), F32),
            pltpu.VMEM((2, DIFF_TQ, LANES), F32), pltpu.VMEM((2, DIFF_TQ, LANES), F32),
            pltpu.VMEM((2 * DIFF_TQ, 2 * LANES), F32), pltpu.VMEM((2 * DIFF_TQ, 2 * LANES), F32),
        ],
        compiler_params=pltpu.CompilerParams(
            dimension_semantics=("arbitrary", "arbitrary"),
            vmem_limit_bytes=V7X_VMEM_LIMIT_BYTES),
        name="diff_attn",
    )(slopes, lq1, lk1, lq2, lk2, subln, proj, proj, proj)


def _merge_ffn_body(x_ref, mq_ref, ona_ref, odf_ref, mk_ref, mv_ref, g_ref, wgate_hbm, bgate_ref,
                    wna_hbm, wdf_hbm, wmem_hbm, wout_hbm, fg_ref, wg_hbm, wu_hbm, wd_hbm, ng_ref,
                    o_ref, wgate_ref, wna_ref, wdf_ref, wmem_ref, wout_ref, wg_ref, wu_ref, wd_ref,
                    stage_ref, sem_ref, *, final_norm):
    @pl.when(pl.program_id(0) == 0)
    def _load_weights():
        _load_weights_bf16(
            ((wgate_hbm, wgate_ref), (wna_hbm, wna_ref), (wdf_hbm, wdf_ref), (wmem_hbm, wmem_ref),
             (wout_hbm, wout_ref), (wg_hbm, wg_ref), (wu_hbm, wu_ref), (wd_hbm, wd_ref)),
            stage_ref, sem_ref)

    x = x_ref[...]
    h = _rmsnorm_f32(x, g_ref[...]).astype(BF16)
    heads = []
    for hh in range(MEM_HEADS):
        sl = slice(hh * MEM_HEAD_DIM, (hh + 1) * MEM_HEAD_DIM)
        logits = _dot_nt(mq_ref[:, sl], mk_ref[:, sl]) * (MEM_HEAD_DIM ** -0.5)
        p = _softmax_rows(logits).astype(BF16)
        heads.append(_dot(p, mv_ref[:, sl]))
    o_mem = jnp.concatenate(heads, axis=-1).astype(BF16)
    branches = (
        _dot(ona_ref[...], wna_ref[...]),
        _dot(odf_ref[...], wdf_ref[...]),
        _dot(o_mem, wmem_ref[...]),
    )
    merged = jnp.zeros(x.shape, F32)
    for i, y in enumerate(branches):
        sl = slice(i * D_MODEL, (i + 1) * D_MODEL)
        gate = jax.nn.sigmoid(_dot(h, wgate_ref[:, sl]) + bgate_ref[:, sl])
        merged = merged + gate * y
    x2 = x + _dot(merged.astype(BF16), wout_ref[...])
    y = _swiglu_half_step(x2, fg_ref, wg_ref, wu_ref, wd_ref)
    if final_norm:
        y = _rmsnorm_f32(y, ng_ref[...])
    o_ref[...] = y


def _merge_ffn(x1, proj, o_na, o_diff, mk, mv, norm_g, w_gate, b_gate, w_na, w_df, w_mem, w_out,
               ffn_g, wg, wu, wd, final_g, seq, m_tokens, *, final_norm):
    n = x1.shape[0]
    per_b = seq // MERGE_TM
    row = lambda i: (i, 0)
    return pl.pallas_call(
        functools.partial(_merge_ffn_body, final_norm=final_norm),
        grid=(n // MERGE_TM,),
        in_specs=[
            pl.BlockSpec((MERGE_TM, D_MODEL), row),
            pl.BlockSpec((MERGE_TM, MEM_WIDTH), lambda i: (i, O_MQ // MEM_WIDTH)),
            pl.BlockSpec((MERGE_TM, NA_WIDTH), row),
            pl.BlockSpec((MERGE_TM, DIFF_V_WIDTH), row),
            pl.BlockSpec((m_tokens, MEM_WIDTH), lambda i: (i // per_b, 0)),
            pl.BlockSpec((m_tokens, MEM_WIDTH), lambda i: (i // per_b, 0)),
            _resident((1, D_MODEL)),
            _HBM,
            _resident((1, 3 * D_MODEL)),
            _HBM, _HBM, _HBM, _HBM,
            _resident((1, D_MODEL)),
            _HBM, _HBM, _HBM,
            _resident((1, D_MODEL)),
        ],
        out_specs=pl.BlockSpec((MERGE_TM, D_MODEL), row),
        out_shape=jax.ShapeDtypeStruct((n, D_MODEL), F32),
        scratch_shapes=[
            pltpu.VMEM((D_MODEL, 3 * D_MODEL), BF16),
            pltpu.VMEM((NA_WIDTH, D_MODEL), BF16), pltpu.VMEM((DIFF_V_WIDTH, D_MODEL), BF16),
            pltpu.VMEM((MEM_WIDTH, D_MODEL), BF16), pltpu.VMEM((D_MODEL, D_MODEL), BF16),
            pltpu.VMEM((D_MODEL, D_FF), BF16), pltpu.VMEM((D_MODEL, D_FF), BF16),
            pltpu.VMEM((D_FF, D_MODEL), BF16),
            pltpu.VMEM((STAGE_SLOTS, STAGE_ROWS,3 * D_MODEL), F32),
            pltpu.SemaphoreType.DMA((STAGE_SLOTS,)),
        ],
        compiler_params=pltpu.CompilerParams(
            dimension_semantics=("arbitrary",), vmem_limit_bytes=V7X_VMEM_LIMIT_BYTES),
        name="merge_ffn",
    )(x1, proj, o_na, o_diff, mk, mv, norm_g, w_gate, b_gate, w_na, w_df, w_mem, w_out,
      ffn_g, wg, wu, wd, final_g)


def kernel(x, mem, ffn1_norm, ffn1_w_gate, ffn1_w_up, ffn1_w_down, mix_norm, w_in, na_rpb,
           diff_lambda_q1, diff_lambda_k1, diff_lambda_q2, diff_lambda_k2, diff_subln,
           mem_norm, w_mem_kv, w_gate, b_gate, w_br_na, w_br_diff, w_br_mem, w_out,
           ffn2_norm, ffn2_w_gate, ffn2_w_up, ffn2_w_down, final_norm):
    batch, seq, d_model = x.shape
    m_tokens = mem.shape[1]
    depth = ffn1_norm.shape[0]
    assert d_model == D_MODEL and seq % GRID_W == 0
    assert seq % DIFF_TQ == 0 and seq % MERGE_TM == 0 and (batch * seq) % FFN_TM == 0
    slopes = jnp.asarray([2.0 ** (-8.0 * (i + 1) / DIFF_HEADS) for i in range(DIFF_HEADS)], F32)
    w32 = lambda w: w.astype(F32)
    vec = lambda v: v.reshape(1, -1).astype(F32)

    xt = x.reshape(batch * seq, d_model)
    mem2d = mem.reshape(batch * m_tokens, d_model)
    for l in range(depth):
        lam_init = 0.8 - 0.6 * math.exp(-0.3 * l)
        xt, proj = _ffn_proj(xt, vec(ffn1_norm[l]), w32(ffn1_w_gate[l]), w32(ffn1_w_up[l]),
                             w32(ffn1_w_down[l]), vec(mix_norm[l]), w32(w_in[l]))
        mk, mv = _memkv(mem2d, vec(mem_norm[l]), w32(w_mem_kv[l]), batch, m_tokens)
        o_na = _na(proj, na_rpb[l].reshape(-1).astype(F32), batch, seq)
        o_diff = _diff(proj, slopes, vec(diff_lambda_q1[l]), vec(diff_lambda_k1[l]),
                       vec(diff_lambda_q2[l]), vec(diff_lambda_k2[l]), vec(diff_subln[l]),
                       batch, seq, lam_init)
        xt = _merge_ffn(xt, proj, o_na, o_diff, mk, mv, vec(mix_norm[l]), w32(w_gate[l]),
                        vec(b_gate[l]), w32(w_br_na[l]), w32(w_br_diff[l]), w32(w_br_mem[l]),
                        w32(w_out[l]), vec(ffn2_norm[l]), w32(ffn2_w_gate[l]), w32(ffn2_w_up[l]),
                        w32(ffn2_w_down[l]), vec(final_norm), seq, m_tokens,
                        final_norm=(l == depth - 1))
    return xt.reshape(batch, seq, d_model)
```

```python
import functools
import math

import jax
import jax.numpy as jnp
from jax import lax
from jax.experimental import pallas as pl
from jax.experimental.pallas import tpu as pltpu

F32 = jnp.float32
BF16 = jnp.bfloat16

D_MODEL = 1024
GRID_W = 64
NA_HEADS = 8
NA_HEAD_DIM = 64
NA_WIN_ROWS = 8
NA_WIN_COLS = 16
DIFF_HEADS = 4
DIFF_QK_DIM = 64
DIFF_V_DIM = 128
MEM_HEADS = 4
MEM_HEAD_DIM = 128
D_FF = 2816
NORM_EPS = 1e-6
NA_WIDTH = NA_HEADS * NA_HEAD_DIM
DIFF_QK_WIDTH = DIFF_HEADS * 2 * DIFF_QK_DIM
DIFF_V_WIDTH = DIFF_HEADS * DIFF_V_DIM
MEM_WIDTH = MEM_HEADS * MEM_HEAD_DIM
IN_WIDTH = 3 * NA_WIDTH + 2 * DIFF_QK_WIDTH + DIFF_V_WIDTH + MEM_WIDTH
O_NQ = 0
O_NK = O_NQ + NA_WIDTH
O_NV = O_NK + NA_WIDTH
O_DQ = O_NV + NA_WIDTH
O_DK = O_DQ + DIFF_QK_WIDTH
O_DV = O_DK + DIFF_QK_WIDTH
O_MQ = O_DV + DIFF_V_WIDTH

LANES = 128
V7X_VMEM_LIMIT_BYTES = 56 * 1024 * 1024

FFN_TM = 512
STAGE_ROWS = 128
STAGE_SLOTS = 4
FFN_TF = 256
PROJ_TN = 512
DIFF_TQ = 512
DIFF_BATCH_CHUNK = 4
MERGE_TM = 512
MEMKV_TM = 1024
NEG_BIG = -1e30


def _rmsnorm_f32(x, g):
    ms = jnp.mean(x * x, axis=-1, keepdims=True)
    return (x * lax.rsqrt(ms + NORM_EPS)) * g


def _softmax_rows(logits):
    m = jnp.max(logits, axis=-1, keepdims=True)
    e = jnp.exp(logits - m)
    s = jnp.sum(e, axis=-1, keepdims=True)
    return e * (1.0 / s)


def _dot_nt(a, b):
    return lax.dot_general(a, b, (((1,), (1,)), ((), ())), preferred_element_type=F32)


def _dot(a, b):
    return jnp.dot(a, b, preferred_element_type=F32)


def _resident(shape):
    nd = len(shape)
    return pl.BlockSpec(shape, lambda *_: (0,) * nd, pipeline_mode=pl.Buffered(1))


def _aligned(idx, multiple):
    return idx if isinstance(idx, int) else pl.multiple_of(idx, multiple)


def _software_pipeline(n_items, stages, order):
    depth = len(stages)
    assert n_items >= depth and sorted(order) == list(range(depth))

    def trip(t, parity, valid):
        for k in order:
            if valid(k):
                stages[k](t - k, (parity - k) % 2)

    for t in range(depth - 1):
        trip(t, t % 2, lambda k, t=t: k <= t)
    start = depth - 1
    if (n_items - start) % 2:
        trip(start, start % 2, lambda k: True)
        start += 1

    def body(u, carry):
        t = start + 2 * u
        trip(t, start % 2, lambda k: True)
        trip(t + 1, (start + 1) % 2, lambda k: True)
        return carry

    lax.fori_loop(0, (n_items - start) // 2, body, 0)
    for t in range(n_items, n_items + depth - 1):
        trip(t, t % 2, lambda k, t=t: t - k < n_items)


def _stage_copy(w_hbm, stage_ref, sem_ref, chunk, slot):
    width = w_hbm.shape[1]
    return pltpu.make_async_copy(
        w_hbm.at[pl.ds(chunk * STAGE_ROWS, STAGE_ROWS), :],
        stage_ref.at[slot, :, :width],
        sem_ref.at[slot])


def _load_weights_bf16(pairs, stage_ref, sem_ref):
    jobs = [(w, dst, c) for w, dst in pairs for c in range(w.shape[0] // STAGE_ROWS)]
    ahead = STAGE_SLOTS - 1
    for n in range(min(ahead, len(jobs))):
        _stage_copy(jobs[n][0], stage_ref, sem_ref, jobs[n][2], n % STAGE_SLOTS).start()
    for n, (w, dst, c) in enumerate(jobs):
        slot = n % STAGE_SLOTS
        if n + ahead < len(jobs):
            nw, _, nc = jobs[n + ahead]
            _stage_copy(nw, stage_ref, sem_ref, nc, (n + ahead) % STAGE_SLOTS).start()
        _stage_copy(w, stage_ref, sem_ref, c, slot).wait()
        dst[c * STAGE_ROWS:(c + 1) * STAGE_ROWS, :] = (
            stage_ref[slot, :, :w.shape[1]].astype(BF16))


_HBM = pl.BlockSpec(memory_space=pl.ANY)


def _swiglu_half_step(x, g_ref, wg_ref, wu_ref, wd_ref):
    h = _rmsnorm_f32(x, g_ref[...]).astype(BF16)
    acc = jnp.zeros(x.shape, F32)
    for c in range(D_FF // FFN_TF):
        sl = slice(c * FFN_TF, (c + 1) * FFN_TF)
        gate = _dot(h, wg_ref[:, sl])
        up = _dot(h, wu_ref[:, sl])
        act = (gate * jax.nn.sigmoid(gate) * up).astype(BF16)
        acc = acc + _dot(act, wd_ref[sl, :])
    return x + 0.5 * acc


def _ffn_proj_body(x_ref, g_ref, wg_hbm, wu_hbm, wd_hbm, pg_ref, win_hbm, x1_ref, proj_ref,
                   wg_ref, wu_ref, wd_ref, win_ref, stage_ref, sem_ref):
    @pl.when(pl.program_id(0) == 0)
    def _load_weights():
        _load_weights_bf16(((wg_hbm, wg_ref), (wu_hbm, wu_ref), (wd_hbm, wd_ref),
                            (win_hbm, win_ref)), stage_ref, sem_ref)

    x1 = _swiglu_half_step(x_ref[...], g_ref, wg_ref, wu_ref, wd_ref)
    x1_ref[...] = x1
    h = _rmsnorm_f32(x1, pg_ref[...]).astype(BF16)
    for c in range(IN_WIDTH // PROJ_TN):
        lo = c * PROJ_TN
        y = _dot(h, win_ref[:, lo:lo + PROJ_TN])
        if lo == O_NQ or lo == O_DQ:
            y = y * (NA_HEAD_DIM ** -0.5)
        proj_ref[:, lo:lo + PROJ_TN] = y.astype(BF16)


def _ffn_proj(x, norm_g, wg, wu, wd, mix_g, w_in):
    n = x.shape[0]
    row = lambda i: (i, 0)
    return pl.pallas_call(
        _ffn_proj_body,
        grid=(n // FFN_TM,),
        in_specs=[
            pl.BlockSpec((FFN_TM, D_MODEL), row),
            _resident((1, D_MODEL)),
            _HBM, _HBM, _HBM,
            _resident((1, D_MODEL)),
            _HBM,
        ],
        out_specs=[pl.BlockSpec((FFN_TM, D_MODEL), row), pl.BlockSpec((FFN_TM, IN_WIDTH), row)],
        out_shape=[jax.ShapeDtypeStruct((n, D_MODEL), F32),
                   jax.ShapeDtypeStruct((n, IN_WIDTH), BF16)],
        scratch_shapes=[
            pltpu.VMEM((D_MODEL, D_FF), BF16), pltpu.VMEM((D_MODEL, D_FF), BF16),
            pltpu.VMEM((D_FF, D_MODEL), BF16), pltpu.VMEM((D_MODEL, IN_WIDTH), BF16),
            pltpu.VMEM((STAGE_SLOTS, STAGE_ROWS,IN_WIDTH), F32),
            pltpu.SemaphoreType.DMA((STAGE_SLOTS,)),
        ],
        compiler_params=pltpu.CompilerParams(
            dimension_semantics=("arbitrary",), vmem_limit_bytes=V7X_VMEM_LIMIT_BYTES),
        name="ffn_proj",
    )(x, norm_g, wg, wu, wd, mix_g, w_in)


def _memkv_body(m_ref, g_ref, w_hbm, k_ref, v_ref, w_ref, stage_ref, sem_ref):
    @pl.when(pl.program_id(0) == 0)
    def _load_weights():
        _load_weights_bf16(((w_hbm, w_ref),), stage_ref, sem_ref)

    h = _rmsnorm_f32(m_ref[...], g_ref[...]).astype(BF16)
    k_ref[...] = _dot(h, w_ref[:, :MEM_WIDTH]).astype(BF16)
    v_ref[...] = _dot(h, w_ref[:, MEM_WIDTH:]).astype(BF16)


def _memkv(mem2d, norm_g, w_kv):
    n = mem2d.shape[0]
    tm = math.gcd(n, MEMKV_TM)
    row = lambda i: (i, 0)
    shp = jax.ShapeDtypeStruct((n, MEM_WIDTH), BF16)
    return pl.pallas_call(
        _memkv_body,
        grid=(n // tm,),
        in_specs=[
            pl.BlockSpec((tm, D_MODEL), row),
            _resident((1, D_MODEL)),
            _HBM,
        ],
        out_specs=[pl.BlockSpec((tm, MEM_WIDTH), row)] * 2,
        out_shape=[shp, shp],
        scratch_shapes=[
            pltpu.VMEM((D_MODEL, 2 * MEM_WIDTH), BF16),
            pltpu.VMEM((STAGE_SLOTS, STAGE_ROWS,2 * MEM_WIDTH), F32),
            pltpu.SemaphoreType.DMA((STAGE_SLOTS,)),
        ],
        compiler_params=pltpu.CompilerParams(dimension_semantics=("arbitrary",)),
        name="memkv",
    )(mem2d, norm_g, w_kv)


RPB_ROWS = 2 * NA_WIN_ROWS - 1
RPB_COLS = 2 * NA_WIN_COLS - 1
NA_BAND_KEYS = NA_WIN_ROWS * GRID_W
NA_BATCH_CHUNK = 4
NA_ITEM_ROWS = 4


def _na_body(rpb_ref, q_ref, k_ref, v_ref, o_ref, tile_ref, band_ref,
             lg0_ref, lg1_ref, m0_ref, m1_ref, a0_ref, a1_ref, *, rows, nbatch):
    lg_ref, m_ref, a_ref = (lg0_ref, lg1_ref), (m0_ref, m1_ref), (a0_ref, a1_ref)
    hp = pl.program_id(0)
    b = pl.program_id(1)
    wr = min(NA_WIN_ROWS, rows)

    @pl.when(b == 0)
    def _build_bias():
        c = lax.broadcasted_iota(jnp.int32, (GRID_W, LANES), 0)
        kc = lax.broadcasted_iota(jnp.int32, (GRID_W, LANES), 1) & (GRID_W - 1)
        cs = jnp.clip(c - NA_WIN_COLS // 2, 0, GRID_W - NA_WIN_COLS)
        valid = (kc >= cs) & (kc < cs + NA_WIN_COLS)
        dc_lane = lax.broadcasted_iota(jnp.int32, (8, LANES), 1) & (GRID_W - 1)

        def tile_step(t, carry):
            hl = t // RPB_ROWS
            dr = t - hl * RPB_ROWS
            base = ((hp * 2 + hl) * RPB_ROWS + dr) * RPB_COLS
            vec = jnp.zeros((8, LANES), F32)
            for dc in range(RPB_COLS):
                vec = jnp.where(dc_lane == dc, rpb_ref[base + dc], vec)
            rows8 = jnp.concatenate([vec] * (GRID_W // 8), axis=0)
            tile = pltpu.roll(rows8, LANES - (NA_WIN_COLS - 1), 1, stride=1, stride_axis=0)
            tile_ref[hl, dr] = jnp.where(valid, tile, NEG_BIG)
            return carry

        lax.fori_loop(0, 2 * RPB_ROWS, tile_step, 0, unroll=True)
        low = lax.broadcasted_iota(jnp.int32, (GRID_W, LANES), 1) < GRID_W
        for hl in range(2):
            for dr0 in range(NA_WIN_ROWS):
                for j in range(wr // 2):
                    band_ref[dr0, hl * GRID_W:(hl + 1) * GRID_W, j * LANES:(j + 1) * LANES] = (
                        jnp.where(low, tile_ref[hl, dr0 + 2 * j], tile_ref[hl, dr0 + 2 * j + 1]))

    lane = lax.broadcasted_iota(jnp.int32, (GRID_W, LANES), 1)
    low = lane < NA_HEAD_DIM

    items_per_batch = rows // NA_ITEM_ROWS
    ipb_shift = items_per_batch.bit_length() - 1

    def locate(item, rr):
        if isinstance(item, int):
            bl, it = divmod(item, items_per_batch)
        else:
            bl, it = lax.shift_right_logical(item, ipb_shift), item & (items_per_batch - 1)
        return it * NA_ITEM_ROWS + rr, bl * (rows * GRID_W)

    def band_rows(r, base):
        if isinstance(r, int):
            rs = min(max(r - wr // 2, 0), rows - wr)
        else:
            rs = jnp.clip(r - wr // 2, 0, rows - wr)
        return rs, pl.ds(_aligned(base + rs * GRID_W, GRID_W), wr * GRID_W)

    def query_rows(r, base):
        return pl.ds(_aligned(base + r * GRID_W, GRID_W), GRID_W)

    def logits_stage(item, slot):
        for rr in range(NA_ITEM_ROWS):
            r, base = locate(item, rr)
            rs, keys = band_rows(r, base)
            q2 = q_ref[query_rows(r, base), :]
            zero = jnp.zeros_like(q2)
            q_st = jnp.concatenate([jnp.where(low, q2, zero), jnp.where(low, zero, q2)], axis=0)
            lg = _dot_nt(q_st, k_ref[keys, :]) + band_ref[rs - r + (NA_WIN_ROWS - 1)]
            lg_ref[slot][rr] = lg
            lane_max = lg[:, :LANES]
            for c in range(1, NA_BAND_KEYS // LANES):
                lane_max = jnp.maximum(lane_max, lg[:, c * LANES:(c + 1) * LANES])
            m_ref[slot][rr] = lane_max

    def value_stage(item, slot):
        for rr in range(NA_ITEM_ROWS):
            r, base = locate(item, rr)
            _, keys = band_rows(r, base)
            row_max = jnp.max(m_ref[slot][rr], axis=-1, keepdims=True)
            e = jnp.exp(lg_ref[slot][rr] - row_max)
            lane_sum = e[:, :LANES]
            for c in range(1, NA_BAND_KEYS // LANES):
                lane_sum = lane_sum + e[:, c * LANES:(c + 1) * LANES]
            a_ref[slot][rr, :, :LANES] = _dot(e.astype(BF16), v_ref[keys, :])
            a_ref[slot][rr, :, LANES:] = lane_sum

    def finish_stage(item, slot):
        for rr in range(NA_ITEM_ROWS):
            r, base = locate(item, rr)
            acc = a_ref[slot][rr]
            o_st = acc[:, :LANES] * (1.0 / jnp.sum(acc[:, LANES:], axis=-1, keepdims=True))
            o = jnp.where(low, o_st[:GRID_W], o_st[GRID_W:])
            o_ref[query_rows(r, base), :] = o.astype(BF16)

    _software_pipeline(nbatch * items_per_batch, (logits_stage, value_stage, finish_stage),
                       order=(2, 0, 1))


def _na(proj, rpb_flat, batch, seq):
    rows = seq // GRID_W
    items_per_batch = rows // NA_ITEM_ROWS
    nbatch = math.gcd(batch, NA_BATCH_CHUNK)
    assert rows >= NA_WIN_ROWS and NA_WIN_ROWS % 2 == 0 and rows % NA_ITEM_ROWS == 0
    assert items_per_batch & (items_per_batch - 1) == 0
    pairs = NA_HEADS // 2
    blk = lambda off: pl.BlockSpec((nbatch * seq, LANES), lambda hp, b, off=off: (b, off + hp))
    return pl.pallas_call(
        functools.partial(_na_body, rows=rows, nbatch=nbatch),
        grid=(pairs, batch // nbatch),
        in_specs=[
            pl.BlockSpec(memory_space=pltpu.SMEM),
            blk(O_NQ // LANES), blk(O_NK // LANES), blk(O_NV // LANES),
        ],
        out_specs=pl.BlockSpec((nbatch * seq, LANES), lambda hp, b: (b, hp)),
        out_shape=jax.ShapeDtypeStruct((batch * seq, NA_WIDTH), BF16),
        scratch_shapes=[
            pltpu.VMEM((2, RPB_ROWS, GRID_W, LANES), F32),
            pltpu.VMEM((NA_WIN_ROWS, 2 * GRID_W, NA_BAND_KEYS), F32),
            pltpu.VMEM((NA_ITEM_ROWS, 2 * GRID_W, NA_BAND_KEYS), F32),
            pltpu.VMEM((NA_ITEM_ROWS, 2 * GRID_W, NA_BAND_KEYS), F32),
            pltpu.VMEM((NA_ITEM_ROWS, 2 * GRID_W, LANES), F32),
            pltpu.VMEM((NA_ITEM_ROWS, 2 * GRID_W, LANES), F32),
            pltpu.VMEM((NA_ITEM_ROWS, 2 * GRID_W, 2 * LANES), F32),
            pltpu.VMEM((NA_ITEM_ROWS, 2 * GRID_W, 2 * LANES), F32),
        ],
        compiler_params=pltpu.CompilerParams(
            dimension_semantics=("arbitrary", "arbitrary"), vmem_limit_bytes=V7X_VMEM_LIMIT_BYTES),
        name="na_attn",
    )(rpb_flat, proj, proj, proj)


def _diff_body(slope_ref, lq1_ref, lk1_ref, lq2_ref, lk2_ref, sub_ref, q_ref, k_ref, v_ref, o_ref,
               dist_ref, v1_ref, kt_ref, s0_ref, s1_ref, m0_ref, m1_ref, a0_ref, a1_ref,
               *, lam_init, seq, nbatch):
    s_ref, m_ref, a_ref = (s0_ref, s1_ref), (m0_ref, m1_ref), (a0_ref, a1_ref)
    h = pl.program_id(0)
    b = pl.program_id(1)
    nq = seq // DIFF_TQ

    @pl.when(b == 0)
    def _build_dist():
        slope = slope_ref[h]
        rel = (lax.broadcasted_iota(jnp.int32, (DIFF_TQ, DIFF_TQ), 0)
               - lax.broadcasted_iota(jnp.int32, (DIFF_TQ, DIFF_TQ), 1))
        for d in range(2 * nq - 1):
            off = (d - (nq - 1)) * DIFF_TQ
            dist_ref[d] = slope * jnp.abs((rel + off).astype(F32))

    lam = (jnp.exp(jnp.sum(lq1_ref[...] * lk1_ref[...], axis=-1, keepdims=True))
           - jnp.exp(jnp.sum(lq2_ref[...] * lk2_ref[...], axis=-1, keepdims=True))
           + lam_init)
    lane = lax.broadcasted_iota(jnp.int32, (DIFF_TQ, LANES), 1)
    first_map = lane < DIFF_QK_DIM
    nq_shift = nq.bit_length() - 1

    def q_rows(j):
        return pl.ds(_aligned(j * DIFF_TQ, DIFF_TQ), DIFF_TQ)

    def batch_rows(j):
        bl = j // nq if isinstance(j, int) else lax.shift_right_logical(j, nq_shift)
        return pl.ds(_aligned(bl * seq, seq), seq)

    def logits_stage(j, slot):
        i = j % nq if isinstance(j, int) else j & (nq - 1)
        q = q_ref[q_rows(j), :]
        zero = jnp.zeros_like(q)
        k_t = kt_ref[j // nq if isinstance(j, int) else lax.shift_right_logical(j, nq_shift)]
        dist = jnp.concatenate([dist_ref[i - c + (nq - 1)] for c in range(nq)], axis=1)
        for m, qm in enumerate((jnp.where(first_map, q, zero), jnp.where(first_map, zero, q))):
            s = _dot(qm, k_t) - dist
            s_ref[slot][m] = s
            lane_max = s[:, :LANES]
            for c in range(1, seq // LANES):
                lane_max = jnp.maximum(lane_max, s[:, c * LANES:(c + 1) * LANES])
            m_ref[slot][m] = lane_max

    def value_stage(j, slot):
        e = jnp.concatenate(
            [jnp.exp(s_ref[slot][m] - jnp.max(m_ref[slot][m], axis=-1, keepdims=True)).astype(BF16)
             for m in range(2)], axis=0)
        a_ref[slot][...] = _dot(e, v1_ref[batch_rows(j), :])

    def finish_stage(j, slot):
        acc = a_ref[slot][...]
        num1, den1 = acc[:DIFF_TQ, :DIFF_V_DIM], acc[:DIFF_TQ, DIFF_V_DIM:DIFF_V_DIM + 1]
        num2, den2 = acc[DIFF_TQ:, :DIFF_V_DIM], acc[DIFF_TQ:, DIFF_V_DIM:DIFF_V_DIM + 1]
        o = num1 * (1.0 / den1) - (lam * (1.0 / den2)) * num2
        o = _rmsnorm_f32(o, sub_ref[...]) * (1.0 - lam_init)
        o_ref[q_rows(j), :] = o.astype(BF16)

    v = v_ref[...]
    v1_ref[:, :DIFF_V_DIM] = v
    v1_ref[:, DIFF_V_DIM:] = jnp.where(
        lax.broadcasted_iota(jnp.int32, v.shape, 1) == 0, 1.0, 0.0).astype(BF16)
    for bl in range(nbatch):
        kt_ref[bl] = k_ref[bl * seq:(bl + 1) * seq, :].T
    _software_pipeline(nbatch * nq, (logits_stage, value_stage, finish_stage), order=(2, 0, 1))


def _diff(proj, slopes, lq1, lk1, lq2, lk2, subln, batch, seq, lam_init):
    nq = seq // DIFF_TQ
    nbatch = math.gcd(batch, DIFF_BATCH_CHUNK)
    assert nq & (nq - 1) == 0
    vec = lambda w: pl.BlockSpec((1, w), lambda h, b: (0, 0))
    blk = lambda off: pl.BlockSpec((nbatch * seq, LANES), lambda h, b, off=off: (b, off + h))
    return pl.pallas_call(
        functools.partial(_diff_body, lam_init=lam_init, seq=seq, nbatch=nbatch),
        grid=(DIFF_HEADS, batch // nbatch),
        in_specs=[
            pl.BlockSpec(memory_space=pltpu.SMEM),
            vec(DIFF_QK_DIM), vec(DIFF_QK_DIM), vec(DIFF_QK_DIM), vec(DIFF_QK_DIM), vec(DIFF_V_DIM),
            blk(O_DQ // LANES), blk(O_DK // LANES), blk(O_DV // LANES),
        ],
        out_specs=pl.BlockSpec((nbatch * seq, LANES), lambda h, b: (b, h)),
        out_shape=jax.ShapeDtypeStruct((batch * seq, DIFF_V_WIDTH), BF16),
        scratch_shapes=[
            pltpu.VMEM((2 * nq - 1, DIFF_TQ, DIFF_TQ), F32),
            pltpu.VMEM((nbatch * seq, 2 * LANES), BF16),
            pltpu.VMEM((nbatch, LANES, seq), BF16),
            pltpu.VMEM((2, DIFF_TQ, seq), F32), pltpu.VMEM((2, DIFF_TQ, seq), F32),
            pltpu.VMEM((2, DIFF_TQ, LANES), F32), pltpu.VMEM((2, DIFF_TQ, LANES), F32),
            pltpu.VMEM((2 * DIFF_TQ, 2 * LANES), F32), pltpu.VMEM((2 * DIFF_TQ, 2 * LANES), F32),
        ],
        compiler_params=pltpu.CompilerParams(
            dimension_semantics=("arbitrary", "arbitrary"),
            vmem_limit_bytes=V7X_VMEM_LIMIT_BYTES),
        name="diff_attn",
    )(slopes, lq1, lk1, lq2, lk2, subln, proj, proj, proj)


def _merge_ffn_body(x_ref, mq_ref, ona_ref, odf_ref, mk_ref, mv_ref, g_ref, wgate_hbm, bgate_ref,
                    wna_hbm, wdf_hbm, wmem_hbm, wout_hbm, fg_ref, wg_hbm, wu_hbm, wd_hbm, ng_ref,
                    o_ref, wgate_ref, wna_ref, wdf_ref, wmem_ref, wout_ref, wg_ref, wu_ref, wd_ref,
                    stage_ref, sem_ref, *, final_norm):
    @pl.when(pl.program_id(0) == 0)
    def _load_weights():
        _load_weights_bf16(
            ((wgate_hbm, wgate_ref), (wna_hbm, wna_ref), (wdf_hbm, wdf_ref), (wmem_hbm, wmem_ref),
             (wout_hbm, wout_ref), (wg_hbm, wg_ref), (wu_hbm, wu_ref), (wd_hbm, wd_ref)),
            stage_ref, sem_ref)

    x = x_ref[...]
    h = _rmsnorm_f32(x, g_ref[...]).astype(BF16)
    heads = []
    for hh in range(MEM_HEADS):
        sl = slice(hh * MEM_HEAD_DIM, (hh + 1) * MEM_HEAD_DIM)
        logits = _dot_nt(mq_ref[:, sl], mk_ref[:, sl]) * (MEM_HEAD_DIM ** -0.5)
        p = _softmax_rows(logits).astype(BF16)
        heads.append(_dot(p, mv_ref[:, sl]))
    o_mem = jnp.concatenate(heads, axis=-1).astype(BF16)
    branches = (
        _dot(ona_ref[...], wna_ref[...]),
        _dot(odf_ref[...], wdf_ref[...]),
        _dot(o_mem, wmem_ref[...]),
    )
    merged = jnp.zeros(x.shape, F32)
    for i, y in enumerate(branches):
        sl = slice(i * D_MODEL, (i + 1) * D_MODEL)
        gate = jax.nn.sigmoid(_dot(h, wgate_ref[:, sl]) + bgate_ref[:, sl])
        merged = merged + gate * y
    x2 = x + _dot(merged.astype(BF16), wout_ref[...])
    y = _swiglu_half_step(x2, fg_ref, wg_ref, wu_ref, wd_ref)
    if final_norm:
        y = _rmsnorm_f32(y, ng_ref[...])
    o_ref[...] = y


def _merge_ffn(x1, proj, o_na, o_diff, mk, mv, norm_g, w_gate, b_gate, w_na, w_df, w_mem, w_out,
               ffn_g, wg, wu, wd, final_g, seq, m_tokens, *, final_norm):
    n = x1.shape[0]
    per_b = seq // MERGE_TM
    row = lambda i: (i, 0)
    return pl.pallas_call(
        functools.partial(_merge_ffn_body, final_norm=final_norm),
        grid=(n // MERGE_TM,),
        in_specs=[
            pl.BlockSpec((MERGE_TM, D_MODEL), row),
            pl.BlockSpec((MERGE_TM, MEM_WIDTH), lambda i: (i, O_MQ // MEM_WIDTH)),
            pl.BlockSpec((MERGE_TM, NA_WIDTH), row),
            pl.BlockSpec((MERGE_TM, DIFF_V_WIDTH), row),
            pl.BlockSpec((m_tokens, MEM_WIDTH), lambda i: (i // per_b, 0)),
            pl.BlockSpec((m_tokens, MEM_WIDTH), lambda i: (i // per_b, 0)),
            _resident((1, D_MODEL)),
            _HBM,
            _resident((1, 3 * D_MODEL)),
            _HBM, _HBM, _HBM, _HBM,
            _resident((1, D_MODEL)),
            _HBM, _HBM, _HBM,
            _resident((1, D_MODEL)),
        ],
        out_specs=pl.BlockSpec((MERGE_TM, D_MODEL), row),
        out_shape=jax.ShapeDtypeStruct((n, D_MODEL), F32),
        scratch_shapes=[
            pltpu.VMEM((D_MODEL, 3 * D_MODEL), BF16),
            pltpu.VMEM((NA_WIDTH, D_MODEL), BF16), pltpu.VMEM((DIFF_V_WIDTH, D_MODEL), BF16),
            pltpu.VMEM((MEM_WIDTH, D_MODEL), BF16), pltpu.VMEM((D_MODEL, D_MODEL), BF16),
            pltpu.VMEM((D_MODEL, D_FF), BF16), pltpu.VMEM((D_MODEL, D_FF), BF16),
            pltpu.VMEM((D_FF, D_MODEL), BF16),
            pltpu.VMEM((STAGE_SLOTS, STAGE_ROWS,3 * D_MODEL), F32),
            pltpu.SemaphoreType.DMA((STAGE_SLOTS,)),
        ],
        compiler_params=pltpu.CompilerParams(
            dimension_semantics=("arbitrary",), vmem_limit_bytes=V7X_VMEM_LIMIT_BYTES),
        name="merge_ffn",
    )(x1, proj, o_na, o_diff, mk, mv, norm_g, w_gate, b_gate, w_na, w_df, w_mem, w_out,
      ffn_g, wg, wu, wd, final_g)


def kernel(x, mem, ffn1_norm, ffn1_w_gate, ffn1_w_up, ffn1_w_down, mix_norm, w_in, na_rpb,
           diff_lambda_q1, diff_lambda_k1, diff_lambda_q2, diff_lambda_k2, diff_subln,
           mem_norm, w_mem_kv, w_gate, b_gate, w_br_na, w_br_diff, w_br_mem, w_out,
           ffn2_norm, ffn2_w_gate, ffn2_w_up, ffn2_w_down, final_norm):
    batch, seq, d_model = x.shape
    m_tokens = mem.shape[1]
    depth = ffn1_norm.shape[0]
    assert d_model == D_MODEL and seq % GRID_W == 0
    assert seq % DIFF_TQ == 0 and seq % MERGE_TM == 0 and (batch * seq) % FFN_TM == 0
    slopes = jnp.asarray([2.0 ** (-8.0 * (i + 1) / DIFF_HEADS) for i in range(DIFF_HEADS)], F32)
    w32 = lambda w: w.astype(F32)
    vec = lambda v: v.reshape(1, -1).astype(F32)

    xt = x.reshape(batch * seq, d_model)
    mem2d = mem.reshape(batch * m_tokens, d_model)
    for l in range(depth):
        lam_init = 0.8 - 0.6 * math.exp(-0.3 * l)
        xt, proj = _ffn_proj(xt, vec(ffn1_norm[l]), w32(ffn1_w_gate[l]), w32(ffn1_w_up[l]),
                             w32(ffn1_w_down[l]), vec(mix_norm[l]), w32(w_in[l]))
        mk, mv = _memkv(mem2d, vec(mem_norm[l]), w32(w_mem_kv[l]))
        o_na = _na(proj, na_rpb[l].reshape(-1).astype(F32), batch, seq)
        o_diff = _diff(proj, slopes, vec(diff_lambda_q1[l]), vec(diff_lambda_k1[l]),
                       vec(diff_lambda_q2[l]), vec(diff_lambda_k2[l]), vec(diff_subln[l]),
                       batch, seq, lam_init)
        xt = _merge_ffn(xt, proj, o_na, o_diff, mk, mv, vec(mix_norm[l]), w32(w_gate[l]),
                        vec(b_gate[l]), w32(w_br_na[l]), w32(w_br_diff[l]), w32(w_br_mem[l]),
                        w32(w_out[l]), vec(ffn2_norm[l]), w32(ffn2_w_gate[l]), w32(ffn2_w_up[l]),
                        w32(ffn2_w_down[l]), vec(final_norm), seq, m_tokens,
                        final_norm=(l == depth - 1))
    return xt.reshape(batch, seq, d_model)
```

```python
import functools
import math

import jax
import jax.numpy as jnp
from jax import lax
from jax.experimental import pallas as pl
from jax.experimental.pallas import tpu as pltpu

F32 = jnp.float32
BF16 = jnp.bfloat16

D_MODEL = 1024
GRID_W = 64
NA_HEADS = 8
NA_HEAD_DIM = 64
NA_WIN_ROWS = 8
NA_WIN_COLS = 16
DIFF_HEADS = 4
DIFF_QK_DIM = 64
DIFF_V_DIM = 128
MEM_HEADS = 4
MEM_HEAD_DIM = 128
D_FF = 2816
NORM_EPS = 1e-6
NA_WIDTH = NA_HEADS * NA_HEAD_DIM
DIFF_QK_WIDTH = DIFF_HEADS * 2 * DIFF_QK_DIM
DIFF_V_WIDTH = DIFF_HEADS * DIFF_V_DIM
MEM_WIDTH = MEM_HEADS * MEM_HEAD_DIM
IN_WIDTH = 3 * NA_WIDTH + 2 * DIFF_QK_WIDTH + DIFF_V_WIDTH + MEM_WIDTH
O_NQ = 0
O_NK = O_NQ + NA_WIDTH
O_NV = O_NK + NA_WIDTH
O_DQ = O_NV + NA_WIDTH
O_DK = O_DQ + DIFF_QK_WIDTH
O_DV = O_DK + DIFF_QK_WIDTH
O_MQ = O_DV + DIFF_V_WIDTH

LANES = 128
V7X_VMEM_LIMIT_BYTES = 56 * 1024 * 1024

FFN_TM = 512
STAGE_ROWS = 128
STAGE_SLOTS = 4
FFN_TF = 256
PROJ_TN = 512
DIFF_TQ = 512
DIFF_BATCH_CHUNK = 4
MERGE_TM = 512
MEMKV_TM = 1024
NEG_BIG = -1e30


def _rmsnorm_f32(x, g):
    ms = jnp.mean(x * x, axis=-1, keepdims=True)
    return (x * lax.rsqrt(ms + NORM_EPS)) * g


def _softmax_rows(logits):
    m = jnp.max(logits, axis=-1, keepdims=True)
    e = jnp.exp(logits - m)
    s = jnp.sum(e, axis=-1, keepdims=True)
    return e * (1.0 / s)


def _dot_nt(a, b):
    return lax.dot_general(a, b, (((1,), (1,)), ((), ())), preferred_element_type=F32)


def _dot(a, b):
    return jnp.dot(a, b, preferred_element_type=F32)


def _resident(shape):
    nd = len(shape)
    return pl.BlockSpec(shape, lambda *_: (0,) * nd, pipeline_mode=pl.Buffered(1))


def _aligned(idx, multiple):
    return idx if isinstance(idx, int) else pl.multiple_of(idx, multiple)


def _software_pipeline(n_items, stages, order):
    depth = len(stages)
    assert n_items >= depth and sorted(order) == list(range(depth))

    def trip(t, parity, valid):
        for k in order:
            if valid(k):
                stages[k](t - k, (parity - k) % 2)

    for t in range(depth - 1):
        trip(t, t % 2, lambda k, t=t: k <= t)
    start = depth - 1
    if (n_items - start) % 2:
        trip(start, start % 2, lambda k: True)
        start += 1

    def body(u, carry):
        t = start + 2 * u
        trip(t, start % 2, lambda k: True)
        trip(t + 1, (start + 1) % 2, lambda k: True)
        return carry

    lax.fori_loop(0, (n_items - start) // 2, body, 0)
    for t in range(n_items, n_items + depth - 1):
        trip(t, t % 2, lambda k, t=t: t - k < n_items)


def _stage_copy(w_hbm, stage_ref, sem_ref, chunk, slot):
    width = w_hbm.shape[1]
    return pltpu.make_async_copy(
        w_hbm.at[pl.ds(chunk * STAGE_ROWS, STAGE_ROWS), :],
        stage_ref.at[slot, :, :width],
        sem_ref.at[slot])


def _load_weights_bf16(pairs, stage_ref, sem_ref):
    jobs = [(w, dst, c) for w, dst in pairs for c in range(w.shape[0] // STAGE_ROWS)]
    ahead = STAGE_SLOTS - 1
    for n in range(min(ahead, len(jobs))):
        _stage_copy(jobs[n][0], stage_ref, sem_ref, jobs[n][2], n % STAGE_SLOTS).start()
    for n, (w, dst, c) in enumerate(jobs):
        slot = n % STAGE_SLOTS
        if n + ahead < len(jobs):
            nw, _, nc = jobs[n + ahead]
            _stage_copy(nw, stage_ref, sem_ref, nc, (n + ahead) % STAGE_SLOTS).start()
        _stage_copy(w, stage_ref, sem_ref, c, slot).wait()
        dst[c * STAGE_ROWS:(c + 1) * STAGE_ROWS, :] = (
            stage_ref[slot, :, :w.shape[1]].astype(BF16))


_HBM = pl.BlockSpec(memory_space=pl.ANY)


def _swiglu_half_step(x, g_ref, wg_ref, wu_ref, wd_ref):
    h = _rmsnorm_f32(x, g_ref[...]).astype(BF16)
    acc = jnp.zeros(x.shape, F32)
    for c in range(D_FF // FFN_TF):
        sl = slice(c * FFN_TF, (c + 1) * FFN_TF)
        gate = _dot(h, wg_ref[:, sl])
        up = _dot(h, wu_ref[:, sl])
        act = (gate * jax.nn.sigmoid(gate) * up).astype(BF16)
        acc = acc + _dot(act, wd_ref[sl, :])
    return x + 0.5 * acc


def _ffn_proj_body(x_ref, g_ref, wg_hbm, wu_hbm, wd_hbm, pg_ref, win_hbm, x1_ref, proj_ref,
                   wg_ref, wu_ref, wd_ref, win_ref, stage_ref, sem_ref):
    @pl.when(pl.program_id(0) == 0)
    def _load_weights():
        _load_weights_bf16(((wg_hbm, wg_ref), (wu_hbm, wu_ref), (wd_hbm, wd_ref),
                            (win_hbm, win_ref)), stage_ref, sem_ref)

    x1 = _swiglu_half_step(x_ref[...], g_ref, wg_ref, wu_ref, wd_ref)
    x1_ref[...] = x1
    h = _rmsnorm_f32(x1, pg_ref[...]).astype(BF16)
    for c in range(IN_WIDTH // PROJ_TN):
        lo = c * PROJ_TN
        y = _dot(h, win_ref[:, lo:lo + PROJ_TN])
        if lo == O_NQ or lo == O_DQ:
            y = y * (NA_HEAD_DIM ** -0.5)
        proj_ref[:, lo:lo + PROJ_TN] = y.astype(BF16)


def _ffn_proj(x, norm_g, wg, wu, wd, mix_g, w_in):
    n = x.shape[0]
    row = lambda i: (i, 0)
    return pl.pallas_call(
        _ffn_proj_body,
        grid=(n // FFN_TM,),
        in_specs=[
            pl.BlockSpec((FFN_TM, D_MODEL), row),
            _resident((1, D_MODEL)),
            _HBM, _HBM, _HBM,
            _resident((1, D_MODEL)),
            _HBM,
        ],
        out_specs=[pl.BlockSpec((FFN_TM, D_MODEL), row), pl.BlockSpec((FFN_TM, IN_WIDTH), row)],
        out_shape=[jax.ShapeDtypeStruct((n, D_MODEL), F32),
                   jax.ShapeDtypeStruct((n, IN_WIDTH), BF16)],
        scratch_shapes=[
            pltpu.VMEM((D_MODEL, D_FF), BF16), pltpu.VMEM((D_MODEL, D_FF), BF16),
            pltpu.VMEM((D_FF, D_MODEL), BF16), pltpu.VMEM((D_MODEL, IN_WIDTH), BF16),
            pltpu.VMEM((STAGE_SLOTS, STAGE_ROWS,IN_WIDTH), F32),
            pltpu.SemaphoreType.DMA((STAGE_SLOTS,)),
        ],
        compiler_params=pltpu.CompilerParams(
            dimension_semantics=("arbitrary",), vmem_limit_bytes=V7X_VMEM_LIMIT_BYTES),
        name="ffn_proj",
    )(x, norm_g, wg, wu, wd, mix_g, w_in)


def _memkv_body(m_ref, g_ref, w_hbm, k_ref, v_ref, w_ref, stage_ref, sem_ref):
    @pl.when(pl.program_id(0) == 0)
    def _load_weights():
        _load_weights_bf16(((w_hbm, w_ref),), stage_ref, sem_ref)

    h = _rmsnorm_f32(m_ref[...], g_ref[...]).astype(BF16)
    k_ref[...] = _dot(h, w_ref[:, :MEM_WIDTH]).astype(BF16)
    v_ref[...] = _dot(h, w_ref[:, MEM_WIDTH:]).astype(BF16)


def _memkv(mem2d, norm_g, w_kv):
    n = mem2d.shape[0]
    tm = math.gcd(n, MEMKV_TM)
    row = lambda i: (i, 0)
    shp = jax.ShapeDtypeStruct((n, MEM_WIDTH), BF16)
    return pl.pallas_call(
        _memkv_body,
        grid=(n // tm,),
        in_specs=[
            pl.BlockSpec((tm, D_MODEL), row),
            _resident((1, D_MODEL)),
            _HBM,
        ],
        out_specs=[pl.BlockSpec((tm, MEM_WIDTH), row)] * 2,
        out_shape=[shp, shp],
        scratch_shapes=[
            pltpu.VMEM((D_MODEL, 2 * MEM_WIDTH), BF16),
            pltpu.VMEM((STAGE_SLOTS, STAGE_ROWS,2 * MEM_WIDTH), F32),
            pltpu.SemaphoreType.DMA((STAGE_SLOTS,)),
        ],
        compiler_params=pltpu.CompilerParams(dimension_semantics=("arbitrary",)),
        name="memkv",
    )(mem2d, norm_g, w_kv)


RPB_ROWS = 2 * NA_WIN_ROWS - 1
RPB_COLS = 2 * NA_WIN_COLS - 1
NA_BAND_KEYS = NA_WIN_ROWS * GRID_W
NA_BATCH_CHUNK = 4
NA_ITEM_ROWS = 8


def _na_body(rpb_ref, q_ref, k_ref, v_ref, o_ref, tile_ref, band_ref,
             lg0_ref, lg1_ref, m0_ref, m1_ref, a0_ref, a1_ref, *, rows, nbatch):
    lg_ref, m_ref, a_ref = (lg0_ref, lg1_ref), (m0_ref, m1_ref), (a0_ref, a1_ref)
    hp = pl.program_id(0)
    b = pl.program_id(1)
    wr = min(NA_WIN_ROWS, rows)

    @pl.when(b == 0)
    def _build_bias():
        c = lax.broadcasted_iota(jnp.int32, (GRID_W, LANES), 0)
        kc = lax.broadcasted_iota(jnp.int32, (GRID_W, LANES), 1) & (GRID_W - 1)
        cs = jnp.clip(c - NA_WIN_COLS // 2, 0, GRID_W - NA_WIN_COLS)
        valid = (kc >= cs) & (kc < cs + NA_WIN_COLS)
        dc_lane = lax.broadcasted_iota(jnp.int32, (8, LANES), 1) & (GRID_W - 1)

        def tile_step(t, carry):
            hl = t // RPB_ROWS
            dr = t - hl * RPB_ROWS
            base = ((hp * 2 + hl) * RPB_ROWS + dr) * RPB_COLS
            vec = jnp.zeros((8, LANES), F32)
            for dc in range(RPB_COLS):
                vec = jnp.where(dc_lane == dc, rpb_ref[base + dc], vec)
            rows8 = jnp.concatenate([vec] * (GRID_W // 8), axis=0)
            tile = pltpu.roll(rows8, LANES - (NA_WIN_COLS - 1), 1, stride=1, stride_axis=0)
            tile_ref[hl, dr] = jnp.where(valid, tile, NEG_BIG)
            return carry

        lax.fori_loop(0, 2 * RPB_ROWS, tile_step, 0, unroll=True)
        low = lax.broadcasted_iota(jnp.int32, (GRID_W, LANES), 1) < GRID_W
        for hl in range(2):
            for dr0 in range(NA_WIN_ROWS):
                for j in range(wr // 2):
                    band_ref[dr0, hl * GRID_W:(hl + 1) * GRID_W, j * LANES:(j + 1) * LANES] = (
                        jnp.where(low, tile_ref[hl, dr0 + 2 * j], tile_ref[hl, dr0 + 2 * j + 1]))

    lane = lax.broadcasted_iota(jnp.int32, (GRID_W, LANES), 1)
    low = lane < NA_HEAD_DIM

    items_per_batch = rows // NA_ITEM_ROWS
    ipb_shift = items_per_batch.bit_length() - 1

    def locate(item, rr):
        if isinstance(item, int):
            bl, it = divmod(item, items_per_batch)
        else:
            bl, it = lax.shift_right_logical(item, ipb_shift), item & (items_per_batch - 1)
        return it * NA_ITEM_ROWS + rr, bl * (rows * GRID_W)

    def band_rows(r, base):
        if isinstance(r, int):
            rs = min(max(r - wr // 2, 0), rows - wr)
        else:
            rs = jnp.clip(r - wr // 2, 0, rows - wr)
        return rs, pl.ds(_aligned(base + rs * GRID_W, GRID_W), wr * GRID_W)

    def query_rows(r, base):
        return pl.ds(_aligned(base + r * GRID_W, GRID_W), GRID_W)

    def logits_stage(item, slot):
        for rr in range(NA_ITEM_ROWS):
            r, base = locate(item, rr)
            rs, keys = band_rows(r, base)
            q2 = q_ref[query_rows(r, base), :]
            zero = jnp.zeros_like(q2)
            q_st = jnp.concatenate([jnp.where(low, q2, zero), jnp.where(low, zero, q2)], axis=0)
            lg = _dot_nt(q_st, k_ref[keys, :]) + band_ref[rs - r + (NA_WIN_ROWS - 1)]
            lg_ref[slot][rr] = lg
            lane_max = lg[:, :LANES]
            for c in range(1, NA_BAND_KEYS // LANES):
                lane_max = jnp.maximum(lane_max, lg[:, c * LANES:(c + 1) * LANES])
            m_ref[slot][rr] = lane_max

    def value_stage(item, slot):
        for rr in range(NA_ITEM_ROWS):
            r, base = locate(item, rr)
            _, keys = band_rows(r, base)
            row_max = jnp.max(m_ref[slot][rr], axis=-1, keepdims=True)
            e = jnp.exp(lg_ref[slot][rr] - row_max)
            lane_sum = e[:, :LANES]
            for c in range(1, NA_BAND_KEYS // LANES):
                lane_sum = lane_sum + e[:, c * LANES:(c + 1) * LANES]
            a_ref[slot][rr, :, :LANES] = _dot(e.astype(BF16), v_ref[keys, :])
            a_ref[slot][rr, :, LANES:] = lane_sum

    def finish_stage(item, slot):
        for rr in range(NA_ITEM_ROWS):
            r, base = locate(item, rr)
            acc = a_ref[slot][rr]
            o_st = acc[:, :LANES] * (1.0 / jnp.sum(acc[:, LANES:], axis=-1, keepdims=True))
            o = jnp.where(low, o_st[:GRID_W], o_st[GRID_W:])
            o_ref[query_rows(r, base), :] = o.astype(BF16)

    _software_pipeline(nbatch * items_per_batch, (logits_stage, value_stage, finish_stage),
                       order=(1, 0, 2))


def _na(proj, rpb_flat, batch, seq):
    rows = seq // GRID_W
    items_per_batch = rows // NA_ITEM_ROWS
    nbatch = math.gcd(batch, NA_BATCH_CHUNK)
    assert rows >= NA_WIN_ROWS and NA_WIN_ROWS % 2 == 0 and rows % NA_ITEM_ROWS == 0
    assert items_per_batch & (items_per_batch - 1) == 0
    pairs = NA_HEADS // 2
    blk = lambda off: pl.BlockSpec((nbatch * seq, LANES), lambda hp, b, off=off: (b, off + hp))
    return pl.pallas_call(
        functools.partial(_na_body, rows=rows, nbatch=nbatch),
        grid=(pairs, batch // nbatch),
        in_specs=[
            pl.BlockSpec(memory_space=pltpu.SMEM),
            blk(O_NQ // LANES), blk(O_NK // LANES), blk(O_NV // LANES),
        ],
        out_specs=pl.BlockSpec((nbatch * seq, LANES), lambda hp, b: (b, hp)),
        out_shape=jax.ShapeDtypeStruct((batch * seq, NA_WIDTH), BF16),
        scratch_shapes=[
            pltpu.VMEM((2, RPB_ROWS, GRID_W, LANES), F32),
            pltpu.VMEM((NA_WIN_ROWS, 2 * GRID_W, NA_BAND_KEYS), F32),
            pltpu.VMEM((NA_ITEM_ROWS, 2 * GRID_W, NA_BAND_KEYS), F32),
            pltpu.VMEM((NA_ITEM_ROWS, 2 * GRID_W, NA_BAND_KEYS), F32),
            pltpu.VMEM((NA_ITEM_ROWS, 2 * GRID_W, LANES), F32),
            pltpu.VMEM((NA_ITEM_ROWS, 2 * GRID_W, LANES), F32),
            pltpu.VMEM((NA_ITEM_ROWS, 2 * GRID_W, 2 * LANES), F32),
            pltpu.VMEM((NA_ITEM_ROWS, 2 * GRID_W, 2 * LANES), F32),
        ],
        compiler_params=pltpu.CompilerParams(
            dimension_semantics=("arbitrary", "arbitrary"), vmem_limit_bytes=V7X_VMEM_LIMIT_BYTES),
        name="na_attn",
    )(rpb_flat, proj, proj, proj)


def _diff_body(slope_ref, lq1_ref, lk1_ref, lq2_ref, lk2_ref, sub_ref, q_ref, k_ref, v_ref, o_ref,
               dist_ref, v1_ref, kt_ref, s0_ref, s1_ref, m0_ref, m1_ref, a0_ref, a1_ref,
               *, lam_init, seq, nbatch):
    s_ref, m_ref, a_ref = (s0_ref, s1_ref), (m0_ref, m1_ref), (a0_ref, a1_ref)
    h = pl.program_id(0)
    b = pl.program_id(1)
    nq = seq // DIFF_TQ

    @pl.when(b == 0)
    def _build_dist():
        slope = slope_ref[h]
        rel = (lax.broadcasted_iota(jnp.int32, (DIFF_TQ, DIFF_TQ), 0)
               - lax.broadcasted_iota(jnp.int32, (DIFF_TQ, DIFF_TQ), 1))
        for d in range(2 * nq - 1):
            off = (d - (nq - 1)) * DIFF_TQ
            dist_ref[d] = slope * jnp.abs((rel + off).astype(F32))

    lam = (jnp.exp(jnp.sum(lq1_ref[...] * lk1_ref[...], axis=-1, keepdims=True))
           - jnp.exp(jnp.sum(lq2_ref[...] * lk2_ref[...], axis=-1, keepdims=True))
           + lam_init)
    lane = lax.broadcasted_iota(jnp.int32, (DIFF_TQ, LANES), 1)
    first_map = lane < DIFF_QK_DIM
    nq_shift = nq.bit_length() - 1

    def q_rows(j):
        return pl.ds(_aligned(j * DIFF_TQ, DIFF_TQ), DIFF_TQ)

    def batch_rows(j):
        bl = j // nq if isinstance(j, int) else lax.shift_right_logical(j, nq_shift)
        return pl.ds(_aligned(bl * seq, seq), seq)

    def logits_stage(j, slot):
        i = j % nq if isinstance(j, int) else j & (nq - 1)
        q = q_ref[q_rows(j), :]
        zero = jnp.zeros_like(q)
        k_t = kt_ref[j // nq if isinstance(j, int) else lax.shift_right_logical(j, nq_shift)]
        dist = jnp.concatenate([dist_ref[i - c + (nq - 1)] for c in range(nq)], axis=1)
        for m, qm in enumerate((jnp.where(first_map, q, zero), jnp.where(first_map, zero, q))):
            s = _dot(qm, k_t) - dist
            s_ref[slot][m] = s
            lane_max = s[:, :LANES]
            for c in range(1, seq // LANES):
                lane_max = jnp.maximum(lane_max, s[:, c * LANES:(c + 1) * LANES])
            m_ref[slot][m] = lane_max

    def value_stage(j, slot):
        e = jnp.concatenate(
            [jnp.exp(s_ref[slot][m] - jnp.max(m_ref[slot][m], axis=-1, keepdims=True)).astype(BF16)
             for m in range(2)], axis=0)
        a_ref[slot][...] = _dot(e, v1_ref[batch_rows(j), :])

    def finish_stage(j, slot):
        acc = a_ref[slot][...]
        num1, den1 = acc[:DIFF_TQ, :DIFF_V_DIM], acc[:DIFF_TQ, DIFF_V_DIM:DIFF_V_DIM + 1]
        num2, den2 = acc[DIFF_TQ:, :DIFF_V_DIM], acc[DIFF_TQ:, DIFF_V_DIM:DIFF_V_DIM + 1]
        o = num1 * (1.0 / den1) - (lam * (1.0 / den2)) * num2
        o = _rmsnorm_f32(o, sub_ref[...]) * (1.0 - lam_init)
        o_ref[q_rows(j), :] = o.astype(BF16)

    v = v_ref[...]
    v1_ref[:, :DIFF_V_DIM] = v
    v1_ref[:, DIFF_V_DIM:] = jnp.where(
        lax.broadcasted_iota(jnp.int32, v.shape, 1) == 0, 1.0, 0.0).astype(BF16)
    for bl in range(nbatch):
        kt_ref[bl] = k_ref[bl * seq:(bl + 1) * seq, :].T
    _software_pipeline(nbatch * nq, (logits_stage, value_stage, finish_stage), order=(1, 0, 2))


def _diff(proj, slopes, lq1, lk1, lq2, lk2, subln, batch, seq, lam_init):
    nq = seq // DIFF_TQ
    nbatch = math.gcd(batch, DIFF_BATCH_CHUNK)
    assert nq & (nq - 1) == 0
    vec = lambda w: pl.BlockSpec((1, w), lambda h, b: (0, 0))
    blk = lambda off: pl.BlockSpec((nbatch * seq, LANES), lambda h, b, off=off: (b, off + h))
    return pl.pallas_call(
        functools.partial(_diff_body, lam_init=lam_init, seq=seq, nbatch=nbatch),
        grid=(DIFF_HEADS, batch // nbatch),
        in_specs=[
            pl.BlockSpec(memory_space=pltpu.SMEM),
            vec(DIFF_QK_DIM), vec(DIFF_QK_DIM), vec(DIFF_QK_DIM), vec(DIFF_QK_DIM), vec(DIFF_V_DIM),
            blk(O_DQ // LANES), blk(O_DK // LANES), blk(O_DV // LANES),
        ],
        out_specs=pl.BlockSpec((nbatch * seq, LANES), lambda h, b: (b, h)),
        out_shape=jax.ShapeDtypeStruct((batch * seq, DIFF_V_WIDTH), BF16),
        scratch_shapes=[
            pltpu.VMEM((2 * nq - 1, DIFF_TQ, DIFF_TQ), F32),
            pltpu.VMEM((nbatch * seq, 2 * LANES), BF16),
            pltpu.VMEM((nbatch, LANES, seq), BF16),
            pltpu.VMEM((2, DIFF_TQ, seq), F32), pltpu.VMEM((2, DIFF_TQ, seq), F32),
            pltpu.VMEM((2, DIFF_TQ, LANES), F32), pltpu.VMEM((2, DIFF_TQ, LANES), F32),
            pltpu.VMEM((2 * DIFF_TQ, 2 * LANES), F32), pltpu.VMEM((2 * DIFF_TQ, 2 * LANES), F32),
        ],
        compiler_params=pltpu.CompilerParams(
            dimension_semantics=("arbitrary", "arbitrary"),
            vmem_limit_bytes=V7X_VMEM_LIMIT_BYTES),
        name="diff_attn",
    )(slopes, lq1, lk1, lq2, lk2, subln, proj, proj, proj)


def _merge_ffn_body(x_ref, mq_ref, ona_ref, odf_ref, mk_ref, mv_ref, g_ref, wgate_hbm, bgate_ref,
                    wna_hbm, wdf_hbm, wmem_hbm, wout_hbm, fg_ref, wg_hbm, wu_hbm, wd_hbm, ng_ref,
                    o_ref, wgate_ref, wna_ref, wdf_ref, wmem_ref, wout_ref, wg_ref, wu_ref, wd_ref,
                    stage_ref, sem_ref, *, final_norm):
    @pl.when(pl.program_id(0) == 0)
    def _load_weights():
        _load_weights_bf16(
            ((wgate_hbm, wgate_ref), (wna_hbm, wna_ref), (wdf_hbm, wdf_ref), (wmem_hbm, wmem_ref),
             (wout_hbm, wout_ref), (wg_hbm, wg_ref), (wu_hbm, wu_ref), (wd_hbm, wd_ref)),
            stage_ref, sem_ref)

    x = x_ref[...]
    h = _rmsnorm_f32(x, g_ref[...]).astype(BF16)
    heads = []
    for hh in range(MEM_HEADS):
        sl = slice(hh * MEM_HEAD_DIM, (hh + 1) * MEM_HEAD_DIM)
        logits = _dot_nt(mq_ref[:, sl], mk_ref[:, sl]) * (MEM_HEAD_DIM ** -0.5)
        p = _softmax_rows(logits).astype(BF16)
        heads.append(_dot(p, mv_ref[:, sl]))
    o_mem = jnp.concatenate(heads, axis=-1).astype(BF16)
    branches = (
        _dot(ona_ref[...], wna_ref[...]),
        _dot(odf_ref[...], wdf_ref[...]),
        _dot(o_mem, wmem_ref[...]),
    )
    merged = jnp.zeros(x.shape, F32)
    for i, y in enumerate(branches):
        sl = slice(i * D_MODEL, (i + 1) * D_MODEL)
        gate = jax.nn.sigmoid(_dot(h, wgate_ref[:, sl]) + bgate_ref[:, sl])
        merged = merged + gate * y
    x2 = x + _dot(merged.astype(BF16), wout_ref[...])
    y = _swiglu_half_step(x2, fg_ref, wg_ref, wu_ref, wd_ref)
    if final_norm:
        y = _rmsnorm_f32(y, ng_ref[...])
    o_ref[...] = y


def _merge_ffn(x1, proj, o_na, o_diff, mk, mv, norm_g, w_gate, b_gate, w_na, w_df, w_mem, w_out,
               ffn_g, wg, wu, wd, final_g, seq, m_tokens, *, final_norm):
    n = x1.shape[0]
    per_b = seq // MERGE_TM
    row = lambda i: (i, 0)
    return pl.pallas_call(
        functools.partial(_merge_ffn_body, final_norm=final_norm),
        grid=(n // MERGE_TM,),
        in_specs=[
            pl.BlockSpec((MERGE_TM, D_MODEL), row),
            pl.BlockSpec((MERGE_TM, MEM_WIDTH), lambda i: (i, O_MQ // MEM_WIDTH)),
            pl.BlockSpec((MERGE_TM, NA_WIDTH), row),
            pl.BlockSpec((MERGE_TM, DIFF_V_WIDTH), row),
            pl.BlockSpec((m_tokens, MEM_WIDTH), lambda i: (i // per_b, 0)),
            pl.BlockSpec((m_tokens, MEM_WIDTH), lambda i: (i // per_b, 0)),
            _resident((1, D_MODEL)),
            _HBM,
            _resident((1, 3 * D_MODEL)),
            _HBM, _HBM, _HBM, _HBM,
            _resident((1, D_MODEL)),
            _HBM, _HBM, _HBM,
            _resident((1, D_MODEL)),
        ],
        out_specs=pl.BlockSpec((MERGE_TM, D_MODEL), row),
        out_shape=jax.ShapeDtypeStruct((n, D_MODEL), F32),
        scratch_shapes=[
            pltpu.VMEM((D_MODEL, 3 * D_MODEL), BF16),
            pltpu.VMEM((NA_WIDTH, D_MODEL), BF16), pltpu.VMEM((DIFF_V_WIDTH, D_MODEL), BF16),
            pltpu.VMEM((MEM_WIDTH, D_MODEL), BF16), pltpu.VMEM((D_MODEL, D_MODEL), BF16),
            pltpu.VMEM((D_MODEL, D_FF), BF16), pltpu.VMEM((D_MODEL, D_FF), BF16),
            pltpu.VMEM((D_FF, D_MODEL), BF16),
            pltpu.VMEM((STAGE_SLOTS, STAGE_ROWS,3 * D_MODEL), F32),
            pltpu.SemaphoreType.DMA((STAGE_SLOTS,)),
        ],
        compiler_params=pltpu.CompilerParams(
            dimension_semantics=("arbitrary",), vmem_limit_bytes=V7X_VMEM_LIMIT_BYTES),
        name="merge_ffn",
    )(x1, proj, o_na, o_diff, mk, mv, norm_g, w_gate, b_gate, w_na, w_df, w_mem, w_out,
      ffn_g, wg, wu, wd, final_g)


def kernel(x, mem, ffn1_norm, ffn1_w_gate, ffn1_w_up, ffn1_w_down, mix_norm, w_in, na_rpb,
           diff_lambda_q1, diff_lambda_k1, diff_lambda_q2, diff_lambda_k2, diff_subln,
           mem_norm, w_mem_kv, w_gate, b_gate, w_br_na, w_br_diff, w_br_mem, w_out,
           ffn2_norm, ffn2_w_gate, ffn2_w_up, ffn2_w_down, final_norm):
    batch, seq, d_model = x.shape
    m_tokens = mem.shape[1]
    depth = ffn1_norm.shape[0]
    assert d_model == D_MODEL and seq % GRID_W == 0
    assert seq % DIFF_TQ == 0 and seq % MERGE_TM == 0 and (batch * seq) % FFN_TM == 0
    slopes = jnp.asarray([2.0 ** (-8.0 * (i + 1) / DIFF_HEADS) for i in range(DIFF_HEADS)], F32)
    w32 = lambda w: w.astype(F32)
    vec = lambda v: v.reshape(1, -1).astype(F32)

    xt = x.reshape(batch * seq, d_model)
    mem2d = mem.reshape(batch * m_tokens, d_model)
    for l in range(depth):
        lam_init = 0.8 - 0.6 * math.exp(-0.3 * l)
        xt, proj = _ffn_proj(xt, vec(ffn1_norm[l]), w32(ffn1_w_gate[l]), w32(ffn1_w_up[l]),
                             w32(ffn1_w_down[l]), vec(mix_norm[l]), w32(w_in[l]))
        mk, mv = _memkv(mem2d, vec(mem_norm[l]), w32(w_mem_kv[l]))
        o_na = _na(proj, na_rpb[l].reshape(-1).astype(F32), batch, seq)
        o_diff = _diff(proj, slopes, vec(diff_lambda_q1[l]), vec(diff_lambda_k1[l]),
                       vec(diff_lambda_q2[l]), vec(diff_lambda_k2[l]), vec(diff_subln[l]),
                       batch, seq, lam_init)
        xt = _merge_ffn(xt, proj, o_na, o_diff, mk, mv, vec(mix_norm[l]), w32(w_gate[l]),
                        vec(b_gate[l]), w32(w_br_na[l]), w32(w_br_diff[l]), w32(w_br_mem[l]),
                        w32(w_out[l]), vec(ffn2_norm[l]), w32(ffn2_w_gate[l]), w32(ffn2_w_up[l]),
                        w32(ffn2_w_down[l]), vec(final_norm), seq, m_tokens,
                        final_norm=(l == depth - 1))
    return xt.reshape(batch, seq, d_model)
```

```python
import functools
import math

import jax
import jax.numpy as jnp
from jax import lax
from jax.experimental import pallas as pl
from jax.experimental.pallas import tpu as pltpu

F32 = jnp.float32
BF16 = jnp.bfloat16

D_MODEL = 1024
GRID_W = 64
NA_HEADS = 8
NA_HEAD_DIM = 64
NA_WIN_ROWS = 8
NA_WIN_COLS = 16
DIFF_HEADS = 4
DIFF_QK_DIM = 64
DIFF_V_DIM = 128
MEM_HEADS = 4
MEM_HEAD_DIM = 128
D_FF = 2816
NORM_EPS = 1e-6
NA_WIDTH = NA_HEADS * NA_HEAD_DIM
DIFF_QK_WIDTH = DIFF_HEADS * 2 * DIFF_QK_DIM
DIFF_V_WIDTH = DIFF_HEADS * DIFF_V_DIM
MEM_WIDTH = MEM_HEADS * MEM_HEAD_DIM
IN_WIDTH = 3 * NA_WIDTH + 2 * DIFF_QK_WIDTH + DIFF_V_WIDTH + MEM_WIDTH
O_NQ = 0
O_NK = O_NQ + NA_WIDTH
O_NV = O_NK + NA_WIDTH
O_DQ = O_NV + NA_WIDTH
O_DK = O_DQ + DIFF_QK_WIDTH
O_DV = O_DK + DIFF_QK_WIDTH
O_MQ = O_DV + DIFF_V_WIDTH

LANES = 128
V7X_VMEM_LIMIT_BYTES = 56 * 1024 * 1024

FFN_TM = 512
STAGE_ROWS = 128
STAGE_SLOTS = 4
FFN_TF = 256
PROJ_TN = 512
DIFF_TQ = 512
DIFF_BATCH_CHUNK = 4
MERGE_TM = 512
MEMKV_TM = 1024
NEG_BIG = -1e30


def _rmsnorm_f32(x, g):
    ms = jnp.mean(x * x, axis=-1, keepdims=True)
    return (x * lax.rsqrt(ms + NORM_EPS)) * g


def _softmax_rows(logits):
    m = jnp.max(logits, axis=-1, keepdims=True)
    e = jnp.exp(logits - m)
    s = jnp.sum(e, axis=-1, keepdims=True)
    return e * (1.0 / s)


def _dot_nt(a, b):
    return lax.dot_general(a, b, (((1,), (1,)), ((), ())), preferred_element_type=F32)


def _dot(a, b):
    return jnp.dot(a, b, preferred_element_type=F32)


def _resident(shape):
    nd = len(shape)
    return pl.BlockSpec(shape, lambda *_: (0,) * nd, pipeline_mode=pl.Buffered(1))


def _aligned(idx, multiple):
    return idx if isinstance(idx, int) else pl.multiple_of(idx, multiple)


def _software_pipeline(n_items, stages, order):
    depth = len(stages)
    assert n_items >= depth and sorted(order) == list(range(depth))

    def trip(t, parity, valid):
        for k in order:
            if valid(k):
                stages[k](t - k, (parity - k) % 2)

    for t in range(depth - 1):
        trip(t, t % 2, lambda k, t=t: k <= t)
    start = depth - 1
    if (n_items - start) % 2:
        trip(start, start % 2, lambda k: True)
        start += 1

    def body(u, carry):
        t = start + 2 * u
        trip(t, start % 2, lambda k: True)
        trip(t + 1, (start + 1) % 2, lambda k: True)
        return carry

    lax.fori_loop(0, (n_items - start) // 2, body, 0)
    for t in range(n_items, n_items + depth - 1):
        trip(t, t % 2, lambda k, t=t: t - k < n_items)


def _stage_copy(w_hbm, stage_ref, sem_ref, chunk, slot):
    width = w_hbm.shape[1]
    return pltpu.make_async_copy(
        w_hbm.at[pl.ds(chunk * STAGE_ROWS, STAGE_ROWS), :],
        stage_ref.at[slot, :, :width],
        sem_ref.at[slot])


def _load_weights_bf16(pairs, stage_ref, sem_ref):
    jobs = [(w, dst, c) for w, dst in pairs for c in range(w.shape[0] // STAGE_ROWS)]
    ahead = STAGE_SLOTS - 1
    for n in range(min(ahead, len(jobs))):
        _stage_copy(jobs[n][0], stage_ref, sem_ref, jobs[n][2], n % STAGE_SLOTS).start()
    for n, (w, dst, c) in enumerate(jobs):
        slot = n % STAGE_SLOTS
        if n + ahead < len(jobs):
            nw, _, nc = jobs[n + ahead]
            _stage_copy(nw, stage_ref, sem_ref, nc, (n + ahead) % STAGE_SLOTS).start()
        _stage_copy(w, stage_ref, sem_ref, c, slot).wait()
        dst[c * STAGE_ROWS:(c + 1) * STAGE_ROWS, :] = (
            stage_ref[slot, :, :w.shape[1]].astype(BF16))


_HBM = pl.BlockSpec(memory_space=pl.ANY)


def _swiglu_half_step(x, g_ref, wg_ref, wu_ref, wd_ref):
    h = _rmsnorm_f32(x, g_ref[...]).astype(BF16)
    acc = jnp.zeros(x.shape, F32)
    for c in range(D_FF // FFN_TF):
        sl = slice(c * FFN_TF, (c + 1) * FFN_TF)
        gate = _dot(h, wg_ref[:, sl])
        up = _dot(h, wu_ref[:, sl])
        act = (gate * jax.nn.sigmoid(gate) * up).astype(BF16)
        acc = acc + _dot(act, wd_ref[sl, :])
    return x + 0.5 * acc


def _ffn_proj_body(x_ref, g_ref, wg_hbm, wu_hbm, wd_hbm, pg_ref, win_hbm, x1_ref, proj_ref,
                   wg_ref, wu_ref, wd_ref, win_ref, stage_ref, sem_ref):
    @pl.when(pl.program_id(0) == 0)
    def _load_weights():
        _load_weights_bf16(((wg_hbm, wg_ref), (wu_hbm, wu_ref), (wd_hbm, wd_ref),
                            (win_hbm, win_ref)), stage_ref, sem_ref)

    x1 = _swiglu_half_step(x_ref[...], g_ref, wg_ref, wu_ref, wd_ref)
    x1_ref[...] = x1
    h = _rmsnorm_f32(x1, pg_ref[...]).astype(BF16)
    for c in range(IN_WIDTH // PROJ_TN):
        lo = c * PROJ_TN
        y = _dot(h, win_ref[:, lo:lo + PROJ_TN])
        if lo == O_NQ or lo == O_DQ:
            y = y * (NA_HEAD_DIM ** -0.5)
        proj_ref[:, lo:lo + PROJ_TN] = y.astype(BF16)


def _ffn_proj(x, norm_g, wg, wu, wd, mix_g, w_in):
    n = x.shape[0]
    row = lambda i: (i, 0)
    return pl.pallas_call(
        _ffn_proj_body,
        grid=(n // FFN_TM,),
        in_specs=[
            pl.BlockSpec((FFN_TM, D_MODEL), row),
            _resident((1, D_MODEL)),
            _HBM, _HBM, _HBM,
            _resident((1, D_MODEL)),
            _HBM,
        ],
        out_specs=[pl.BlockSpec((FFN_TM, D_MODEL), row), pl.BlockSpec((FFN_TM, IN_WIDTH), row)],
        out_shape=[jax.ShapeDtypeStruct((n, D_MODEL), F32),
                   jax.ShapeDtypeStruct((n, IN_WIDTH), BF16)],
        scratch_shapes=[
            pltpu.VMEM((D_MODEL, D_FF), BF16), pltpu.VMEM((D_MODEL, D_FF), BF16),
            pltpu.VMEM((D_FF, D_MODEL), BF16), pltpu.VMEM((D_MODEL, IN_WIDTH), BF16),
            pltpu.VMEM((STAGE_SLOTS, STAGE_ROWS,IN_WIDTH), F32),
            pltpu.SemaphoreType.DMA((STAGE_SLOTS,)),
        ],
        compiler_params=pltpu.CompilerParams(
            dimension_semantics=("arbitrary",), vmem_limit_bytes=V7X_VMEM_LIMIT_BYTES),
        name="ffn_proj",
    )(x, norm_g, wg, wu, wd, mix_g, w_in)


def _memkv_body(m_ref, g_ref, w_hbm, k_ref, v_ref, w_ref, stage_ref, sem_ref):
    @pl.when(pl.program_id(0) == 0)
    def _load_weights():
        _load_weights_bf16(((w_hbm, w_ref),), stage_ref, sem_ref)

    h = _rmsnorm_f32(m_ref[...], g_ref[...]).astype(BF16)
    k_ref[...] = _dot(h, w_ref[:, :MEM_WIDTH]).astype(BF16)
    v_ref[...] = _dot(h, w_ref[:, MEM_WIDTH:]).astype(BF16)


def _memkv(mem2d, norm_g, w_kv):
    n = mem2d.shape[0]
    tm = math.gcd(n, MEMKV_TM)
    row = lambda i: (i, 0)
    shp = jax.ShapeDtypeStruct((n, MEM_WIDTH), BF16)
    return pl.pallas_call(
        _memkv_body,
        grid=(n // tm,),
        in_specs=[
            pl.BlockSpec((tm, D_MODEL), row),
            _resident((1, D_MODEL)),
            _HBM,
        ],
        out_specs=[pl.BlockSpec((tm, MEM_WIDTH), row)] * 2,
        out_shape=[shp, shp],
        scratch_shapes=[
            pltpu.VMEM((D_MODEL, 2 * MEM_WIDTH), BF16),
            pltpu.VMEM((STAGE_SLOTS, STAGE_ROWS,2 * MEM_WIDTH), F32),
            pltpu.SemaphoreType.DMA((STAGE_SLOTS,)),
        ],
        compiler_params=pltpu.CompilerParams(dimension_semantics=("arbitrary",)),
        name="memkv",
    )(mem2d, norm_g, w_kv)


RPB_ROWS = 2 * NA_WIN_ROWS - 1
RPB_COLS = 2 * NA_WIN_COLS - 1
NA_BAND_KEYS = NA_WIN_ROWS * GRID_W
NA_BATCH_CHUNK = 8
NA_ITEM_ROWS = 16


def _na_body(rpb_ref, q_ref, k_ref, v_ref, o_ref, tile_ref, band_ref,
             lg0_ref, lg1_ref, m0_ref, m1_ref, a0_ref, a1_ref, *, rows, nbatch):
    lg_ref, m_ref, a_ref = (lg0_ref, lg1_ref), (m0_ref, m1_ref), (a0_ref, a1_ref)
    hp = pl.program_id(0)
    b = pl.program_id(1)
    wr = min(NA_WIN_ROWS, rows)

    @pl.when(b == 0)
    def _build_bias():
        c = lax.broadcasted_iota(jnp.int32, (GRID_W, LANES), 0)
        kc = lax.broadcasted_iota(jnp.int32, (GRID_W, LANES), 1) & (GRID_W - 1)
        cs = jnp.clip(c - NA_WIN_COLS // 2, 0, GRID_W - NA_WIN_COLS)
        valid = (kc >= cs) & (kc < cs + NA_WIN_COLS)
        dc_lane = lax.broadcasted_iota(jnp.int32, (8, LANES), 1) & (GRID_W - 1)

        def tile_step(t, carry):
            hl = t // RPB_ROWS
            dr = t - hl * RPB_ROWS
            base = ((hp * 2 + hl) * RPB_ROWS + dr) * RPB_COLS
            vec = jnp.zeros((8, LANES), F32)
            for dc in range(RPB_COLS):
                vec = jnp.where(dc_lane == dc, rpb_ref[base + dc], vec)
            rows8 = jnp.concatenate([vec] * (GRID_W // 8), axis=0)
            tile = pltpu.roll(rows8, LANES - (NA_WIN_COLS - 1), 1, stride=1, stride_axis=0)
            tile_ref[hl, dr] = jnp.where(valid, tile, NEG_BIG)
            return carry

        lax.fori_loop(0, 2 * RPB_ROWS, tile_step, 0, unroll=True)
        low = lax.broadcasted_iota(jnp.int32, (GRID_W, LANES), 1) < GRID_W
        for hl in range(2):
            for dr0 in range(NA_WIN_ROWS):
                for j in range(wr // 2):
                    band_ref[dr0, hl * GRID_W:(hl + 1) * GRID_W, j * LANES:(j + 1) * LANES] = (
                        jnp.where(low, tile_ref[hl, dr0 + 2 * j], tile_ref[hl, dr0 + 2 * j + 1]))

    lane = lax.broadcasted_iota(jnp.int32, (GRID_W, LANES), 1)
    low = lane < NA_HEAD_DIM

    items_per_batch = rows // NA_ITEM_ROWS
    ipb_shift = items_per_batch.bit_length() - 1

    def locate(item, rr):
        if isinstance(item, int):
            bl, it = divmod(item, items_per_batch)
        else:
            bl, it = lax.shift_right_logical(item, ipb_shift), item & (items_per_batch - 1)
        return it * NA_ITEM_ROWS + rr, bl * (rows * GRID_W)

    def band_rows(r, base):
        if isinstance(r, int):
            rs = min(max(r - wr // 2, 0), rows - wr)
        else:
            rs = jnp.clip(r - wr // 2, 0, rows - wr)
        return rs, pl.ds(_aligned(base + rs * GRID_W, GRID_W), wr * GRID_W)

    def query_rows(r, base):
        return pl.ds(_aligned(base + r * GRID_W, GRID_W), GRID_W)

    def logits_stage(item, slot):
        for rr in range(NA_ITEM_ROWS):
            r, base = locate(item, rr)
            rs, keys = band_rows(r, base)
            q2 = q_ref[query_rows(r, base), :]
            zero = jnp.zeros_like(q2)
            q_st = jnp.concatenate([jnp.where(low, q2, zero), jnp.where(low, zero, q2)], axis=0)
            lg = _dot_nt(q_st, k_ref[keys, :]) + band_ref[rs - r + (NA_WIN_ROWS - 1)]
            lg_ref[slot][rr] = lg
            lane_max = lg[:, :LANES]
            for c in range(1, NA_BAND_KEYS // LANES):
                lane_max = jnp.maximum(lane_max, lg[:, c * LANES:(c + 1) * LANES])
            m_ref[slot][rr] = lane_max

    def value_stage(item, slot):
        for rr in range(NA_ITEM_ROWS):
            r, base = locate(item, rr)
            _, keys = band_rows(r, base)
            row_max = jnp.max(m_ref[slot][rr], axis=-1, keepdims=True)
            e = jnp.exp(lg_ref[slot][rr] - row_max)
            lane_sum = e[:, :LANES]
            for c in range(1, NA_BAND_KEYS // LANES):
                lane_sum = lane_sum + e[:, c * LANES:(c + 1) * LANES]
            a_ref[slot][rr, :, :LANES] = _dot(e.astype(BF16), v_ref[keys, :])
            a_ref[slot][rr, :, LANES:] = lane_sum

    def finish_stage(item, slot):
        for rr in range(NA_ITEM_ROWS):
            r, base = locate(item, rr)
            acc = a_ref[slot][rr]
            o_st = acc[:, :LANES] * (1.0 / jnp.sum(acc[:, LANES:], axis=-1, keepdims=True))
            o = jnp.where(low, o_st[:GRID_W], o_st[GRID_W:])
            o_ref[query_rows(r, base), :] = o.astype(BF16)

    _software_pipeline(nbatch * items_per_batch, (logits_stage, value_stage, finish_stage),
                       order=(1, 0, 2))


def _na(proj, rpb_flat, batch, seq):
    rows = seq // GRID_W
    items_per_batch = rows // NA_ITEM_ROWS
    nbatch = math.gcd(batch, NA_BATCH_CHUNK)
    assert rows >= NA_WIN_ROWS and NA_WIN_ROWS % 2 == 0 and rows % NA_ITEM_ROWS == 0
    assert items_per_batch & (items_per_batch - 1) == 0
    pairs = NA_HEADS // 2
    blk = lambda off: pl.BlockSpec((nbatch * seq, LANES), lambda hp, b, off=off: (b, off + hp))
    return pl.pallas_call(
        functools.partial(_na_body, rows=rows, nbatch=nbatch),
        grid=(pairs, batch // nbatch),
        in_specs=[
            pl.BlockSpec(memory_space=pltpu.SMEM),
            blk(O_NQ // LANES), blk(O_NK // LANES), blk(O_NV // LANES),
        ],
        out_specs=pl.BlockSpec((nbatch * seq, LANES), lambda hp, b: (b, hp)),
        out_shape=jax.ShapeDtypeStruct((batch * seq, NA_WIDTH), BF16),
        scratch_shapes=[
            pltpu.VMEM((2, RPB_ROWS, GRID_W, LANES), F32),
            pltpu.VMEM((NA_WIN_ROWS, 2 * GRID_W, NA_BAND_KEYS), F32),
            pltpu.VMEM((NA_ITEM_ROWS, 2 * GRID_W, NA_BAND_KEYS), F32),
            pltpu.VMEM((NA_ITEM_ROWS, 2 * GRID_W, NA_BAND_KEYS), F32),
            pltpu.VMEM((NA_ITEM_ROWS, 2 * GRID_W, LANES), F32),
            pltpu.VMEM((NA_ITEM_ROWS, 2 * GRID_W, LANES), F32),
            pltpu.VMEM((NA_ITEM_ROWS, 2 * GRID_W, 2 * LANES), F32),
            pltpu.VMEM((NA_ITEM_ROWS, 2 * GRID_W, 2 * LANES), F32),
        ],
        compiler_params=pltpu.CompilerParams(
            dimension_semantics=("arbitrary", "arbitrary"), vmem_limit_bytes=V7X_VMEM_LIMIT_BYTES),
        name="na_attn",
    )(rpb_flat, proj, proj, proj)


def _diff_body(slope_ref, lq1_ref, lk1_ref, lq2_ref, lk2_ref, sub_ref, q_ref, k_ref, v_ref, o_ref,
               dist_ref, v1_ref, kt_ref, s0_ref, s1_ref, m0_ref, m1_ref, a0_ref, a1_ref,
               *, lam_init, seq, nbatch):
    s_ref, m_ref, a_ref = (s0_ref, s1_ref), (m0_ref, m1_ref), (a0_ref, a1_ref)
    h = pl.program_id(0)
    b = pl.program_id(1)
    nq = seq // DIFF_TQ

    @pl.when(b == 0)
    def _build_dist():
        slope = slope_ref[h]
        rel = (lax.broadcasted_iota(jnp.int32, (DIFF_TQ, DIFF_TQ), 0)
               - lax.broadcasted_iota(jnp.int32, (DIFF_TQ, DIFF_TQ), 1))
        for d in range(2 * nq - 1):
            off = (d - (nq - 1)) * DIFF_TQ
            dist_ref[d] = slope * jnp.abs((rel + off).astype(F32))

    lam = (jnp.exp(jnp.sum(lq1_ref[...] * lk1_ref[...], axis=-1, keepdims=True))
           - jnp.exp(jnp.sum(lq2_ref[...] * lk2_ref[...], axis=-1, keepdims=True))
           + lam_init)
    lane = lax.broadcasted_iota(jnp.int32, (DIFF_TQ, LANES), 1)
    first_map = lane < DIFF_QK_DIM
    nq_shift = nq.bit_length() - 1

    def q_rows(j):
        return pl.ds(_aligned(j * DIFF_TQ, DIFF_TQ), DIFF_TQ)

    def batch_rows(j):
        bl = j // nq if isinstance(j, int) else lax.shift_right_logical(j, nq_shift)
        return pl.ds(_aligned(bl * seq, seq), seq)

    def logits_stage(j, slot):
        i = j % nq if isinstance(j, int) else j & (nq - 1)
        q = q_ref[q_rows(j), :]
        zero = jnp.zeros_like(q)
        k_t = kt_ref[j // nq if isinstance(j, int) else lax.shift_right_logical(j, nq_shift)]
        dist = jnp.concatenate([dist_ref[i - c + (nq - 1)] for c in range(nq)], axis=1)
        for m, qm in enumerate((jnp.where(first_map, q, zero), jnp.where(first_map, zero, q))):
            s = _dot(qm, k_t) - dist
            s_ref[slot][m] = s
            lane_max = s[:, :LANES]
            for c in range(1, seq // LANES):
                lane_max = jnp.maximum(lane_max, s[:, c * LANES:(c + 1) * LANES])
            m_ref[slot][m] = lane_max

    def value_stage(j, slot):
        e = jnp.concatenate(
            [jnp.exp(s_ref[slot][m] - jnp.max(m_ref[slot][m], axis=-1, keepdims=True)).astype(BF16)
             for m in range(2)], axis=0)
        a_ref[slot][...] = _dot(e, v1_ref[batch_rows(j), :])

    def finish_stage(j, slot):
        acc = a_ref[slot][...]
        num1, den1 = acc[:DIFF_TQ, :DIFF_V_DIM], acc[:DIFF_TQ, DIFF_V_DIM:DIFF_V_DIM + 1]
        num2, den2 = acc[DIFF_TQ:, :DIFF_V_DIM], acc[DIFF_TQ:, DIFF_V_DIM:DIFF_V_DIM + 1]
        o = num1 * (1.0 / den1) - (lam * (1.0 / den2)) * num2
        o = _rmsnorm_f32(o, sub_ref[...]) * (1.0 - lam_init)
        o_ref[q_rows(j), :] = o.astype(BF16)

    v = v_ref[...]
    v1_ref[:, :DIFF_V_DIM] = v
    v1_ref[:, DIFF_V_DIM:] = jnp.where(
        lax.broadcasted_iota(jnp.int32, v.shape, 1) == 0, 1.0, 0.0).astype(BF16)
    for bl in range(nbatch):
        kt_ref[bl] = k_ref[bl * seq:(bl + 1) * seq, :].T
    _software_pipeline(nbatch * nq, (logits_stage, value_stage, finish_stage), order=(1, 0, 2))


def _diff(proj, slopes, lq1, lk1, lq2, lk2, subln, batch, seq, lam_init):
    nq = seq // DIFF_TQ
    nbatch = math.gcd(batch, DIFF_BATCH_CHUNK)
    assert nq & (nq - 1) == 0
    vec = lambda w: pl.BlockSpec((1, w), lambda h, b: (0, 0))
    blk = lambda off: pl.BlockSpec((nbatch * seq, LANES), lambda h, b, off=off: (b, off + h))
    return pl.pallas_call(
        functools.partial(_diff_body, lam_init=lam_init, seq=seq, nbatch=nbatch),
        grid=(DIFF_HEADS, batch // nbatch),
        in_specs=[
            pl.BlockSpec(memory_space=pltpu.SMEM),
            vec(DIFF_QK_DIM), vec(DIFF_QK_DIM), vec(DIFF_QK_DIM), vec(DIFF_QK_DIM), vec(DIFF_V_DIM),
            blk(O_DQ // LANES), blk(O_DK // LANES), blk(O_DV // LANES),
        ],
        out_specs=pl.BlockSpec((nbatch * seq, LANES), lambda h, b: (b, h)),
        out_shape=jax.ShapeDtypeStruct((batch * seq, DIFF_V_WIDTH), BF16),
        scratch_shapes=[
            pltpu.VMEM((2 * nq - 1, DIFF_TQ, DIFF_TQ), F32),
            pltpu.VMEM((nbatch * seq, 2 * LANES), BF16),
            pltpu.VMEM((nbatch, LANES, seq), BF16),
            pltpu.VMEM((2, DIFF_TQ, seq), F32), pltpu.VMEM((2, DIFF_TQ, seq), F32),
            pltpu.VMEM((2, DIFF_TQ, LANES), F32), pltpu.VMEM((2, DIFF_TQ, LANES), F32),
            pltpu.VMEM((2 * DIFF_TQ, 2 * LANES), F32), pltpu.VMEM((2 * DIFF_TQ, 2 * LANES), F32),
        ],
        compiler_params=pltpu.CompilerParams(
            dimension_semantics=("arbitrary", "arbitrary"),
            vmem_limit_bytes=V7X_VMEM_LIMIT_BYTES),
        name="diff_attn",
    )(slopes, lq1, lk1, lq2, lk2, subln, proj, proj, proj)


def _merge_ffn_body(x_ref, mq_ref, ona_ref, odf_ref, mk_ref, mv_ref, g_ref, wgate_hbm, bgate_ref,
                    wna_hbm, wdf_hbm, wmem_hbm, wout_hbm, fg_ref, wg_hbm, wu_hbm, wd_hbm, ng_ref,
                    o_ref, wgate_ref, wna_ref, wdf_ref, wmem_ref, wout_ref, wg_ref, wu_ref, wd_ref,
                    stage_ref, sem_ref, *, final_norm):
    @pl.when(pl.program_id(0) == 0)
    def _load_weights():
        _load_weights_bf16(
            ((wgate_hbm, wgate_ref), (wna_hbm, wna_ref), (wdf_hbm, wdf_ref), (wmem_hbm, wmem_ref),
             (wout_hbm, wout_ref), (wg_hbm, wg_ref), (wu_hbm, wu_ref), (wd_hbm, wd_ref)),
            stage_ref, sem_ref)

    x = x_ref[...]
    h = _rmsnorm_f32(x, g_ref[...]).astype(BF16)
    heads = []
    for hh in range(MEM_HEADS):
        sl = slice(hh * MEM_HEAD_DIM, (hh + 1) * MEM_HEAD_DIM)
        logits = _dot_nt(mq_ref[:, sl], mk_ref[:, sl]) * (MEM_HEAD_DIM ** -0.5)
        p = _softmax_rows(logits).astype(BF16)
        heads.append(_dot(p, mv_ref[:, sl]))
    o_mem = jnp.concatenate(heads, axis=-1).astype(BF16)
    branches = (
        _dot(ona_ref[...], wna_ref[...]),
        _dot(odf_ref[...], wdf_ref[...]),
        _dot(o_mem, wmem_ref[...]),
    )
    merged = jnp.zeros(x.shape, F32)
    for i, y in enumerate(branches):
        sl = slice(i * D_MODEL, (i + 1) * D_MODEL)
        gate = jax.nn.sigmoid(_dot(h, wgate_ref[:, sl]) + bgate_ref[:, sl])
        merged = merged + gate * y
    x2 = x + _dot(merged.astype(BF16), wout_ref[...])
    y = _swiglu_half_step(x2, fg_ref, wg_ref, wu_ref, wd_ref)
    if final_norm:
        y = _rmsnorm_f32(y, ng_ref[...])
    o_ref[...] = y


def _merge_ffn(x1, proj, o_na, o_diff, mk, mv, norm_g, w_gate, b_gate, w_na, w_df, w_mem, w_out,
               ffn_g, wg, wu, wd, final_g, seq, m_tokens, *, final_norm):
    n = x1.shape[0]
    per_b = seq // MERGE_TM
    row = lambda i: (i, 0)
    return pl.pallas_call(
        functools.partial(_merge_ffn_body, final_norm=final_norm),
        grid=(n // MERGE_TM,),
        in_specs=[
            pl.BlockSpec((MERGE_TM, D_MODEL), row),
            pl.BlockSpec((MERGE_TM, MEM_WIDTH), lambda i: (i, O_MQ // MEM_WIDTH)),
            pl.BlockSpec((MERGE_TM, NA_WIDTH), row),
            pl.BlockSpec((MERGE_TM, DIFF_V_WIDTH), row),
            pl.BlockSpec((m_tokens, MEM_WIDTH), lambda i: (i // per_b, 0)),
            pl.BlockSpec((m_tokens, MEM_WIDTH), lambda i: (i // per_b, 0)),
            _resident((1, D_MODEL)),
            _HBM,
            _resident((1, 3 * D_MODEL)),
            _HBM, _HBM, _HBM, _HBM,
            _resident((1, D_MODEL)),
            _HBM, _HBM, _HBM,
            _resident((1, D_MODEL)),
        ],
        out_specs=pl.BlockSpec((MERGE_TM, D_MODEL), row),
        out_shape=jax.ShapeDtypeStruct((n, D_MODEL), F32),
        scratch_shapes=[
            pltpu.VMEM((D_MODEL, 3 * D_MODEL), BF16),
            pltpu.VMEM((NA_WIDTH, D_MODEL), BF16), pltpu.VMEM((DIFF_V_WIDTH, D_MODEL), BF16),
            pltpu.VMEM((MEM_WIDTH, D_MODEL), BF16), pltpu.VMEM((D_MODEL, D_MODEL), BF16),
            pltpu.VMEM((D_MODEL, D_FF), BF16), pltpu.VMEM((D_MODEL, D_FF), BF16),
            pltpu.VMEM((D_FF, D_MODEL), BF16),
            pltpu.VMEM((STAGE_SLOTS, STAGE_ROWS,3 * D_MODEL), F32),
            pltpu.SemaphoreType.DMA((STAGE_SLOTS,)),
        ],
        compiler_params=pltpu.CompilerParams(
            dimension_semantics=("arbitrary",), vmem_limit_bytes=V7X_VMEM_LIMIT_BYTES),
        name="merge_ffn",
    )(x1, proj, o_na, o_diff, mk, mv, norm_g, w_gate, b_gate, w_na, w_df, w_mem, w_out,
      ffn_g, wg, wu, wd, final_g)


def kernel(x, mem, ffn1_norm, ffn1_w_gate, ffn1_w_up, ffn1_w_down, mix_norm, w_in, na_rpb,
           diff_lambda_q1, diff_lambda_k1, diff_lambda_q2, diff_lambda_k2, diff_subln,
           mem_norm, w_mem_kv, w_gate, b_gate, w_br_na, w_br_diff, w_br_mem, w_out,
           ffn2_norm, ffn2_w_gate, ffn2_w_up, ffn2_w_down, final_norm):
    batch, seq, d_model = x.shape
    m_tokens = mem.shape[1]
    depth = ffn1_norm.shape[0]
    assert d_model == D_MODEL and seq % GRID_W == 0
    assert seq % DIFF_TQ == 0 and seq % MERGE_TM == 0 and (batch * seq) % FFN_TM == 0
    slopes = jnp.asarray([2.0 ** (-8.0 * (i + 1) / DIFF_HEADS) for i in range(DIFF_HEADS)], F32)
    w32 = lambda w: w.astype(F32)
    vec = lambda v: v.reshape(1, -1).astype(F32)

    xt = x.reshape(batch * seq, d_model)
    mem2d = mem.reshape(batch * m_tokens, d_model)
    for l in range(depth):
        lam_init = 0.8 - 0.6 * math.exp(-0.3 * l)
        xt, proj = _ffn_proj(xt, vec(ffn1_norm[l]), w32(ffn1_w_gate[l]), w32(ffn1_w_up[l]),
                             w32(ffn1_w_down[l]), vec(mix_norm[l]), w32(w_in[l]))
        mk, mv = _memkv(mem2d, vec(mem_norm[l]), w32(w_mem_kv[l]))
        o_na = _na(proj, na_rpb[l].reshape(-1).astype(F32), batch, seq)
        o_diff = _diff(proj, slopes, vec(diff_lambda_q1[l]), vec(diff_lambda_k1[l]),
                       vec(diff_lambda_q2[l]), vec(diff_lambda_k2[l]), vec(diff_subln[l]),
                       batch, seq, lam_init)
        xt = _merge_ffn(xt, proj, o_na, o_diff, mk, mv, vec(mix_norm[l]), w32(w_gate[l]),
                        vec(b_gate[l]), w32(w_br_na[l]), w32(w_br_diff[l]), w32(w_br_mem[l]),
                        w32(w_out[l]), vec(ffn2_norm[l]), w32(ffn2_w_gate[l]), w32(ffn2_w_up[l]),
                        w32(ffn2_w_down[l]), vec(final_norm), seq, m_tokens,
                        final_norm=(l == depth - 1))
    return xt.reshape(batch, seq, d_model)
```

```python
import functools
import math

import jax
import jax.numpy as jnp
from jax import lax
from jax.experimental import pallas as pl
from jax.experimental.pallas import tpu as pltpu

F32 = jnp.float32
BF16 = jnp.bfloat16

D_MODEL = 1024
GRID_W = 64
NA_HEADS = 8
NA_HEAD_DIM = 64
NA_WIN_ROWS = 8
NA_WIN_COLS = 16
DIFF_HEADS = 4
DIFF_QK_DIM = 64
DIFF_V_DIM = 128
MEM_HEADS = 4
MEM_HEAD_DIM = 128
D_FF = 2816
NORM_EPS = 1e-6
NA_WIDTH = NA_HEADS * NA_HEAD_DIM
DIFF_QK_WIDTH = DIFF_HEADS * 2 * DIFF_QK_DIM
DIFF_V_WIDTH = DIFF_HEADS * DIFF_V_DIM
MEM_WIDTH = MEM_HEADS * MEM_HEAD_DIM
IN_WIDTH = 3 * NA_WIDTH + 2 * DIFF_QK_WIDTH + DIFF_V_WIDTH + MEM_WIDTH
O_NQ = 0
O_NK = O_NQ + NA_WIDTH
O_NV = O_NK + NA_WIDTH
O_DQ = O_NV + NA_WIDTH
O_DK = O_DQ + DIFF_QK_WIDTH
O_DV = O_DK + DIFF_QK_WIDTH
O_MQ = O_DV + DIFF_V_WIDTH

LANES = 128
V7X_VMEM_LIMIT_BYTES = 56 * 1024 * 1024

FFN_TM = 512
STAGE_ROWS = 128
STAGE_SLOTS = 4
FFN_TF = 256
PROJ_TN = 512
DIFF_TQ = 512
DIFF_BATCH_CHUNK = 4
MERGE_TM = 512
MEMKV_TM = 1024
NEG_BIG = -1e30


def _rmsnorm_f32(x, g):
    ms = jnp.mean(x * x, axis=-1, keepdims=True)
    return (x * lax.rsqrt(ms + NORM_EPS)) * g


def _softmax_rows(logits):
    m = jnp.max(logits, axis=-1, keepdims=True)
    e = jnp.exp(logits - m)
    s = jnp.sum(e, axis=-1, keepdims=True)
    return e * (1.0 / s)


def _dot_nt(a, b):
    return lax.dot_general(a, b, (((1,), (1,)), ((), ())), preferred_element_type=F32)


def _dot(a, b):
    return jnp.dot(a, b, preferred_element_type=F32)


def _resident(shape):
    nd = len(shape)
    return pl.BlockSpec(shape, lambda *_: (0,) * nd, pipeline_mode=pl.Buffered(1))


def _aligned(idx, multiple):
    return idx if isinstance(idx, int) else pl.multiple_of(idx, multiple)


def _software_pipeline(n_items, stages, order):
    depth = len(stages)
    assert n_items >= depth and sorted(order) == list(range(depth))

    def trip(t, parity, valid):
        for k in order:
            if valid(k):
                stages[k](t - k, (parity - k) % 2)

    for t in range(depth - 1):
        trip(t, t % 2, lambda k, t=t: k <= t)
    start = depth - 1
    if (n_items - start) % 2:
        trip(start, start % 2, lambda k: True)
        start += 1

    def body(u, carry):
        t = start + 2 * u
        trip(t, start % 2, lambda k: True)
        trip(t + 1, (start + 1) % 2, lambda k: True)
        return carry

    lax.fori_loop(0, (n_items - start) // 2, body, 0)
    for t in range(n_items, n_items + depth - 1):
        trip(t, t % 2, lambda k, t=t: t - k < n_items)


def _stage_copy(piece, stage_ref, sem_ref, slot):
    w_hbm, _, chunk, lane = piece
    return pltpu.make_async_copy(
        w_hbm.at[pl.ds(chunk * STAGE_ROWS, STAGE_ROWS), :],
        stage_ref.at[slot, :, lane:lane + w_hbm.shape[1]],
        sem_ref.at[slot])


def _load_weights_bf16(pairs, stage_ref, sem_ref):
    slot_width = stage_ref.shape[2]
    groups = []
    for w, dst in pairs:
        per_slot = slot_width // w.shape[1]
        chunks = list(range(w.shape[0] // STAGE_ROWS))
        for g in range(0, len(chunks), per_slot):
            groups.append([(w, dst, c, k * w.shape[1])
                           for k, c in enumerate(chunks[g:g + per_slot])])

    def start(n):
        for piece in groups[n]:
            _stage_copy(piece, stage_ref, sem_ref, n % STAGE_SLOTS).start()

    ahead = STAGE_SLOTS - 1
    for n in range(min(ahead, len(groups))):
        start(n)
    for n, group in enumerate(groups):
        slot = n % STAGE_SLOTS
        if n + ahead < len(groups):
            start(n + ahead)
        for piece in group:
            _stage_copy(piece, stage_ref, sem_ref, slot).wait()
        for w, dst, c, lane in group:
            dst[c * STAGE_ROWS:(c + 1) * STAGE_ROWS, :] = (
                stage_ref[slot, :, lane:lane + w.shape[1]].astype(BF16))


_HBM = pl.BlockSpec(memory_space=pl.ANY)


def _swiglu_half_step(x, g_ref, wg_ref, wu_ref, wd_ref):
    h = _rmsnorm_f32(x, g_ref[...]).astype(BF16)
    acc = jnp.zeros(x.shape, F32)
    for c in range(D_FF // FFN_TF):
        sl = slice(c * FFN_TF, (c + 1) * FFN_TF)
        gate = _dot(h, wg_ref[:, sl])
        up = _dot(h, wu_ref[:, sl])
        act = (gate * jax.nn.sigmoid(gate) * up).astype(BF16)
        acc = acc + _dot(act, wd_ref[sl, :])
    return x + 0.5 * acc


def _ffn_proj_body(x_ref, g_ref, wg_hbm, wu_hbm, wd_hbm, pg_ref, win_hbm, x1_ref, proj_ref,
                   wg_ref, wu_ref, wd_ref, win_ref, stage_ref, sem_ref):
    @pl.when(pl.program_id(0) == 0)
    def _load_weights():
        _load_weights_bf16(((wg_hbm, wg_ref), (wu_hbm, wu_ref), (wd_hbm, wd_ref),
                            (win_hbm, win_ref)), stage_ref, sem_ref)

    x1 = _swiglu_half_step(x_ref[...], g_ref, wg_ref, wu_ref, wd_ref)
    x1_ref[...] = x1
    h = _rmsnorm_f32(x1, pg_ref[...]).astype(BF16)
    for c in range(IN_WIDTH // PROJ_TN):
        lo = c * PROJ_TN
        y = _dot(h, win_ref[:, lo:lo + PROJ_TN])
        if lo == O_NQ or lo == O_DQ:
            y = y * (NA_HEAD_DIM ** -0.5)
        proj_ref[:, lo:lo + PROJ_TN] = y.astype(BF16)


def _ffn_proj(x, norm_g, wg, wu, wd, mix_g, w_in):
    n = x.shape[0]
    row = lambda i: (i, 0)
    return pl.pallas_call(
        _ffn_proj_body,
        grid=(n // FFN_TM,),
        in_specs=[
            pl.BlockSpec((FFN_TM, D_MODEL), row),
            _resident((1, D_MODEL)),
            _HBM, _HBM, _HBM,
            _resident((1, D_MODEL)),
            _HBM,
        ],
        out_specs=[pl.BlockSpec((FFN_TM, D_MODEL), row), pl.BlockSpec((FFN_TM, IN_WIDTH), row)],
        out_shape=[jax.ShapeDtypeStruct((n, D_MODEL), F32),
                   jax.ShapeDtypeStruct((n, IN_WIDTH), BF16)],
        scratch_shapes=[
            pltpu.VMEM((D_MODEL, D_FF), BF16), pltpu.VMEM((D_MODEL, D_FF), BF16),
            pltpu.VMEM((D_FF, D_MODEL), BF16), pltpu.VMEM((D_MODEL, IN_WIDTH), BF16),
            pltpu.VMEM((STAGE_SLOTS, STAGE_ROWS,IN_WIDTH), F32),
            pltpu.SemaphoreType.DMA((STAGE_SLOTS,)),
        ],
        compiler_params=pltpu.CompilerParams(
            dimension_semantics=("arbitrary",), vmem_limit_bytes=V7X_VMEM_LIMIT_BYTES),
        name="ffn_proj",
    )(x, norm_g, wg, wu, wd, mix_g, w_in)


def _memkv_body(m_ref, g_ref, w_hbm, k_ref, v_ref, w_ref, stage_ref, sem_ref):
    @pl.when(pl.program_id(0) == 0)
    def _load_weights():
        _load_weights_bf16(((w_hbm, w_ref),), stage_ref, sem_ref)

    h = _rmsnorm_f32(m_ref[...], g_ref[...]).astype(BF16)
    k_ref[...] = _dot(h, w_ref[:, :MEM_WIDTH]).astype(BF16)
    v_ref[...] = _dot(h, w_ref[:, MEM_WIDTH:]).astype(BF16)


def _memkv(mem2d, norm_g, w_kv):
    n = mem2d.shape[0]
    tm = math.gcd(n, MEMKV_TM)
    row = lambda i: (i, 0)
    shp = jax.ShapeDtypeStruct((n, MEM_WIDTH), BF16)
    return pl.pallas_call(
        _memkv_body,
        grid=(n // tm,),
        in_specs=[
            pl.BlockSpec((tm, D_MODEL), row),
            _resident((1, D_MODEL)),
            _HBM,
        ],
        out_specs=[pl.BlockSpec((tm, MEM_WIDTH), row)] * 2,
        out_shape=[shp, shp],
        scratch_shapes=[
            pltpu.VMEM((D_MODEL, 2 * MEM_WIDTH), BF16),
            pltpu.VMEM((STAGE_SLOTS, STAGE_ROWS,2 * MEM_WIDTH), F32),
            pltpu.SemaphoreType.DMA((STAGE_SLOTS,)),
        ],
        compiler_params=pltpu.CompilerParams(dimension_semantics=("arbitrary",)),
        name="memkv",
    )(mem2d, norm_g, w_kv)


RPB_ROWS = 2 * NA_WIN_ROWS - 1
RPB_COLS = 2 * NA_WIN_COLS - 1
NA_BAND_KEYS = NA_WIN_ROWS * GRID_W
NA_BATCH_CHUNK = 8
NA_ITEM_ROWS = 16


def _na_body(rpb_ref, q_ref, k_ref, v_ref, o_ref, tile_ref, band_ref,
             lg0_ref, lg1_ref, m0_ref, m1_ref, a0_ref, a1_ref, *, rows, nbatch):
    lg_ref, m_ref, a_ref = (lg0_ref, lg1_ref), (m0_ref, m1_ref), (a0_ref, a1_ref)
    hp = pl.program_id(0)
    b = pl.program_id(1)
    wr = min(NA_WIN_ROWS, rows)

    @pl.when(b == 0)
    def _build_bias():
        c = lax.broadcasted_iota(jnp.int32, (GRID_W, LANES), 0)
        kc = lax.broadcasted_iota(jnp.int32, (GRID_W, LANES), 1) & (GRID_W - 1)
        cs = jnp.clip(c - NA_WIN_COLS // 2, 0, GRID_W - NA_WIN_COLS)
        valid = (kc >= cs) & (kc < cs + NA_WIN_COLS)
        dc_lane = lax.broadcasted_iota(jnp.int32, (8, LANES), 1) & (GRID_W - 1)

        def tile_step(t, carry):
            hl = t // RPB_ROWS
            dr = t - hl * RPB_ROWS
            base = ((hp * 2 + hl) * RPB_ROWS + dr) * RPB_COLS
            vec = jnp.zeros((8, LANES), F32)
            for dc in range(RPB_COLS):
                vec = jnp.where(dc_lane == dc, rpb_ref[base + dc], vec)
            rows8 = jnp.concatenate([vec] * (GRID_W // 8), axis=0)
            tile = pltpu.roll(rows8, LANES - (NA_WIN_COLS - 1), 1, stride=1, stride_axis=0)
            tile_ref[hl, dr] = jnp.where(valid, tile, NEG_BIG)
            return carry

        lax.fori_loop(0, 2 * RPB_ROWS, tile_step, 0, unroll=True)
        low = lax.broadcasted_iota(jnp.int32, (GRID_W, LANES), 1) < GRID_W
        for hl in range(2):
            for dr0 in range(NA_WIN_ROWS):
                for j in range(wr // 2):
                    band_ref[dr0, hl * GRID_W:(hl + 1) * GRID_W, j * LANES:(j + 1) * LANES] = (
                        jnp.where(low, tile_ref[hl, dr0 + 2 * j], tile_ref[hl, dr0 + 2 * j + 1]))

    lane = lax.broadcasted_iota(jnp.int32, (GRID_W, LANES), 1)
    low = lane < NA_HEAD_DIM

    items_per_batch = rows // NA_ITEM_ROWS
    ipb_shift = items_per_batch.bit_length() - 1

    def locate(item, rr):
        if isinstance(item, int):
            bl, it = divmod(item, items_per_batch)
        else:
            bl, it = lax.shift_right_logical(item, ipb_shift), item & (items_per_batch - 1)
        return it * NA_ITEM_ROWS + rr, bl * (rows * GRID_W)

    def band_rows(r, base):
        if isinstance(r, int):
            rs = min(max(r - wr // 2, 0), rows - wr)
        else:
            rs = jnp.clip(r - wr // 2, 0, rows - wr)
        return rs, pl.ds(_aligned(base + rs * GRID_W, GRID_W), wr * GRID_W)

    def query_rows(r, base):
        return pl.ds(_aligned(base + r * GRID_W, GRID_W), GRID_W)

    def logits_stage(item, slot):
        for rr in range(NA_ITEM_ROWS):
            r, base = locate(item, rr)
            rs, keys = band_rows(r, base)
            q2 = q_ref[query_rows(r, base), :]
            zero = jnp.zeros_like(q2)
            q_st = jnp.concatenate([jnp.where(low, q2, zero), jnp.where(low, zero, q2)], axis=0)
            lg = _dot_nt(q_st, k_ref[keys, :]) + band_ref[rs - r + (NA_WIN_ROWS - 1)]
            lg_ref[slot][rr] = lg
            lane_max = lg[:, :LANES]
            for c in range(1, NA_BAND_KEYS // LANES):
                lane_max = jnp.maximum(lane_max, lg[:, c * LANES:(c + 1) * LANES])
            m_ref[slot][rr] = lane_max

    def value_stage(item, slot):
        for rr in range(NA_ITEM_ROWS):
            r, base = locate(item, rr)
            _, keys = band_rows(r, base)
            row_max = jnp.max(m_ref[slot][rr], axis=-1, keepdims=True)
            e = jnp.exp(lg_ref[slot][rr] - row_max)
            lane_sum = e[:, :LANES]
            for c in range(1, NA_BAND_KEYS // LANES):
                lane_sum = lane_sum + e[:, c * LANES:(c + 1) * LANES]
            a_ref[slot][rr, :, :LANES] = _dot(e.astype(BF16), v_ref[keys, :])
            a_ref[slot][rr, :, LANES:] = lane_sum

    def finish_stage(item, slot):
        for rr in range(NA_ITEM_ROWS):
            r, base = locate(item, rr)
            acc = a_ref[slot][rr]
            o_st = acc[:, :LANES] * (1.0 / jnp.sum(acc[:, LANES:], axis=-1, keepdims=True))
            o = jnp.where(low, o_st[:GRID_W], o_st[GRID_W:])
            o_ref[query_rows(r, base), :] = o.astype(BF16)

    _software_pipeline(nbatch * items_per_batch, (logits_stage, value_stage, finish_stage),
                       order=(1, 0, 2))


def _na(proj, rpb_flat, batch, seq):
    rows = seq // GRID_W
    items_per_batch = rows // NA_ITEM_ROWS
    nbatch = math.gcd(batch, NA_BATCH_CHUNK)
    assert rows >= NA_WIN_ROWS and NA_WIN_ROWS % 2 == 0 and rows % NA_ITEM_ROWS == 0
    assert items_per_batch & (items_per_batch - 1) == 0
    pairs = NA_HEADS // 2
    blk = lambda off: pl.BlockSpec((nbatch * seq, LANES), lambda hp, b, off=off: (b, off + hp))
    return pl.pallas_call(
        functools.partial(_na_body, rows=rows, nbatch=nbatch),
        grid=(pairs, batch // nbatch),
        in_specs=[
            pl.BlockSpec(memory_space=pltpu.SMEM),
            blk(O_NQ // LANES), blk(O_NK // LANES), blk(O_NV // LANES),
        ],
        out_specs=pl.BlockSpec((nbatch * seq, LANES), lambda hp, b: (b, hp)),
        out_shape=jax.ShapeDtypeStruct((batch * seq, NA_WIDTH), BF16),
        scratch_shapes=[
            pltpu.VMEM((2, RPB_ROWS, GRID_W, LANES), F32),
            pltpu.VMEM((NA_WIN_ROWS, 2 * GRID_W, NA_BAND_KEYS), F32),
            pltpu.VMEM((NA_ITEM_ROWS, 2 * GRID_W, NA_BAND_KEYS), F32),
            pltpu.VMEM((NA_ITEM_ROWS, 2 * GRID_W, NA_BAND_KEYS), F32),
            pltpu.VMEM((NA_ITEM_ROWS, 2 * GRID_W, LANES), F32),
            pltpu.VMEM((NA_ITEM_ROWS, 2 * GRID_W, LANES), F32),
            pltpu.VMEM((NA_ITEM_ROWS, 2 * GRID_W, 2 * LANES), F32),
            pltpu.VMEM((NA_ITEM_ROWS, 2 * GRID_W, 2 * LANES), F32),
        ],
        compiler_params=pltpu.CompilerParams(
            dimension_semantics=("arbitrary", "arbitrary"), vmem_limit_bytes=V7X_VMEM_LIMIT_BYTES),
        name="na_attn",
    )(rpb_flat, proj, proj, proj)


def _diff_body(slope_ref, lq1_ref, lk1_ref, lq2_ref, lk2_ref, sub_ref, q_ref, k_ref, v_ref, o_ref,
               dist_ref, v1_ref, kt_ref, s0_ref, s1_ref, m0_ref, m1_ref, a0_ref, a1_ref,
               *, lam_init, seq, nbatch):
    s_ref, m_ref, a_ref = (s0_ref, s1_ref), (m0_ref, m1_ref), (a0_ref, a1_ref)
    h = pl.program_id(0)
    b = pl.program_id(1)
    nq = seq // DIFF_TQ

    @pl.when(b == 0)
    def _build_dist():
        slope = slope_ref[h]
        rel = (lax.broadcasted_iota(jnp.int32, (DIFF_TQ, DIFF_TQ), 0)
               - lax.broadcasted_iota(jnp.int32, (DIFF_TQ, DIFF_TQ), 1))
        for d in range(2 * nq - 1):
            off = (d - (nq - 1)) * DIFF_TQ
            dist_ref[d] = slope * jnp.abs((rel + off).astype(F32))

    lam = (jnp.exp(jnp.sum(lq1_ref[...] * lk1_ref[...], axis=-1, keepdims=True))
           - jnp.exp(jnp.sum(lq2_ref[...] * lk2_ref[...], axis=-1, keepdims=True))
           + lam_init)
    lane = lax.broadcasted_iota(jnp.int32, (DIFF_TQ, LANES), 1)
    first_map = lane < DIFF_QK_DIM
    nq_shift = nq.bit_length() - 1

    def q_rows(j):
        return pl.ds(_aligned(j * DIFF_TQ, DIFF_TQ), DIFF_TQ)

    def batch_rows(j):
        bl = j // nq if isinstance(j, int) else lax.shift_right_logical(j, nq_shift)
        return pl.ds(_aligned(bl * seq, seq), seq)

    def logits_stage(j, slot):
        i = j % nq if isinstance(j, int) else j & (nq - 1)
        q = q_ref[q_rows(j), :]
        zero = jnp.zeros_like(q)
        k_t = kt_ref[j // nq if isinstance(j, int) else lax.shift_right_logical(j, nq_shift)]
        dist = jnp.concatenate([dist_ref[i - c + (nq - 1)] for c in range(nq)], axis=1)
        for m, qm in enumerate((jnp.where(first_map, q, zero), jnp.where(first_map, zero, q))):
            s = _dot(qm, k_t) - dist
            s_ref[slot][m] = s
            lane_max = s[:, :LANES]
            for c in range(1, seq // LANES):
                lane_max = jnp.maximum(lane_max, s[:, c * LANES:(c + 1) * LANES])
            m_ref[slot][m] = lane_max

    def value_stage(j, slot):
        e = jnp.concatenate(
            [jnp.exp(s_ref[slot][m] - jnp.max(m_ref[slot][m], axis=-1, keepdims=True)).astype(BF16)
             for m in range(2)], axis=0)
        a_ref[slot][...] = _dot(e, v1_ref[batch_rows(j), :])

    def finish_stage(j, slot):
        acc = a_ref[slot][...]
        num1, den1 = acc[:DIFF_TQ, :DIFF_V_DIM], acc[:DIFF_TQ, DIFF_V_DIM:DIFF_V_DIM + 1]
        num2, den2 = acc[DIFF_TQ:, :DIFF_V_DIM], acc[DIFF_TQ:, DIFF_V_DIM:DIFF_V_DIM + 1]
        o = num1 * (1.0 / den1) - (lam * (1.0 / den2)) * num2
        o = _rmsnorm_f32(o, sub_ref[...]) * (1.0 - lam_init)
        o_ref[q_rows(j), :] = o.astype(BF16)

    v = v_ref[...]
    v1_ref[:, :DIFF_V_DIM] = v
    v1_ref[:, DIFF_V_DIM:] = jnp.where(
        lax.broadcasted_iota(jnp.int32, v.shape, 1) == 0, 1.0, 0.0).astype(BF16)
    for bl in range(nbatch):
        kt_ref[bl] = k_ref[bl * seq:(bl + 1) * seq, :].T
    _software_pipeline(nbatch * nq, (logits_stage, value_stage, finish_stage), order=(1, 0, 2))


def _diff(proj, slopes, lq1, lk1, lq2, lk2, subln, batch, seq, lam_init):
    nq = seq // DIFF_TQ
    nbatch = math.gcd(batch, DIFF_BATCH_CHUNK)
    assert nq & (nq - 1) == 0
    vec = lambda w: pl.BlockSpec((1, w), lambda h, b: (0, 0))
    blk = lambda off: pl.BlockSpec((nbatch * seq, LANES), lambda h, b, off=off: (b, off + h))
    return pl.pallas_call(
        functools.partial(_diff_body, lam_init=lam_init, seq=seq, nbatch=nbatch),
        grid=(DIFF_HEADS, batch // nbatch),
        in_specs=[
            pl.BlockSpec(memory_space=pltpu.SMEM),
            vec(DIFF_QK_DIM), vec(DIFF_QK_DIM), vec(DIFF_QK_DIM), vec(DIFF_QK_DIM), vec(DIFF_V_DIM),
            blk(O_DQ // LANES), blk(O_DK // LANES), blk(O_DV // LANES),
        ],
        out_specs=pl.BlockSpec((nbatch * seq, LANES), lambda h, b: (b, h)),
        out_shape=jax.ShapeDtypeStruct((batch * seq, DIFF_V_WIDTH), BF16),
        scratch_shapes=[
            pltpu.VMEM((2 * nq - 1, DIFF_TQ, DIFF_TQ), F32),
            pltpu.VMEM((nbatch * seq, 2 * LANES), BF16),
            pltpu.VMEM((nbatch, LANES, seq), BF16),
            pltpu.VMEM((2, DIFF_TQ, seq), F32), pltpu.VMEM((2, DIFF_TQ, seq), F32),
            pltpu.VMEM((2, DIFF_TQ, LANES), F32), pltpu.VMEM((2, DIFF_TQ, LANES), F32),
            pltpu.VMEM((2 * DIFF_TQ, 2 * LANES), F32), pltpu.VMEM((2 * DIFF_TQ, 2 * LANES), F32),
        ],
        compiler_params=pltpu.CompilerParams(
            dimension_semantics=("arbitrary", "arbitrary"),
            vmem_limit_bytes=V7X_VMEM_LIMIT_BYTES),
        name="diff_attn",
    )(slopes, lq1, lk1, lq2, lk2, subln, proj, proj, proj)


def _merge_ffn_body(x_ref, mq_ref, ona_ref, odf_ref, mk_ref, mv_ref, g_ref, wgate_hbm, bgate_ref,
                    wna_hbm, wdf_hbm, wmem_hbm, wout_hbm, fg_ref, wg_hbm, wu_hbm, wd_hbm, ng_ref,
                    o_ref, wgate_ref, wna_ref, wdf_ref, wmem_ref, wout_ref, wg_ref, wu_ref, wd_ref,
                    stage_ref, sem_ref, *, final_norm):
    @pl.when(pl.program_id(0) == 0)
    def _load_weights():
        _load_weights_bf16(
            ((wgate_hbm, wgate_ref), (wna_hbm, wna_ref), (wdf_hbm, wdf_ref), (wmem_hbm, wmem_ref),
             (wout_hbm, wout_ref), (wg_hbm, wg_ref), (wu_hbm, wu_ref), (wd_hbm, wd_ref)),
            stage_ref, sem_ref)

    x = x_ref[...]
    h = _rmsnorm_f32(x, g_ref[...]).astype(BF16)
    heads = []
    for hh in range(MEM_HEADS):
        sl = slice(hh * MEM_HEAD_DIM, (hh + 1) * MEM_HEAD_DIM)
        logits = _dot_nt(mq_ref[:, sl], mk_ref[:, sl]) * (MEM_HEAD_DIM ** -0.5)
        p = _softmax_rows(logits).astype(BF16)
        heads.append(_dot(p, mv_ref[:, sl]))
    o_mem = jnp.concatenate(heads, axis=-1).astype(BF16)
    branches = (
        _dot(ona_ref[...], wna_ref[...]),
        _dot(odf_ref[...], wdf_ref[...]),
        _dot(o_mem, wmem_ref[...]),
    )
    merged = jnp.zeros(x.shape, F32)
    for i, y in enumerate(branches):
        sl = slice(i * D_MODEL, (i + 1) * D_MODEL)
        gate = jax.nn.sigmoid(_dot(h, wgate_ref[:, sl]) + bgate_ref[:, sl])
        merged = merged + gate * y
    x2 = x + _dot(merged.astype(BF16), wout_ref[...])
    y = _swiglu_half_step(x2, fg_ref, wg_ref, wu_ref, wd_ref)
    if final_norm:
        y = _rmsnorm_f32(y, ng_ref[...])
    o_ref[...] = y


def _merge_ffn(x1, proj, o_na, o_diff, mk, mv, norm_g, w_gate, b_gate, w_na, w_df, w_mem, w_out,
               ffn_g, wg, wu, wd, final_g, seq, m_tokens, *, final_norm):
    n = x1.shape[0]
    per_b = seq // MERGE_TM
    row = lambda i: (i, 0)
    return pl.pallas_call(
        functools.partial(_merge_ffn_body, final_norm=final_norm),
        grid=(n // MERGE_TM,),
        in_specs=[
            pl.BlockSpec((MERGE_TM, D_MODEL), row),
            pl.BlockSpec((MERGE_TM, MEM_WIDTH), lambda i: (i, O_MQ // MEM_WIDTH)),
            pl.BlockSpec((MERGE_TM, NA_WIDTH), row),
            pl.BlockSpec((MERGE_TM, DIFF_V_WIDTH), row),
            pl.BlockSpec((m_tokens, MEM_WIDTH), lambda i: (i // per_b, 0)),
            pl.BlockSpec((m_tokens, MEM_WIDTH), lambda i: (i // per_b, 0)),
            _resident((1, D_MODEL)),
            _HBM,
            _resident((1, 3 * D_MODEL)),
            _HBM, _HBM, _HBM, _HBM,
            _resident((1, D_MODEL)),
            _HBM, _HBM, _HBM,
            _resident((1, D_MODEL)),
        ],
        out_specs=pl.BlockSpec((MERGE_TM, D_MODEL), row),
        out_shape=jax.ShapeDtypeStruct((n, D_MODEL), F32),
        scratch_shapes=[
            pltpu.VMEM((D_MODEL, 3 * D_MODEL), BF16),
            pltpu.VMEM((NA_WIDTH, D_MODEL), BF16), pltpu.VMEM((DIFF_V_WIDTH, D_MODEL), BF16),
            pltpu.VMEM((MEM_WIDTH, D_MODEL), BF16), pltpu.VMEM((D_MODEL, D_MODEL), BF16),
            pltpu.VMEM((D_MODEL, D_FF), BF16), pltpu.VMEM((D_MODEL, D_FF), BF16),
            pltpu.VMEM((D_FF, D_MODEL), BF16),
            pltpu.VMEM((STAGE_SLOTS, STAGE_ROWS,3 * D_MODEL), F32),
            pltpu.SemaphoreType.DMA((STAGE_SLOTS,)),
        ],
        compiler_params=pltpu.CompilerParams(
            dimension_semantics=("arbitrary",), vmem_limit_bytes=V7X_VMEM_LIMIT_BYTES),
        name="merge_ffn",
    )(x1, proj, o_na, o_diff, mk, mv, norm_g, w_gate, b_gate, w_na, w_df, w_mem, w_out,
      ffn_g, wg, wu, wd, final_g)


def kernel(x, mem, ffn1_norm, ffn1_w_gate, ffn1_w_up, ffn1_w_down, mix_norm, w_in, na_rpb,
           diff_lambda_q1, diff_lambda_k1, diff_lambda_q2, diff_lambda_k2, diff_subln,
           mem_norm, w_mem_kv, w_gate, b_gate, w_br_na, w_br_diff, w_br_mem, w_out,
           ffn2_norm, ffn2_w_gate, ffn2_w_up, ffn2_w_down, final_norm):
    batch, seq, d_model = x.shape
    m_tokens = mem.shape[1]
    depth = ffn1_norm.shape[0]
    assert d_model == D_MODEL and seq % GRID_W == 0
    assert seq % DIFF_TQ == 0 and seq % MERGE_TM == 0 and (batch * seq) % FFN_TM == 0
    slopes = jnp.asarray([2.0 ** (-8.0 * (i + 1) / DIFF_HEADS) for i in range(DIFF_HEADS)], F32)
    w32 = lambda w: w.astype(F32)
    vec = lambda v: v.reshape(1, -1).astype(F32)

    xt = x.reshape(batch * seq, d_model)
    mem2d = mem.reshape(batch * m_tokens, d_model)
    for l in range(depth):
        lam_init = 0.8 - 0.6 * math.exp(-0.3 * l)
        xt, proj = _ffn_proj(xt, vec(ffn1_norm[l]), w32(ffn1_w_gate[l]), w32(ffn1_w_up[l]),
                             w32(ffn1_w_down[l]), vec(mix_norm[l]), w32(w_in[l]))
        mk, mv = _memkv(mem2d, vec(mem_norm[l]), w32(w_mem_kv[l]))
        o_na = _na(proj, na_rpb[l].reshape(-1).astype(F32), batch, seq)
        o_diff = _diff(proj, slopes, vec(diff_lambda_q1[l]), vec(diff_lambda_k1[l]),
                       vec(diff_lambda_q2[l]), vec(diff_lambda_k2[l]), vec(diff_subln[l]),
                       batch, seq, lam_init)
        xt = _merge_ffn(xt, proj, o_na, o_diff, mk, mv, vec(mix_norm[l]), w32(w_gate[l]),
                        vec(b_gate[l]), w32(w_br_na[l]), w32(w_br_diff[l]), w32(w_br_mem[l]),
                        w32(w_out[l]), vec(ffn2_norm[l]), w32(ffn2_w_gate[l]), w32(ffn2_w_up[l]),
                        w32(ffn2_w_down[l]), vec(final_norm), seq, m_tokens,
                        final_norm=(l == depth - 1))
    return xt.reshape(batch, seq, d_model)
```

```python
import functools
import math

import jax
import jax.numpy as jnp
from jax import lax
from jax.experimental import pallas as pl
from jax.experimental.pallas import tpu as pltpu

F32 = jnp.float32
BF16 = jnp.bfloat16

D_MODEL = 1024
GRID_W = 64
NA_HEADS = 8
NA_HEAD_DIM = 64
NA_WIN_ROWS = 8
NA_WIN_COLS = 16
DIFF_HEADS = 4
DIFF_QK_DIM = 64
DIFF_V_DIM = 128
MEM_HEADS = 4
MEM_HEAD_DIM = 128
D_FF = 2816
NORM_EPS = 1e-6
NA_WIDTH = NA_HEADS * NA_HEAD_DIM
DIFF_QK_WIDTH = DIFF_HEADS * 2 * DIFF_QK_DIM
DIFF_V_WIDTH = DIFF_HEADS * DIFF_V_DIM
MEM_WIDTH = MEM_HEADS * MEM_HEAD_DIM
IN_WIDTH = 3 * NA_WIDTH + 2 * DIFF_QK_WIDTH + DIFF_V_WIDTH + MEM_WIDTH
O_NQ = 0
O_NK = O_NQ + NA_WIDTH
O_NV = O_NK + NA_WIDTH
O_DQ = O_NV + NA_WIDTH
O_DK = O_DQ + DIFF_QK_WIDTH
O_DV = O_DK + DIFF_QK_WIDTH
O_MQ = O_DV + DIFF_V_WIDTH

LANES = 128
V7X_VMEM_LIMIT_BYTES = 56 * 1024 * 1024

FFN_TM = 512
STAGE_ROWS = 128
STAGE_SLOTS = 4
FFN_PROJ_STAGE_SLOTS = 6
FFN_TF = 256
PROJ_TN = 512
DIFF_TQ = 512
DIFF_BATCH_CHUNK = 4
MERGE_TM = 512
MEMKV_TM = 1024
NEG_BIG = -1e30


def _rmsnorm_f32(x, g):
    ms = jnp.mean(x * x, axis=-1, keepdims=True)
    return (x * lax.rsqrt(ms + NORM_EPS)) * g


def _softmax_rows(logits):
    m = jnp.max(logits, axis=-1, keepdims=True)
    e = jnp.exp(logits - m)
    s = jnp.sum(e, axis=-1, keepdims=True)
    return e * (1.0 / s)


def _dot_nt(a, b):
    return lax.dot_general(a, b, (((1,), (1,)), ((), ())), preferred_element_type=F32)


def _dot(a, b):
    return jnp.dot(a, b, preferred_element_type=F32)


def _resident(shape):
    nd = len(shape)
    return pl.BlockSpec(shape, lambda *_: (0,) * nd, pipeline_mode=pl.Buffered(1))


def _aligned(idx, multiple):
    return idx if isinstance(idx, int) else pl.multiple_of(idx, multiple)


def _software_pipeline(n_items, stages, order):
    depth = len(stages)
    assert n_items >= depth and sorted(order) == list(range(depth))

    def trip(t, parity, valid):
        for k in order:
            if valid(k):
                stages[k](t - k, (parity - k) % 2)

    for t in range(depth - 1):
        trip(t, t % 2, lambda k, t=t: k <= t)
    start = depth - 1
    if (n_items - start) % 2:
        trip(start, start % 2, lambda k: True)
        start += 1

    def body(u, carry):
        t = start + 2 * u
        trip(t, start % 2, lambda k: True)
        trip(t + 1, (start + 1) % 2, lambda k: True)
        return carry

    lax.fori_loop(0, (n_items - start) // 2, body, 0)
    for t in range(n_items, n_items + depth - 1):
        trip(t, t % 2, lambda k, t=t: t - k < n_items)


def _stage_copy(piece, stage_ref, sem_ref, slot):
    w_hbm, _, chunk, lane = piece
    return pltpu.make_async_copy(
        w_hbm.at[pl.ds(chunk * STAGE_ROWS, STAGE_ROWS), :],
        stage_ref.at[slot, :, lane:lane + w_hbm.shape[1]],
        sem_ref.at[slot])


def _load_weights_bf16(pairs, stage_ref, sem_ref):
    n_slots, _, slot_width = stage_ref.shape
    groups = []
    for w, dst in pairs:
        per_slot = slot_width // w.shape[1]
        chunks = list(range(w.shape[0] // STAGE_ROWS))
        for g in range(0, len(chunks), per_slot):
            groups.append([(w, dst, c, k * w.shape[1])
                           for k, c in enumerate(chunks[g:g + per_slot])])

    def start(n):
        for piece in groups[n]:
            _stage_copy(piece, stage_ref, sem_ref, n % n_slots).start()

    ahead = n_slots - 1
    for n in range(min(ahead, len(groups))):
        start(n)
    for n, group in enumerate(groups):
        slot = n % n_slots
        if n + ahead < len(groups):
            start(n + ahead)
        for piece in group:
            _stage_copy(piece, stage_ref, sem_ref, slot).wait()
        for w, dst, c, lane in group:
            dst[c * STAGE_ROWS:(c + 1) * STAGE_ROWS, :] = (
                stage_ref[slot, :, lane:lane + w.shape[1]].astype(BF16))


_HBM = pl.BlockSpec(memory_space=pl.ANY)


def _swiglu_half_step(x, g_ref, wg_ref, wu_ref, wd_ref):
    h = _rmsnorm_f32(x, g_ref[...]).astype(BF16)
    acc = jnp.zeros(x.shape, F32)
    for c in range(D_FF // FFN_TF):
        sl = slice(c * FFN_TF, (c + 1) * FFN_TF)
        gate = _dot(h, wg_ref[:, sl])
        up = _dot(h, wu_ref[:, sl])
        act = (gate * jax.nn.sigmoid(gate) * up).astype(BF16)
        acc = acc + _dot(act, wd_ref[sl, :])
    return x + 0.5 * acc


def _ffn_proj_body(x_ref, g_ref, wg_hbm, wu_hbm, wd_hbm, pg_ref, win_hbm, x1_ref, proj_ref,
                   wg_ref, wu_ref, wd_ref, win_ref, stage_ref, sem_ref):
    @pl.when(pl.program_id(0) == 0)
    def _load_weights():
        _load_weights_bf16(((wg_hbm, wg_ref), (wu_hbm, wu_ref), (wd_hbm, wd_ref),
                            (win_hbm, win_ref)), stage_ref, sem_ref)

    x1 = _swiglu_half_step(x_ref[...], g_ref, wg_ref, wu_ref, wd_ref)
    x1_ref[...] = x1
    h = _rmsnorm_f32(x1, pg_ref[...]).astype(BF16)
    for c in range(IN_WIDTH // PROJ_TN):
        lo = c * PROJ_TN
        y = _dot(h, win_ref[:, lo:lo + PROJ_TN])
        if lo == O_NQ or lo == O_DQ:
            y = y * (NA_HEAD_DIM ** -0.5)
        proj_ref[:, lo:lo + PROJ_TN] = y.astype(BF16)


def _ffn_proj(x, norm_g, wg, wu, wd, mix_g, w_in):
    n = x.shape[0]
    row = lambda i: (i, 0)
    return pl.pallas_call(
        _ffn_proj_body,
        grid=(n // FFN_TM,),
        in_specs=[
            pl.BlockSpec((FFN_TM, D_MODEL), row),
            _resident((1, D_MODEL)),
            _HBM, _HBM, _HBM,
            _resident((1, D_MODEL)),
            _HBM,
        ],
        out_specs=[pl.BlockSpec((FFN_TM, D_MODEL), row), pl.BlockSpec((FFN_TM, IN_WIDTH), row)],
        out_shape=[jax.ShapeDtypeStruct((n, D_MODEL), F32),
                   jax.ShapeDtypeStruct((n, IN_WIDTH), BF16)],
        scratch_shapes=[
            pltpu.VMEM((D_MODEL, D_FF), BF16), pltpu.VMEM((D_MODEL, D_FF), BF16),
            pltpu.VMEM((D_FF, D_MODEL), BF16), pltpu.VMEM((D_MODEL, IN_WIDTH), BF16),
            pltpu.VMEM((FFN_PROJ_STAGE_SLOTS, STAGE_ROWS, IN_WIDTH), F32),
            pltpu.SemaphoreType.DMA((FFN_PROJ_STAGE_SLOTS,)),
        ],
        compiler_params=pltpu.CompilerParams(
            dimension_semantics=("arbitrary",), vmem_limit_bytes=V7X_VMEM_LIMIT_BYTES),
        name="ffn_proj",
    )(x, norm_g, wg, wu, wd, mix_g, w_in)


def _memkv_body(m_ref, g_ref, w_hbm, k_ref, v_ref, w_ref, stage_ref, sem_ref):
    @pl.when(pl.program_id(0) == 0)
    def _load_weights():
        _load_weights_bf16(((w_hbm, w_ref),), stage_ref, sem_ref)

    h = _rmsnorm_f32(m_ref[...], g_ref[...]).astype(BF16)
    k_ref[...] = _dot(h, w_ref[:, :MEM_WIDTH]).astype(BF16)
    v_ref[...] = _dot(h, w_ref[:, MEM_WIDTH:]).astype(BF16)


def _memkv(mem2d, norm_g, w_kv):
    n = mem2d.shape[0]
    tm = math.gcd(n, MEMKV_TM)
    row = lambda i: (i, 0)
    shp = jax.ShapeDtypeStruct((n, MEM_WIDTH), BF16)
    return pl.pallas_call(
        _memkv_body,
        grid=(n // tm,),
        in_specs=[
            pl.BlockSpec((tm, D_MODEL), row),
            _resident((1, D_MODEL)),
            _HBM,
        ],
        out_specs=[pl.BlockSpec((tm, MEM_WIDTH), row)] * 2,
        out_shape=[shp, shp],
        scratch_shapes=[
            pltpu.VMEM((D_MODEL, 2 * MEM_WIDTH), BF16),
            pltpu.VMEM((STAGE_SLOTS, STAGE_ROWS, 2 * MEM_WIDTH), F32),
            pltpu.SemaphoreType.DMA((STAGE_SLOTS,)),
        ],
        compiler_params=pltpu.CompilerParams(dimension_semantics=("arbitrary",)),
        name="memkv",
    )(mem2d, norm_g, w_kv)


RPB_ROWS = 2 * NA_WIN_ROWS - 1
RPB_COLS = 2 * NA_WIN_COLS - 1
NA_BAND_KEYS = NA_WIN_ROWS * GRID_W
NA_BATCH_CHUNK = 8
NA_ITEM_ROWS = 16


def _na_body(rpb_ref, q_ref, k_ref, v_ref, o_ref, tile_ref, band_ref,
             lg0_ref, lg1_ref, m0_ref, m1_ref, a0_ref, a1_ref, *, rows, nbatch):
    lg_ref, m_ref, a_ref = (lg0_ref, lg1_ref), (m0_ref, m1_ref), (a0_ref, a1_ref)
    hp = pl.program_id(0)
    b = pl.program_id(1)
    wr = min(NA_WIN_ROWS, rows)

    @pl.when(b == 0)
    def _build_bias():
        c = lax.broadcasted_iota(jnp.int32, (GRID_W, LANES), 0)
        kc = lax.broadcasted_iota(jnp.int32, (GRID_W, LANES), 1) & (GRID_W - 1)
        cs = jnp.clip(c - NA_WIN_COLS // 2, 0, GRID_W - NA_WIN_COLS)
        valid = (kc >= cs) & (kc < cs + NA_WIN_COLS)
        dc_lane = lax.broadcasted_iota(jnp.int32, (8, LANES), 1) & (GRID_W - 1)

        def tile_step(t, carry):
            hl = t // RPB_ROWS
            dr = t - hl * RPB_ROWS
            base = ((hp * 2 + hl) * RPB_ROWS + dr) * RPB_COLS
            vec = jnp.zeros((8, LANES), F32)
            for dc in range(RPB_COLS):
                vec = jnp.where(dc_lane == dc, rpb_ref[base + dc], vec)
            rows8 = jnp.concatenate([vec] * (GRID_W // 8), axis=0)
            tile = pltpu.roll(rows8, LANES - (NA_WIN_COLS - 1), 1, stride=1, stride_axis=0)
            tile_ref[hl, dr] = jnp.where(valid, tile, NEG_BIG)
            return carry

        lax.fori_loop(0, 2 * RPB_ROWS, tile_step, 0, unroll=True)
        low = lax.broadcasted_iota(jnp.int32, (GRID_W, LANES), 1) < GRID_W
        for hl in range(2):
            for dr0 in range(NA_WIN_ROWS):
                for j in range(wr // 2):
                    band_ref[dr0, hl * GRID_W:(hl + 1) * GRID_W, j * LANES:(j + 1) * LANES] = (
                        jnp.where(low, tile_ref[hl, dr0 + 2 * j], tile_ref[hl, dr0 + 2 * j + 1]))

    lane = lax.broadcasted_iota(jnp.int32, (GRID_W, LANES), 1)
    low = lane < NA_HEAD_DIM

    items_per_batch = rows // NA_ITEM_ROWS
    ipb_shift = items_per_batch.bit_length() - 1

    def locate(item, rr):
        if isinstance(item, int):
            bl, it = divmod(item, items_per_batch)
        else:
            bl, it = lax.shift_right_logical(item, ipb_shift), item & (items_per_batch - 1)
        return it * NA_ITEM_ROWS + rr, bl * (rows * GRID_W)

    def band_rows(r, base):
        if isinstance(r, int):
            rs = min(max(r - wr // 2, 0), rows - wr)
        else:
            rs = jnp.clip(r - wr // 2, 0, rows - wr)
        return rs, pl.ds(_aligned(base + rs * GRID_W, GRID_W), wr * GRID_W)

    def query_rows(r, base):
        return pl.ds(_aligned(base + r * GRID_W, GRID_W), GRID_W)

    def logits_stage(item, slot):
        for rr in range(NA_ITEM_ROWS):
            r, base = locate(item, rr)
            rs, keys = band_rows(r, base)
            q2 = q_ref[query_rows(r, base), :]
            zero = jnp.zeros_like(q2)
            q_st = jnp.concatenate([jnp.where(low, q2, zero), jnp.where(low, zero, q2)], axis=0)
            lg = _dot_nt(q_st, k_ref[keys, :]) + band_ref[rs - r + (NA_WIN_ROWS - 1)]
            lg_ref[slot][rr] = lg
            lane_max = lg[:, :LANES]
            for c in range(1, NA_BAND_KEYS // LANES):
                lane_max = jnp.maximum(lane_max, lg[:, c * LANES:(c + 1) * LANES])
            m_ref[slot][rr] = lane_max

    def value_stage(item, slot):
        for rr in range(NA_ITEM_ROWS):
            r, base = locate(item, rr)
            _, keys = band_rows(r, base)
            row_max = jnp.max(m_ref[slot][rr], axis=-1, keepdims=True)
            e = jnp.exp(lg_ref[slot][rr] - row_max)
            lane_sum = e[:, :LANES]
            for c in range(1, NA_BAND_KEYS // LANES):
                lane_sum = lane_sum + e[:, c * LANES:(c + 1) * LANES]
            a_ref[slot][rr, :, :LANES] = _dot(e.astype(BF16), v_ref[keys, :])
            a_ref[slot][rr, :, LANES:] = lane_sum

    def finish_stage(item, slot):
        for rr in range(NA_ITEM_ROWS):
            r, base = locate(item, rr)
            acc = a_ref[slot][rr]
            o_st = acc[:, :LANES] * (1.0 / jnp.sum(acc[:, LANES:], axis=-1, keepdims=True))
            o = jnp.where(low, o_st[:GRID_W], o_st[GRID_W:])
            o_ref[query_rows(r, base), :] = o.astype(BF16)

    _software_pipeline(nbatch * items_per_batch, (logits_stage, value_stage, finish_stage),
                       order=(1, 0, 2))


def _na(proj, rpb_flat, batch, seq):
    rows = seq // GRID_W
    items_per_batch = rows // NA_ITEM_ROWS
    nbatch = math.gcd(batch, NA_BATCH_CHUNK)
    assert rows >= NA_WIN_ROWS and NA_WIN_ROWS % 2 == 0 and rows % NA_ITEM_ROWS == 0
    assert items_per_batch & (items_per_batch - 1) == 0
    pairs = NA_HEADS // 2
    blk = lambda off: pl.BlockSpec((nbatch * seq, LANES), lambda hp, b, off=off: (b, off + hp))
    return pl.pallas_call(
        functools.partial(_na_body, rows=rows, nbatch=nbatch),
        grid=(pairs, batch // nbatch),
        in_specs=[
            pl.BlockSpec(memory_space=pltpu.SMEM),
            blk(O_NQ // LANES), blk(O_NK // LANES), blk(O_NV // LANES),
        ],
        out_specs=pl.BlockSpec((nbatch * seq, LANES), lambda hp, b: (b, hp)),
        out_shape=jax.ShapeDtypeStruct((batch * seq, NA_WIDTH), BF16),
        scratch_shapes=[
            pltpu.VMEM((2, RPB_ROWS, GRID_W, LANES), F32),
            pltpu.VMEM((NA_WIN_ROWS, 2 * GRID_W, NA_BAND_KEYS), F32),
            pltpu.VMEM((NA_ITEM_ROWS, 2 * GRID_W, NA_BAND_KEYS), F32),
            pltpu.VMEM((NA_ITEM_ROWS, 2 * GRID_W, NA_BAND_KEYS), F32),
            pltpu.VMEM((NA_ITEM_ROWS, 2 * GRID_W, LANES), F32),
            pltpu.VMEM((NA_ITEM_ROWS, 2 * GRID_W, LANES), F32),
            pltpu.VMEM((NA_ITEM_ROWS, 2 * GRID_W, 2 * LANES), F32),
            pltpu.VMEM((NA_ITEM_ROWS, 2 * GRID_W, 2 * LANES), F32),
        ],
        compiler_params=pltpu.CompilerParams(
            dimension_semantics=("arbitrary", "arbitrary"), vmem_limit_bytes=V7X_VMEM_LIMIT_BYTES),
        name="na_attn",
    )(rpb_flat, proj, proj, proj)


def _diff_body(slope_ref, lq1_ref, lk1_ref, lq2_ref, lk2_ref, sub_ref, q_ref, k_ref, v_ref, o_ref,
               dist_ref, v1_ref, kt_ref, s0_ref, s1_ref, m0_ref, m1_ref, a0_ref, a1_ref,
               *, lam_init, seq, nbatch):
    s_ref, m_ref, a_ref = (s0_ref, s1_ref), (m0_ref, m1_ref), (a0_ref, a1_ref)
    h = pl.program_id(0)
    b = pl.program_id(1)
    nq = seq // DIFF_TQ

    @pl.when(b == 0)
    def _build_dist():
        slope = slope_ref[h]
        rel = (lax.broadcasted_iota(jnp.int32, (DIFF_TQ, DIFF_TQ), 0)
               - lax.broadcasted_iota(jnp.int32, (DIFF_TQ, DIFF_TQ), 1))
        for d in range(2 * nq - 1):
            off = (d - (nq - 1)) * DIFF_TQ
            dist_ref[d] = slope * jnp.abs((rel + off).astype(F32))

    lam = (jnp.exp(jnp.sum(lq1_ref[...] * lk1_ref[...], axis=-1, keepdims=True))
           - jnp.exp(jnp.sum(lq2_ref[...] * lk2_ref[...], axis=-1, keepdims=True))
           + lam_init)
    lane = lax.broadcasted_iota(jnp.int32, (DIFF_TQ, LANES), 1)
    first_map = lane < DIFF_QK_DIM
    nq_shift = nq.bit_length() - 1

    def q_rows(j):
        return pl.ds(_aligned(j * DIFF_TQ, DIFF_TQ), DIFF_TQ)

    def batch_rows(j):
        bl = j // nq if isinstance(j, int) else lax.shift_right_logical(j, nq_shift)
        return pl.ds(_aligned(bl * seq, seq), seq)

    def logits_stage(j, slot):
        i = j % nq if isinstance(j, int) else j & (nq - 1)
        q = q_ref[q_rows(j), :]
        zero = jnp.zeros_like(q)
        k_t = kt_ref[j // nq if isinstance(j, int) else lax.shift_right_logical(j, nq_shift)]
        dist = jnp.concatenate([dist_ref[i - c + (nq - 1)] for c in range(nq)], axis=1)
        for m, qm in enumerate((jnp.where(first_map, q, zero), jnp.where(first_map, zero, q))):
            s = _dot(qm, k_t) - dist
            s_ref[slot][m] = s
            lane_max = s[:, :LANES]
            for c in range(1, seq // LANES):
                lane_max = jnp.maximum(lane_max, s[:, c * LANES:(c + 1) * LANES])
            m_ref[slot][m] = lane_max

    def value_stage(j, slot):
        e = jnp.concatenate(
            [jnp.exp(s_ref[slot][m] - jnp.max(m_ref[slot][m], axis=-1, keepdims=True)).astype(BF16)
             for m in range(2)], axis=0)
        a_ref[slot][...] = _dot(e, v1_ref[batch_rows(j), :])

    def finish_stage(j, slot):
        acc = a_ref[slot][...]
        num1, den1 = acc[:DIFF_TQ, :DIFF_V_DIM], acc[:DIFF_TQ, DIFF_V_DIM:DIFF_V_DIM + 1]
        num2, den2 = acc[DIFF_TQ:, :DIFF_V_DIM], acc[DIFF_TQ:, DIFF_V_DIM:DIFF_V_DIM + 1]
        o = num1 * (1.0 / den1) - (lam * (1.0 / den2)) * num2
        o = _rmsnorm_f32(o, sub_ref[...]) * (1.0 - lam_init)
        o_ref[q_rows(j), :] = o.astype(BF16)

    v = v_ref[...]
    v1_ref[:, :DIFF_V_DIM] = v
    v1_ref[:, DIFF_V_DIM:] = jnp.where(
        lax.broadcasted_iota(jnp.int32, v.shape, 1) == 0, 1.0, 0.0).astype(BF16)
    for bl in range(nbatch):
        kt_ref[bl] = k_ref[bl * seq:(bl + 1) * seq, :].T
    _software_pipeline(nbatch * nq, (logits_stage, value_stage, finish_stage), order=(1, 0, 2))


def _diff(proj, slopes, lq1, lk1, lq2, lk2, subln, batch, seq, lam_init):
    nq = seq // DIFF_TQ
    nbatch = math.gcd(batch, DIFF_BATCH_CHUNK)
    assert nq & (nq - 1) == 0
    vec = lambda w: pl.BlockSpec((1, w), lambda h, b: (0, 0))
    blk = lambda off: pl.BlockSpec((nbatch * seq, LANES), lambda h, b, off=off: (b, off + h))
    return pl.pallas_call(
        functools.partial(_diff_body, lam_init=lam_init, seq=seq, nbatch=nbatch),
        grid=(DIFF_HEADS, batch // nbatch),
        in_specs=[
            pl.BlockSpec(memory_space=pltpu.SMEM),
            vec(DIFF_QK_DIM), vec(DIFF_QK_DIM), vec(DIFF_QK_DIM), vec(DIFF_QK_DIM), vec(DIFF_V_DIM),
            blk(O_DQ // LANES), blk(O_DK // LANES), blk(O_DV // LANES),
        ],
        out_specs=pl.BlockSpec((nbatch * seq, LANES), lambda h, b: (b, h)),
        out_shape=jax.ShapeDtypeStruct((batch * seq, DIFF_V_WIDTH), BF16),
        scratch_shapes=[
            pltpu.VMEM((2 * nq - 1, DIFF_TQ, DIFF_TQ), F32),
            pltpu.VMEM((nbatch * seq, 2 * LANES), BF16),
            pltpu.VMEM((nbatch, LANES, seq), BF16),
            pltpu.VMEM((2, DIFF_TQ, seq), F32), pltpu.VMEM((2, DIFF_TQ, seq), F32),
            pltpu.VMEM((2, DIFF_TQ, LANES), F32), pltpu.VMEM((2, DIFF_TQ, LANES), F32),
            pltpu.VMEM((2 * DIFF_TQ, 2 * LANES), F32), pltpu.VMEM((2 * DIFF_TQ, 2 * LANES), F32),
        ],
        compiler_params=pltpu.CompilerParams(
            dimension_semantics=("arbitrary", "arbitrary"),
            vmem_limit_bytes=V7X_VMEM_LIMIT_BYTES),
        name="diff_attn",
    )(slopes, lq1, lk1, lq2, lk2, subln, proj, proj, proj)


def _merge_ffn_body(x_ref, mq_ref, ona_ref, odf_ref, mk_ref, mv_ref, g_ref, wgate_hbm, bgate_ref,
                    wna_hbm, wdf_hbm, wmem_hbm, wout_hbm, fg_ref, wg_hbm, wu_hbm, wd_hbm, ng_ref,
                    o_ref, wgate_ref, wna_ref, wdf_ref, wmem_ref, wout_ref, wg_ref, wu_ref, wd_ref,
                    stage_ref, sem_ref, *, final_norm):
    @pl.when(pl.program_id(0) == 0)
    def _load_weights():
        _load_weights_bf16(
            ((wgate_hbm, wgate_ref), (wna_hbm, wna_ref), (wdf_hbm, wdf_ref), (wmem_hbm, wmem_ref),
             (wout_hbm, wout_ref), (wg_hbm, wg_ref), (wu_hbm, wu_ref), (wd_hbm, wd_ref)),
            stage_ref, sem_ref)

    x = x_ref[...]
    h = _rmsnorm_f32(x, g_ref[...]).astype(BF16)
    heads = []
    for hh in range(MEM_HEADS):
        sl = slice(hh * MEM_HEAD_DIM, (hh + 1) * MEM_HEAD_DIM)
        logits = _dot_nt(mq_ref[:, sl], mk_ref[:, sl]) * (MEM_HEAD_DIM ** -0.5)
        p = _softmax_rows(logits).astype(BF16)
        heads.append(_dot(p, mv_ref[:, sl]))
    o_mem = jnp.concatenate(heads, axis=-1).astype(BF16)
    branches = (
        _dot(ona_ref[...], wna_ref[...]),
        _dot(odf_ref[...], wdf_ref[...]),
        _dot(o_mem, wmem_ref[...]),
    )
    merged = jnp.zeros(x.shape, F32)
    for i, y in enumerate(branches):
        sl = slice(i * D_MODEL, (i + 1) * D_MODEL)
        gate = jax.nn.sigmoid(_dot(h, wgate_ref[:, sl]) + bgate_ref[:, sl])
        merged = merged + gate * y
    x2 = x + _dot(merged.astype(BF16), wout_ref[...])
    y = _swiglu_half_step(x2, fg_ref, wg_ref, wu_ref, wd_ref)
    if final_norm:
        y = _rmsnorm_f32(y, ng_ref[...])
    o_ref[...] = y


def _merge_ffn(x1, proj, o_na, o_diff, mk, mv, norm_g, w_gate, b_gate, w_na, w_df, w_mem, w_out,
               ffn_g, wg, wu, wd, final_g, seq, m_tokens, *, final_norm):
    n = x1.shape[0]
    per_b = seq // MERGE_TM
    row = lambda i: (i, 0)
    return pl.pallas_call(
        functools.partial(_merge_ffn_body, final_norm=final_norm),
        grid=(n // MERGE_TM,),
        in_specs=[
            pl.BlockSpec((MERGE_TM, D_MODEL), row),
            pl.BlockSpec((MERGE_TM, MEM_WIDTH), lambda i: (i, O_MQ // MEM_WIDTH)),
            pl.BlockSpec((MERGE_TM, NA_WIDTH), row),
            pl.BlockSpec((MERGE_TM, DIFF_V_WIDTH), row),
            pl.BlockSpec((m_tokens, MEM_WIDTH), lambda i: (i // per_b, 0)),
            pl.BlockSpec((m_tokens, MEM_WIDTH), lambda i: (i // per_b, 0)),
            _resident((1, D_MODEL)),
            _HBM,
            _resident((1, 3 * D_MODEL)),
            _HBM, _HBM, _HBM, _HBM,
            _resident((1, D_MODEL)),
            _HBM, _HBM, _HBM,
            _resident((1, D_MODEL)),
        ],
        out_specs=pl.BlockSpec((MERGE_TM, D_MODEL), row),
        out_shape=jax.ShapeDtypeStruct((n, D_MODEL), F32),
        scratch_shapes=[
            pltpu.VMEM((D_MODEL, 3 * D_MODEL), BF16),
            pltpu.VMEM((NA_WIDTH, D_MODEL), BF16), pltpu.VMEM((DIFF_V_WIDTH, D_MODEL), BF16),
            pltpu.VMEM((MEM_WIDTH, D_MODEL), BF16), pltpu.VMEM((D_MODEL, D_MODEL), BF16),
            pltpu.VMEM((D_MODEL, D_FF), BF16), pltpu.VMEM((D_MODEL, D_FF), BF16),
            pltpu.VMEM((D_FF, D_MODEL), BF16),
            pltpu.VMEM((STAGE_SLOTS, STAGE_ROWS, 3 * D_MODEL), F32),
            pltpu.SemaphoreType.DMA((STAGE_SLOTS,)),
        ],
        compiler_params=pltpu.CompilerParams(
            dimension_semantics=("arbitrary",), vmem_limit_bytes=V7X_VMEM_LIMIT_BYTES),
        name="merge_ffn",
    )(x1, proj, o_na, o_diff, mk, mv, norm_g, w_gate, b_gate, w_na, w_df, w_mem, w_out,
      ffn_g, wg, wu, wd, final_g)


def kernel(x, mem, ffn1_norm, ffn1_w_gate, ffn1_w_up, ffn1_w_down, mix_norm, w_in, na_rpb,
           diff_lambda_q1, diff_lambda_k1, diff_lambda_q2, diff_lambda_k2, diff_subln,
           mem_norm, w_mem_kv, w_gate, b_gate, w_br_na, w_br_diff, w_br_mem, w_out,
           ffn2_norm, ffn2_w_gate, ffn2_w_up, ffn2_w_down, final_norm):
    batch, seq, d_model = x.shape
    m_tokens = mem.shape[1]
    depth = ffn1_norm.shape[0]
    assert d_model == D_MODEL and seq % GRID_W == 0
    assert seq % DIFF_TQ == 0 and seq % MERGE_TM == 0 and (batch * seq) % FFN_TM == 0
    slopes = jnp.asarray([2.0 ** (-8.0 * (i + 1) / DIFF_HEADS) for i in range(DIFF_HEADS)], F32)
    w32 = lambda w: w.astype(F32)
    vec = lambda v: v.reshape(1, -1).astype(F32)

    xt = x.reshape(batch * seq, d_model)
    mem2d = mem.reshape(batch * m_tokens, d_model)
    for l in range(depth):
        lam_init = 0.8 - 0.6 * math.exp(-0.3 * l)
        xt, proj = _ffn_proj(xt, vec(ffn1_norm[l]), w32(ffn1_w_gate[l]), w32(ffn1_w_up[l]),
                             w32(ffn1_w_down[l]), vec(mix_norm[l]), w32(w_in[l]))
        mk, mv = _memkv(mem2d, vec(mem_norm[l]), w32(w_mem_kv[l]))
        o_na = _na(proj, na_rpb[l].reshape(-1).astype(F32), batch, seq)
        o_diff = _diff(proj, slopes, vec(diff_lambda_q1[l]), vec(diff_lambda_k1[l]),
                       vec(diff_lambda_q2[l]), vec(diff_lambda_k2[l]), vec(diff_subln[l]),
                       batch, seq, lam_init)
        xt = _merge_ffn(xt, proj, o_na, o_diff, mk, mv, vec(mix_norm[l]), w32(w_gate[l]),
                        vec(b_gate[l]), w32(w_br_na[l]), w32(w_br_diff[l]), w32(w_br_mem[l]),
                        w32(w_out[l]), vec(ffn2_norm[l]), w32(ffn2_w_gate[l]), w32(ffn2_w_up[l]),
                        w32(ffn2_w_down[l]), vec(final_norm), seq, m_tokens,
                        final_norm=(l == depth - 1))
    return xt.reshape(batch, seq, d_model)
```

```python
import functools
import math

import jax
import jax.numpy as jnp
from jax import lax
from jax.experimental import pallas as pl
from jax.experimental.pallas import tpu as pltpu

F32 = jnp.float32
BF16 = jnp.bfloat16

D_MODEL = 1024
GRID_W = 64
NA_HEADS = 8
NA_HEAD_DIM = 64
NA_WIN_ROWS = 8
NA_WIN_COLS = 16
DIFF_HEADS = 4
DIFF_QK_DIM = 64
DIFF_V_DIM = 128
MEM_HEADS = 4
MEM_HEAD_DIM = 128
D_FF = 2816
NORM_EPS = 1e-6
NA_WIDTH = NA_HEADS * NA_HEAD_DIM
DIFF_QK_WIDTH = DIFF_HEADS * 2 * DIFF_QK_DIM
DIFF_V_WIDTH = DIFF_HEADS * DIFF_V_DIM
MEM_WIDTH = MEM_HEADS * MEM_HEAD_DIM
IN_WIDTH = 3 * NA_WIDTH + 2 * DIFF_QK_WIDTH + DIFF_V_WIDTH + MEM_WIDTH
O_NQ = 0
O_NK = O_NQ + NA_WIDTH
O_NV = O_NK + NA_WIDTH
O_DQ = O_NV + NA_WIDTH
O_DK = O_DQ + DIFF_QK_WIDTH
O_DV = O_DK + DIFF_QK_WIDTH
O_MQ = O_DV + DIFF_V_WIDTH

LANES = 128
V7X_VMEM_LIMIT_BYTES = 56 * 1024 * 1024

FFN_TM = 512
STAGE_ROWS = 128
STAGE_SLOTS = 4
FFN_PROJ_STAGE_SLOTS = 6
FFN_TF = 256
PROJ_TN = 512
DIFF_TQ = 512
DIFF_BATCH_CHUNK = 4
MERGE_TM = 512
MEMKV_TM = 1024
NEG_BIG = -1e30


def _rmsnorm_f32(x, g):
    ms = jnp.mean(x * x, axis=-1, keepdims=True)
    return (x * lax.rsqrt(ms + NORM_EPS)) * g


def _softmax_rows(logits):
    m = jnp.max(logits, axis=-1, keepdims=True)
    e = jnp.exp(logits - m)
    s = jnp.sum(e, axis=-1, keepdims=True)
    return e * (1.0 / s)


def _dot_nt(a, b):
    return lax.dot_general(a, b, (((1,), (1,)), ((), ())), preferred_element_type=F32)


def _dot(a, b):
    return jnp.dot(a, b, preferred_element_type=F32)


def _resident(shape):
    nd = len(shape)
    return pl.BlockSpec(shape, lambda *_: (0,) * nd, pipeline_mode=pl.Buffered(1))


def _aligned(idx, multiple):
    return idx if isinstance(idx, int) else pl.multiple_of(idx, multiple)


def _software_pipeline(n_items, stages, order):
    depth = len(stages)
    assert n_items >= depth and sorted(order) == list(range(depth))

    def trip(t, parity, valid):
        for k in order:
            if valid(k):
                stages[k](t - k, (parity - k) % 2)

    for t in range(depth - 1):
        trip(t, t % 2, lambda k, t=t: k <= t)
    start = depth - 1
    if (n_items - start) % 2:
        trip(start, start % 2, lambda k: True)
        start += 1

    def body(u, carry):
        t = start + 2 * u
        trip(t, start % 2, lambda k: True)
        trip(t + 1, (start + 1) % 2, lambda k: True)
        return carry

    lax.fori_loop(0, (n_items - start) // 2, body, 0)
    for t in range(n_items, n_items + depth - 1):
        trip(t, t % 2, lambda k, t=t: t - k < n_items)


def _stage_copy(piece, stage_ref, sem_ref, slot):
    w_hbm, _, chunk, lane = piece
    return pltpu.make_async_copy(
        w_hbm.at[pl.ds(chunk * STAGE_ROWS, STAGE_ROWS), :],
        stage_ref.at[slot, :, lane:lane + w_hbm.shape[1]],
        sem_ref.at[slot])


def _load_weights_bf16(pairs, stage_ref, sem_ref):
    n_slots, _, slot_width = stage_ref.shape
    groups = []
    for w, dst in pairs:
        per_slot = slot_width // w.shape[1]
        chunks = list(range(w.shape[0] // STAGE_ROWS))
        for g in range(0, len(chunks), per_slot):
            groups.append([(w, dst, c, k * w.shape[1])
                           for k, c in enumerate(chunks[g:g + per_slot])])

    def start(n):
        for piece in groups[n]:
            _stage_copy(piece, stage_ref, sem_ref, n % n_slots).start()

    ahead = n_slots - 1
    for n in range(min(ahead, len(groups))):
        start(n)
    for n, group in enumerate(groups):
        slot = n % n_slots
        if n + ahead < len(groups):
            start(n + ahead)
        for piece in group:
            _stage_copy(piece, stage_ref, sem_ref, slot).wait()
        for w, dst, c, lane in group:
            dst[c * STAGE_ROWS:(c + 1) * STAGE_ROWS, :] = (
                stage_ref[slot, :, lane:lane + w.shape[1]].astype(BF16))


_HBM = pl.BlockSpec(memory_space=pl.ANY)


def _prenorm(x, g):
    rowscale = lax.rsqrt(jnp.mean(x * x, axis=-1, keepdims=True) + NORM_EPS)
    return (x * g).astype(BF16), rowscale


def _swiglu_half_step(x, g_ref, wg_ref, wu_ref, wd_ref):
    h, rowscale = _prenorm(x, g_ref[...])
    acc = jnp.zeros(x.shape, F32)
    for c in range(D_FF // FFN_TF):
        sl = slice(c * FFN_TF, (c + 1) * FFN_TF)
        gate = _dot(h, wg_ref[:, sl]) * rowscale
        up = _dot(h, wu_ref[:, sl]) * rowscale
        act = (gate * jax.nn.sigmoid(gate) * up).astype(BF16)
        acc = acc + _dot(act, wd_ref[sl, :])
    return x + 0.5 * acc


def _ffn_proj_body(x_ref, g_ref, wg_hbm, wu_hbm, wd_hbm, pg_ref, win_hbm, x1_ref, proj_ref,
                   wg_ref, wu_ref, wd_ref, win_ref, stage_ref, sem_ref):
    @pl.when(pl.program_id(0) == 0)
    def _load_weights():
        _load_weights_bf16(((wg_hbm, wg_ref), (wu_hbm, wu_ref), (wd_hbm, wd_ref),
                            (win_hbm, win_ref)), stage_ref, sem_ref)

    x1 = _swiglu_half_step(x_ref[...], g_ref, wg_ref, wu_ref, wd_ref)
    x1_ref[...] = x1
    h, rowscale = _prenorm(x1, pg_ref[...])
    for c in range(IN_WIDTH // PROJ_TN):
        lo = c * PROJ_TN
        scale = rowscale * (NA_HEAD_DIM ** -0.5) if lo in (O_NQ, O_DQ) else rowscale
        proj_ref[:, lo:lo + PROJ_TN] = (_dot(h, win_ref[:, lo:lo + PROJ_TN]) * scale).astype(BF16)


def _ffn_proj(x, norm_g, wg, wu, wd, mix_g, w_in):
    n = x.shape[0]
    row = lambda i: (i, 0)
    return pl.pallas_call(
        _ffn_proj_body,
        grid=(n // FFN_TM,),
        in_specs=[
            pl.BlockSpec((FFN_TM, D_MODEL), row),
            _resident((1, D_MODEL)),
            _HBM, _HBM, _HBM,
            _resident((1, D_MODEL)),
            _HBM,
        ],
        out_specs=[pl.BlockSpec((FFN_TM, D_MODEL), row), pl.BlockSpec((FFN_TM, IN_WIDTH), row)],
        out_shape=[jax.ShapeDtypeStruct((n, D_MODEL), F32),
                   jax.ShapeDtypeStruct((n, IN_WIDTH), BF16)],
        scratch_shapes=[
            pltpu.VMEM((D_MODEL, D_FF), BF16), pltpu.VMEM((D_MODEL, D_FF), BF16),
            pltpu.VMEM((D_FF, D_MODEL), BF16), pltpu.VMEM((D_MODEL, IN_WIDTH), BF16),
            pltpu.VMEM((FFN_PROJ_STAGE_SLOTS, STAGE_ROWS, IN_WIDTH), F32),
            pltpu.SemaphoreType.DMA((FFN_PROJ_STAGE_SLOTS,)),
        ],
        compiler_params=pltpu.CompilerParams(
            dimension_semantics=("arbitrary",), vmem_limit_bytes=V7X_VMEM_LIMIT_BYTES),
        name="ffn_proj",
    )(x, norm_g, wg, wu, wd, mix_g, w_in)


def _memkv_body(m_ref, g_ref, w_hbm, k_ref, v_ref, w_ref, stage_ref, sem_ref):
    @pl.when(pl.program_id(0) == 0)
    def _load_weights():
        _load_weights_bf16(((w_hbm, w_ref),), stage_ref, sem_ref)

    h = _rmsnorm_f32(m_ref[...], g_ref[...]).astype(BF16)
    k_ref[...] = _dot(h, w_ref[:, :MEM_WIDTH]).astype(BF16)
    v_ref[...] = _dot(h, w_ref[:, MEM_WIDTH:]).astype(BF16)


def _memkv(mem2d, norm_g, w_kv):
    n = mem2d.shape[0]
    tm = math.gcd(n, MEMKV_TM)
    row = lambda i: (i, 0)
    shp = jax.ShapeDtypeStruct((n, MEM_WIDTH), BF16)
    return pl.pallas_call(
        _memkv_body,
        grid=(n // tm,),
        in_specs=[
            pl.BlockSpec((tm, D_MODEL), row),
            _resident((1, D_MODEL)),
            _HBM,
        ],
        out_specs=[pl.BlockSpec((tm, MEM_WIDTH), row)] * 2,
        out_shape=[shp, shp],
        scratch_shapes=[
            pltpu.VMEM((D_MODEL, 2 * MEM_WIDTH), BF16),
            pltpu.VMEM((STAGE_SLOTS, STAGE_ROWS, 2 * MEM_WIDTH), F32),
            pltpu.SemaphoreType.DMA((STAGE_SLOTS,)),
        ],
        compiler_params=pltpu.CompilerParams(dimension_semantics=("arbitrary",)),
        name="memkv",
    )(mem2d, norm_g, w_kv)


RPB_ROWS = 2 * NA_WIN_ROWS - 1
RPB_COLS = 2 * NA_WIN_COLS - 1
NA_BAND_KEYS = NA_WIN_ROWS * GRID_W
NA_BATCH_CHUNK = 8
NA_ITEM_ROWS = 16


def _na_body(rpb_ref, q_ref, k_ref, v_ref, o_ref, tile_ref, band_ref,
             lg0_ref, lg1_ref, m0_ref, m1_ref, a0_ref, a1_ref, *, rows, nbatch):
    lg_ref, m_ref, a_ref = (lg0_ref, lg1_ref), (m0_ref, m1_ref), (a0_ref, a1_ref)
    hp = pl.program_id(0)
    b = pl.program_id(1)
    wr = min(NA_WIN_ROWS, rows)

    @pl.when(b == 0)
    def _build_bias():
        c = lax.broadcasted_iota(jnp.int32, (GRID_W, LANES), 0)
        kc = lax.broadcasted_iota(jnp.int32, (GRID_W, LANES), 1) & (GRID_W - 1)
        cs = jnp.clip(c - NA_WIN_COLS // 2, 0, GRID_W - NA_WIN_COLS)
        valid = (kc >= cs) & (kc < cs + NA_WIN_COLS)
        dc_lane = lax.broadcasted_iota(jnp.int32, (8, LANES), 1) & (GRID_W - 1)

        def tile_step(t, carry):
            hl = t // RPB_ROWS
            dr = t - hl * RPB_ROWS
            base = ((hp * 2 + hl) * RPB_ROWS + dr) * RPB_COLS
            vec = jnp.zeros((8, LANES), F32)
            for dc in range(RPB_COLS):
                vec = jnp.where(dc_lane == dc, rpb_ref[base + dc], vec)
            rows8 = jnp.concatenate([vec] * (GRID_W // 8), axis=0)
            tile = pltpu.roll(rows8, LANES - (NA_WIN_COLS - 1), 1, stride=1, stride_axis=0)
            tile_ref[hl, dr] = jnp.where(valid, tile, NEG_BIG)
            return carry

        lax.fori_loop(0, 2 * RPB_ROWS, tile_step, 0, unroll=True)
        low = lax.broadcasted_iota(jnp.int32, (GRID_W, LANES), 1) < GRID_W
        for hl in range(2):
            for dr0 in range(NA_WIN_ROWS):
                for j in range(wr // 2):
                    band_ref[dr0, hl * GRID_W:(hl + 1) * GRID_W, j * LANES:(j + 1) * LANES] = (
                        jnp.where(low, tile_ref[hl, dr0 + 2 * j], tile_ref[hl, dr0 + 2 * j + 1]))

    lane = lax.broadcasted_iota(jnp.int32, (GRID_W, LANES), 1)
    low = lane < NA_HEAD_DIM

    items_per_batch = rows // NA_ITEM_ROWS
    ipb_shift = items_per_batch.bit_length() - 1

    def locate(item, rr):
        if isinstance(item, int):
            bl, it = divmod(item, items_per_batch)
        else:
            bl, it = lax.shift_right_logical(item, ipb_shift), item & (items_per_batch - 1)
        return it * NA_ITEM_ROWS + rr, bl * (rows * GRID_W)

    def band_rows(r, base):
        if isinstance(r, int):
            rs = min(max(r - wr // 2, 0), rows - wr)
        else:
            rs = jnp.clip(r - wr // 2, 0, rows - wr)
        return rs, pl.ds(_aligned(base + rs * GRID_W, GRID_W), wr * GRID_W)

    def query_rows(r, base):
        return pl.ds(_aligned(base + r * GRID_W, GRID_W), GRID_W)

    def logits_stage(item, slot):
        for rr in range(NA_ITEM_ROWS):
            r, base = locate(item, rr)
            rs, keys = band_rows(r, base)
            q2 = q_ref[query_rows(r, base), :]
            zero = jnp.zeros_like(q2)
            q_st = jnp.concatenate([jnp.where(low, q2, zero), jnp.where(low, zero, q2)], axis=0)
            lg = _dot_nt(q_st, k_ref[keys, :]) + band_ref[rs - r + (NA_WIN_ROWS - 1)]
            lg_ref[slot][rr] = lg
            lane_max = lg[:, :LANES]
            for c in range(1, NA_BAND_KEYS // LANES):
                lane_max = jnp.maximum(lane_max, lg[:, c * LANES:(c + 1) * LANES])
            m_ref[slot][rr] = lane_max

    def value_stage(item, slot):
        for rr in range(NA_ITEM_ROWS):
            r, base = locate(item, rr)
            _, keys = band_rows(r, base)
            row_max = jnp.max(m_ref[slot][rr], axis=-1, keepdims=True)
            e = jnp.exp(lg_ref[slot][rr] - row_max)
            lane_sum = e[:, :LANES]
            for c in range(1, NA_BAND_KEYS // LANES):
                lane_sum = lane_sum + e[:, c * LANES:(c + 1) * LANES]
            a_ref[slot][rr, :, :LANES] = _dot(e.astype(BF16), v_ref[keys, :])
            a_ref[slot][rr, :, LANES:] = lane_sum

    def finish_stage(item, slot):
        for rr in range(NA_ITEM_ROWS):
            r, base = locate(item, rr)
            acc = a_ref[slot][rr]
            o_st = acc[:, :LANES] * (1.0 / jnp.sum(acc[:, LANES:], axis=-1, keepdims=True))
            o = jnp.where(low, o_st[:GRID_W], o_st[GRID_W:])
            o_ref[query_rows(r, base), :] = o.astype(BF16)

    _software_pipeline(nbatch * items_per_batch, (logits_stage, value_stage, finish_stage),
                       order=(1, 0, 2))


def _na(proj, rpb_flat, batch, seq):
    rows = seq // GRID_W
    items_per_batch = rows // NA_ITEM_ROWS
    nbatch = math.gcd(batch, NA_BATCH_CHUNK)
    assert rows >= NA_WIN_ROWS and NA_WIN_ROWS % 2 == 0 and rows % NA_ITEM_ROWS == 0
    assert items_per_batch & (items_per_batch - 1) == 0
    pairs = NA_HEADS // 2
    blk = lambda off: pl.BlockSpec((nbatch * seq, LANES), lambda hp, b, off=off: (b, off + hp))
    return pl.pallas_call(
        functools.partial(_na_body, rows=rows, nbatch=nbatch),
        grid=(pairs, batch // nbatch),
        in_specs=[
            pl.BlockSpec(memory_space=pltpu.SMEM),
            blk(O_NQ // LANES), blk(O_NK // LANES), blk(O_NV // LANES),
        ],
        out_specs=pl.BlockSpec((nbatch * seq, LANES), lambda hp, b: (b, hp)),
        out_shape=jax.ShapeDtypeStruct((batch * seq, NA_WIDTH), BF16),
        scratch_shapes=[
            pltpu.VMEM((2, RPB_ROWS, GRID_W, LANES), F32),
            pltpu.VMEM((NA_WIN_ROWS, 2 * GRID_W, NA_BAND_KEYS), F32),
            pltpu.VMEM((NA_ITEM_ROWS, 2 * GRID_W, NA_BAND_KEYS), F32),
            pltpu.VMEM((NA_ITEM_ROWS, 2 * GRID_W, NA_BAND_KEYS), F32),
            pltpu.VMEM((NA_ITEM_ROWS, 2 * GRID_W, LANES), F32),
            pltpu.VMEM((NA_ITEM_ROWS, 2 * GRID_W, LANES), F32),
            pltpu.VMEM((NA_ITEM_ROWS, 2 * GRID_W, 2 * LANES), F32),
            pltpu.VMEM((NA_ITEM_ROWS, 2 * GRID_W, 2 * LANES), F32),
        ],
        compiler_params=pltpu.CompilerParams(
            dimension_semantics=("arbitrary", "arbitrary"), vmem_limit_bytes=V7X_VMEM_LIMIT_BYTES),
        name="na_attn",
    )(rpb_flat, proj, proj, proj)


def _diff_body(slope_ref, lq1_ref, lk1_ref, lq2_ref, lk2_ref, sub_ref, q_ref, k_ref, v_ref, o_ref,
               dist_ref, v1_ref, kt_ref, s0_ref, s1_ref, m0_ref, m1_ref, a0_ref, a1_ref,
               *, lam_init, seq, nbatch):
    s_ref, m_ref, a_ref = (s0_ref, s1_ref), (m0_ref, m1_ref), (a0_ref, a1_ref)
    h = pl.program_id(0)
    b = pl.program_id(1)
    nq = seq // DIFF_TQ

    @pl.when(b == 0)
    def _build_dist():
        slope = slope_ref[h]
        rel = (lax.broadcasted_iota(jnp.int32, (DIFF_TQ, DIFF_TQ), 0)
               - lax.broadcasted_iota(jnp.int32, (DIFF_TQ, DIFF_TQ), 1))
        for d in range(2 * nq - 1):
            off = (d - (nq - 1)) * DIFF_TQ
            dist_ref[d] = slope * jnp.abs((rel + off).astype(F32))

    lam = (jnp.exp(jnp.sum(lq1_ref[...] * lk1_ref[...], axis=-1, keepdims=True))
           - jnp.exp(jnp.sum(lq2_ref[...] * lk2_ref[...], axis=-1, keepdims=True))
           + lam_init)
    lane = lax.broadcasted_iota(jnp.int32, (DIFF_TQ, LANES), 1)
    first_map = lane < DIFF_QK_DIM
    nq_shift = nq.bit_length() - 1

    def q_rows(j):
        return pl.ds(_aligned(j * DIFF_TQ, DIFF_TQ), DIFF_TQ)

    def batch_rows(j):
        bl = j // nq if isinstance(j, int) else lax.shift_right_logical(j, nq_shift)
        return pl.ds(_aligned(bl * seq, seq), seq)

    def logits_stage(j, slot):
        i = j % nq if isinstance(j, int) else j & (nq - 1)
        q = q_ref[q_rows(j), :]
        zero = jnp.zeros_like(q)
        k_t = kt_ref[j // nq if isinstance(j, int) else lax.shift_right_logical(j, nq_shift)]
        dist = jnp.concatenate([dist_ref[i - c + (nq - 1)] for c in range(nq)], axis=1)
        for m, qm in enumerate((jnp.where(first_map, q, zero), jnp.where(first_map, zero, q))):
            s = _dot(qm, k_t) - dist
            s_ref[slot][m] = s
            lane_max = s[:, :LANES]
            for c in range(1, seq // LANES):
                lane_max = jnp.maximum(lane_max, s[:, c * LANES:(c + 1) * LANES])
            m_ref[slot][m] = lane_max

    def value_stage(j, slot):
        e = jnp.concatenate(
            [jnp.exp(s_ref[slot][m] - jnp.max(m_ref[slot][m], axis=-1, keepdims=True)).astype(BF16)
             for m in range(2)], axis=0)
        a_ref[slot][...] = _dot(e, v1_ref[batch_rows(j), :])

    def finish_stage(j, slot):
        acc = a_ref[slot][...]
        num1, den1 = acc[:DIFF_TQ, :DIFF_V_DIM], acc[:DIFF_TQ, DIFF_V_DIM:DIFF_V_DIM + 1]
        num2, den2 = acc[DIFF_TQ:, :DIFF_V_DIM], acc[DIFF_TQ:, DIFF_V_DIM:DIFF_V_DIM + 1]
        o = num1 * (1.0 / den1) - (lam * (1.0 / den2)) * num2
        o = _rmsnorm_f32(o, sub_ref[...]) * (1.0 - lam_init)
        o_ref[q_rows(j), :] = o.astype(BF16)

    v = v_ref[...]
    v1_ref[:, :DIFF_V_DIM] = v
    v1_ref[:, DIFF_V_DIM:] = jnp.where(
        lax.broadcasted_iota(jnp.int32, v.shape, 1) == 0, 1.0, 0.0).astype(BF16)
    for bl in range(nbatch):
        kt_ref[bl] = k_ref[bl * seq:(bl + 1) * seq, :].T
    _software_pipeline(nbatch * nq, (logits_stage, value_stage, finish_stage), order=(1, 0, 2))


def _diff(proj, slopes, lq1, lk1, lq2, lk2, subln, batch, seq, lam_init):
    nq = seq // DIFF_TQ
    nbatch = math.gcd(batch, DIFF_BATCH_CHUNK)
    assert nq & (nq - 1) == 0
    vec = lambda w: pl.BlockSpec((1, w), lambda h, b: (0, 0))
    blk = lambda off: pl.BlockSpec((nbatch * seq, LANES), lambda h, b, off=off: (b, off + h))
    return pl.pallas_call(
        functools.partial(_diff_body, lam_init=lam_init, seq=seq, nbatch=nbatch),
        grid=(DIFF_HEADS, batch // nbatch),
        in_specs=[
            pl.BlockSpec(memory_space=pltpu.SMEM),
            vec(DIFF_QK_DIM), vec(DIFF_QK_DIM), vec(DIFF_QK_DIM), vec(DIFF_QK_DIM), vec(DIFF_V_DIM),
            blk(O_DQ // LANES), blk(O_DK // LANES), blk(O_DV // LANES),
        ],
        out_specs=pl.BlockSpec((nbatch * seq, LANES), lambda h, b: (b, h)),
        out_shape=jax.ShapeDtypeStruct((batch * seq, DIFF_V_WIDTH), BF16),
        scratch_shapes=[
            pltpu.VMEM((2 * nq - 1, DIFF_TQ, DIFF_TQ), F32),
            pltpu.VMEM((nbatch * seq, 2 * LANES), BF16),
            pltpu.VMEM((nbatch, LANES, seq), BF16),
            pltpu.VMEM((2, DIFF_TQ, seq), F32), pltpu.VMEM((2, DIFF_TQ, seq), F32),
            pltpu.VMEM((2, DIFF_TQ, LANES), F32), pltpu.VMEM((2, DIFF_TQ, LANES), F32),
            pltpu.VMEM((2 * DIFF_TQ, 2 * LANES), F32), pltpu.VMEM((2 * DIFF_TQ, 2 * LANES), F32),
        ],
        compiler_params=pltpu.CompilerParams(
            dimension_semantics=("arbitrary", "arbitrary"),
            vmem_limit_bytes=V7X_VMEM_LIMIT_BYTES),
        name="diff_attn",
    )(slopes, lq1, lk1, lq2, lk2, subln, proj, proj, proj)


def _merge_ffn_body(x_ref, mq_ref, ona_ref, odf_ref, mk_ref, mv_ref, g_ref, wgate_hbm, bgate_ref,
                    wna_hbm, wdf_hbm, wmem_hbm, wout_hbm, fg_ref, wg_hbm, wu_hbm, wd_hbm, ng_ref,
                    o_ref, wgate_ref, wna_ref, wdf_ref, wmem_ref, wout_ref, wg_ref, wu_ref, wd_ref,
                    stage_ref, sem_ref, *, final_norm):
    @pl.when(pl.program_id(0) == 0)
    def _load_weights():
        _load_weights_bf16(
            ((wgate_hbm, wgate_ref), (wna_hbm, wna_ref), (wdf_hbm, wdf_ref), (wmem_hbm, wmem_ref),
             (wout_hbm, wout_ref), (wg_hbm, wg_ref), (wu_hbm, wu_ref), (wd_hbm, wd_ref)),
            stage_ref, sem_ref)

    x = x_ref[...]
    h, rowscale = _prenorm(x, g_ref[...])
    heads = []
    for hh in range(MEM_HEADS):
        sl = slice(hh * MEM_HEAD_DIM, (hh + 1) * MEM_HEAD_DIM)
        logits = _dot_nt(mq_ref[:, sl], mk_ref[:, sl]) * (MEM_HEAD_DIM ** -0.5)
        p = _softmax_rows(logits).astype(BF16)
        heads.append(_dot(p, mv_ref[:, sl]))
    o_mem = jnp.concatenate(heads, axis=-1).astype(BF16)
    branches = (
        _dot(ona_ref[...], wna_ref[...]),
        _dot(odf_ref[...], wdf_ref[...]),
        _dot(o_mem, wmem_ref[...]),
    )
    merged = jnp.zeros(x.shape, F32)
    for i, y in enumerate(branches):
        sl = slice(i * D_MODEL, (i + 1) * D_MODEL)
        gate = jax.nn.sigmoid(_dot(h, wgate_ref[:, sl]) * rowscale + bgate_ref[:, sl])
        merged = merged + gate * y
    x2 = x + _dot(merged.astype(BF16), wout_ref[...])
    y = _swiglu_half_step(x2, fg_ref, wg_ref, wu_ref, wd_ref)
    if final_norm:
        y = _rmsnorm_f32(y, ng_ref[...])
    o_ref[...] = y


def _merge_ffn(x1, proj, o_na, o_diff, mk, mv, norm_g, w_gate, b_gate, w_na, w_df, w_mem, w_out,
               ffn_g, wg, wu, wd, final_g, seq, m_tokens, *, final_norm):
    n = x1.shape[0]
    per_b = seq // MERGE_TM
    row = lambda i: (i, 0)
    return pl.pallas_call(
        functools.partial(_merge_ffn_body, final_norm=final_norm),
        grid=(n // MERGE_TM,),
        in_specs=[
            pl.BlockSpec((MERGE_TM, D_MODEL), row),
            pl.BlockSpec((MERGE_TM, MEM_WIDTH), lambda i: (i, O_MQ // MEM_WIDTH)),
            pl.BlockSpec((MERGE_TM, NA_WIDTH), row),
            pl.BlockSpec((MERGE_TM, DIFF_V_WIDTH), row),
            pl.BlockSpec((m_tokens, MEM_WIDTH), lambda i: (i // per_b, 0)),
            pl.BlockSpec((m_tokens, MEM_WIDTH), lambda i: (i // per_b, 0)),
            _resident((1, D_MODEL)),
            _HBM,
            _resident((1, 3 * D_MODEL)),
            _HBM, _HBM, _HBM, _HBM,
            _resident((1, D_MODEL)),
            _HBM, _HBM, _HBM,
            _resident((1, D_MODEL)),
        ],
        out_specs=pl.BlockSpec((MERGE_TM, D_MODEL), row),
        out_shape=jax.ShapeDtypeStruct((n, D_MODEL), F32),
        scratch_shapes=[
            pltpu.VMEM((D_MODEL, 3 * D_MODEL), BF16),
            pltpu.VMEM((NA_WIDTH, D_MODEL), BF16), pltpu.VMEM((DIFF_V_WIDTH, D_MODEL), BF16),
            pltpu.VMEM((MEM_WIDTH, D_MODEL), BF16), pltpu.VMEM((D_MODEL, D_MODEL), BF16),
            pltpu.VMEM((D_MODEL, D_FF), BF16), pltpu.VMEM((D_MODEL, D_FF), BF16),
            pltpu.VMEM((D_FF, D_MODEL), BF16),
            pltpu.VMEM((STAGE_SLOTS, STAGE_ROWS, 3 * D_MODEL), F32),
            pltpu.SemaphoreType.DMA((STAGE_SLOTS,)),
        ],
        compiler_params=pltpu.CompilerParams(
            dimension_semantics=("arbitrary",), vmem_limit_bytes=V7X_VMEM_LIMIT_BYTES),
        name="merge_ffn",
    )(x1, proj, o_na, o_diff, mk, mv, norm_g, w_gate, b_gate, w_na, w_df, w_mem, w_out,
      ffn_g, wg, wu, wd, final_g)


def kernel(x, mem, ffn1_norm, ffn1_w_gate, ffn1_w_up, ffn1_w_down, mix_norm, w_in, na_rpb,
           diff_lambda_q1, diff_lambda_k1, diff_lambda_q2, diff_lambda_k2, diff_subln,
           mem_norm, w_mem_kv, w_gate, b_gate, w_br_na, w_br_diff, w_br_mem, w_out,
           ffn2_norm, ffn2_w_gate, ffn2_w_up, ffn2_w_down, final_norm):
    batch, seq, d_model = x.shape
    m_tokens = mem.shape[1]
    depth = ffn1_norm.shape[0]
    assert d_model == D_MODEL and seq % GRID_W == 0
    assert seq % DIFF_TQ == 0 and seq % MERGE_TM == 0 and (batch * seq) % FFN_TM == 0
    slopes = jnp.asarray([2.0 ** (-8.0 * (i + 1) / DIFF_HEADS) for i in range(DIFF_HEADS)], F32)
    w32 = lambda w: w.astype(F32)
    vec = lambda v: v.reshape(1, -1).astype(F32)

    xt = x.reshape(batch * seq, d_model)
    mem2d = mem.reshape(batch * m_tokens, d_model)
    for l in range(depth):
        lam_init = 0.8 - 0.6 * math.exp(-0.3 * l)
        xt, proj = _ffn_proj(xt, vec(ffn1_norm[l]), w32(ffn1_w_gate[l]), w32(ffn1_w_up[l]),
                             w32(ffn1_w_down[l]), vec(mix_norm[l]), w32(w_in[l]))
        mk, mv = _memkv(mem2d, vec(mem_norm[l]), w32(w_mem_kv[l]))
        o_na = _na(proj, na_rpb[l].reshape(-1).astype(F32), batch, seq)
        o_diff = _diff(proj, slopes, vec(diff_lambda_q1[l]), vec(diff_lambda_k1[l]),
                       vec(diff_lambda_q2[l]), vec(diff_lambda_k2[l]), vec(diff_subln[l]),
                       batch, seq, lam_init)
        xt = _merge_ffn(xt, proj, o_na, o_diff, mk, mv, vec(mix_norm[l]), w32(w_gate[l]),
                        vec(b_gate[l]), w32(w_br_na[l]), w32(w_br_diff[l]), w32(w_br_mem[l]),
                        w32(w_out[l]), vec(ffn2_norm[l]), w32(ffn2_w_gate[l]), w32(ffn2_w_up[l]),
                        w32(ffn2_w_down[l]), vec(final_norm), seq, m_tokens,
                        final_norm=(l == depth - 1))
    return xt.reshape(batch, seq, d_model)
```
